```python
import math, functools
import jax, jax.numpy as jnp
from jax import lax
import numpy as np

D_MODEL = 2048
BATCH = 8
SEQ = 2048
DEPTH = 2
DEC_BATCH = 1
DEC_SEQ = 8192
PAST_LEN = 128

N_MIXERS = 2
N_A_LAYERS = (DEPTH + 1) // 2
N_B_LAYERS = DEPTH // 2
GDN_DK = 128
GDN_DV = 128
GDN_HK = D_MODEL // 128
GDN_HV = 2 * GDN_HK
GDN_KD = GDN_HK * GDN_DK
GDN_VD = GDN_HV * GDN_DV
GDN_CONV = 5
GDN_CHUNK = 64
GDN_PROJ = 2 * GDN_KD + 2 * GDN_VD + 4 * GDN_HV
RWKV_N = 64
RWKV_H = D_MODEL // RWKV_N
RWKV_DECAY_LORA = max(32, int(round(1.8 * D_MODEL ** 0.5 / 32)) * 32)
RWKV_A_LORA = max(32, int(round(1.8 * D_MODEL ** 0.5 / 32)) * 32)
RWKV_GATE_LORA = max(32, int(round(0.6 * D_MODEL ** 0.8 / 32)) * 32)
RWKV_GN_EPS = 64e-5
D_FF = 4 * D_MODEL
PLE_DIM = 256
DN_ALPHA = (2 * DEPTH) ** 0.25
DN_BETA = (8 * DEPTH) ** -0.25
LN_EPS = 1e-5

kernel_name = "hybrid_gdn_rwkv7_deepnorm_encoder"


def _layernorm(x, g, b):
    xf = x.astype(jnp.float32)
    mu = jnp.mean(xf, -1, keepdims=True)
    var = jnp.mean(jnp.square(xf - mu), -1, keepdims=True)
    return ((xf - mu) * lax.rsqrt(var + LN_EPS) * g.astype(jnp.float32) + b.astype(jnp.float32)).astype(x.dtype)


def _l2norm(t):
    t = t.astype(jnp.float32)
    return t * lax.rsqrt(jnp.sum(t * t, -1, keepdims=True) + 1e-6)


def _centred_dwconv(x, w):
    width, ch = w.shape
    pad = (width - 1) // 2
    return lax.conv_general_dilated(x, w[:, None, :], window_strides=(1,), padding=[(pad, pad)],
                                    dimension_numbers=("NWC", "WIO", "NWC"), feature_group_count=ch)


def _gated_delta_rule_chunked(q, k, v, g, beta):
    B, T, H, DK = q.shape
    DV = v.shape[-1]
    C = GDN_CHUNK
    NC = T // C

    def blocks(t):
        t = t.astype(jnp.float32).reshape(B, NC, C, H, *t.shape[3:])
        return jnp.moveaxis(jnp.moveaxis(t, 3, 2), 1, 0)

    q, k, v, g, beta = (blocks(t) for t in (q, k, v, g, beta))
    gc = jnp.cumsum(g, axis=-1)
    idx = jnp.arange(C)
    incl = idx[:, None] >= idx[None, :]
    strict = idx[:, None] > idx[None, :]
    diff = gc[..., :, None] - gc[..., None, :]
    decay = jnp.where(incl, jnp.exp(jnp.where(incl, diff, 0.0)), 0.0)
    kb = k * beta[..., None]
    a_kk = jnp.where(strict, jnp.einsum("nbhcd,nbhsd->nbhcs", kb, k) * decay, 0.0)
    eye = jnp.eye(C, dtype=jnp.float32)
    t_inv = lax.linalg.triangular_solve(a_kk + eye, jnp.broadcast_to(eye, a_kk.shape),
                                        left_side=True, lower=True, unit_diagonal=True)
    u = jnp.einsum("nbhcs,nbhse->nbhce", t_inv, v * beta[..., None])
    w = jnp.einsum("nbhcs,nbhsd->nbhcd", t_inv, kb * jnp.exp(gc)[..., None])
    a_qk = jnp.einsum("nbhcd,nbhsd->nbhcs", q, k) * decay
    q_dec = q * jnp.exp(gc)[..., None]
    g_last = gc[..., -1]
    k_dec = k * jnp.exp(g_last[..., None] - gc)[..., None]

    def step(S, xs):
        q_n, w_n, u_n, aqk_n, k_n, gl_n = xs
        v_new = u_n - jnp.einsum("bhcd,bhde->bhce", w_n, S)
        o = jnp.einsum("bhcd,bhde->bhce", q_n, S) + jnp.einsum("bhcs,bhse->bhce", aqk_n, v_new)
        S = S * jnp.exp(gl_n)[..., None, None] + jnp.einsum("bhcd,bhce->bhde", k_n, v_new)
        return S, o

    S0 = jnp.zeros((B, H, DK, DV), jnp.float32)
    _, o = lax.scan(step, S0, (q_dec, w, u, a_qk, k_dec, g_last))
    o = jnp.moveaxis(o, 0, 1)
    return jnp.swapaxes(o, 2, 3).reshape(B, T, H, DV)


def _gdn_mixer(x, w_in, conv_w, a_log, dt_bias, norm_w, w_out):
    B, T, _ = x.shape
    proj = x @ w_in
    qkv, z, gates = jnp.split(proj, [2 * GDN_KD + GDN_VD, 2 * GDN_KD + 2 * GDN_VD], axis=-1)
    qkv = jax.nn.silu(_centred_dwconv(qkv, conv_w))
    q, k, v = jnp.split(qkv, [GDN_KD, 2 * GDN_KD], axis=-1)
    rep = GDN_HV // GDN_HK
    q = jnp.repeat(_l2norm(q.reshape(B, T, GDN_HK, GDN_DK)) * GDN_DK ** -0.5, rep, axis=2)
    k = jnp.repeat(_l2norm(k.reshape(B, T, GDN_HK, GDN_DK)), rep, axis=2)
    v = v.reshape(B, T, GDN_HV, GDN_DV)
    a_f, a_b, b_f, b_b = jnp.split(gates.astype(jnp.float32), 4, axis=-1)

    def log_decay(a, d):
        return -jnp.exp(a_log[d].astype(jnp.float32)) * jax.nn.softplus(a + dt_bias[d].astype(jnp.float32))

    rev = lambda t: t[:, ::-1]
    o_f = _gated_delta_rule_chunked(q, k, v, log_decay(a_f, 0), jax.nn.sigmoid(b_f))
    o_b = rev(_gated_delta_rule_chunked(rev(q), rev(k), rev(v), rev(log_decay(a_b, 1)), rev(jax.nn.sigmoid(b_b))))
    o = o_f + o_b
    o = o * lax.rsqrt(jnp.mean(o * o, -1, keepdims=True) + 1e-6) * norm_w.astype(jnp.float32)
    o = o * jax.nn.silu(z.astype(jnp.float32).reshape(B, T, GDN_HV, GDN_DV))
    return o.reshape(B, T, GDN_VD).astype(x.dtype) @ w_out


def _rwkv7_scan(r, log_w, k, v, a, b):
    B, T, H, N = r.shape
    xs = tuple(jnp.moveaxis(t.astype(jnp.float32), 1, 0) for t in (r, jnp.exp(log_w.astype(jnp.float32)), k, v, a, b))

    def step(S, xs_t):
        r_t, w_t, k_t, v_t, a_t, b_t = xs_t
        sa = jnp.einsum("bhij,bhj->bhi", S, a_t)
        S = S * w_t[:, :, None, :] + sa[..., None] * b_t[:, :, None, :] + v_t[..., None] * k_t[:, :, None, :]
        return S, jnp.einsum("bhij,bhj->bhi", S, r_t)

    _, y = lax.scan(step, jnp.zeros((B, H, N, N), jnp.float32), xs)
    return jnp.moveaxis(y, 0, 1)


def _rwkv7_direction(r, k, v, kk, xw, xa, w0, w1, w2, a0, a1, a2, k_a, r_k, reverse):
    B, T, H, N = r.shape
    heads = lambda t: t.reshape(B, T, H, N)
    log_w = heads(-jnp.exp(-jax.nn.softplus(-(w0 + jnp.tanh(xw @ w1) @ w2)) - 0.5))
    a = jax.nn.sigmoid(a0 + (xa @ a1) @ a2)
    k_d = heads(k * (1 + (a - 1) * k_a))
    a = heads(a)
    rev = (lambda t: t[:, ::-1]) if reverse else (lambda t: t)
    y = rev(_rwkv7_scan(rev(r), rev(log_w), rev(k_d), rev(v), rev(-kk), rev(kk * a)))
    bonus = jnp.sum(r * k_d * r_k, axis=-1, keepdims=True) * v
    return y, bonus


def _rwkv7_mixer(x, mix, w_rkv, w0, w1, w2, a0, a1, a2, g1, g2, k_k, k_a, r_k, ln_w, ln_b, w_o):
    B, T, D = x.shape
    heads = lambda t: t.reshape(B, T, RWKV_H, RWKV_N)
    x_prev = jnp.pad(x[:, :-1], ((0, 0), (1, 0), (0, 0)))
    x_next = jnp.pad(x[:, 1:], ((0, 0), (0, 1), (0, 0)))
    xx = 0.5 * (x_prev + x_next) - x
    xr, xw, xk, xv, xa, xg = (x + xx * mix[m] for m in range(6))
    r = heads(xr @ w_rkv[0])
    k = xk @ w_rkv[1]
    v = heads(xv @ w_rkv[2])
    g = jax.nn.sigmoid(xg @ g1) @ g2
    kk = _l2norm(heads(k * k_k))
    y_f, bonus_f = _rwkv7_direction(r, k, v, kk, xw, xa, w0[0], w1[0], w2[0], a0[0], a1[0], a2[0], k_a, r_k, reverse=False)
    y_b, bonus_b = _rwkv7_direction(r, k, v, kk, xw, xa, w0[1], w1[1], w2[1], a0[1], a1[1], a2[1], k_a, r_k, reverse=True)
    wkv = y_f + y_b
    mu = jnp.mean(wkv, -1, keepdims=True)
    var = jnp.mean(jnp.square(wkv - mu), -1, keepdims=True)
    o = ((wkv - mu) * lax.rsqrt(var + RWKV_GN_EPS)).reshape(B, T, D) * ln_w.astype(jnp.float32) + ln_b.astype(jnp.float32)
    o = o + (bonus_f + bonus_b).reshape(B, T, D).astype(jnp.float32)
    return (o.astype(x.dtype) * g) @ w_o


def _sqrelu_mlp(x, w_up, w_down):
    return jnp.square(jax.nn.relu(x @ w_up)) @ w_down


def _trunk(x, p, gdn_w_in, gdn_conv, gdn_a_log, gdn_dt_bias, gdn_norm, gdn_w_out,
           rwkv_mix, rwkv_w_rkv, rwkv_w0, rwkv_w1, rwkv_w2, rwkv_a0, rwkv_a1, rwkv_a2,
           rwkv_g1, rwkv_g2, rwkv_k_k, rwkv_k_a, rwkv_r_k, rwkv_ln_w, rwkv_ln_b, rwkv_w_o,
           ln_g, ln_b, mlp_w_up, mlp_w_down, ple_w_proj, ple_w_gate):
    for i in range(DEPTH):
        j = i // N_MIXERS
        if i % N_MIXERS == 0:
            h = _gdn_mixer(x, gdn_w_in[j], gdn_conv[j], gdn_a_log[j], gdn_dt_bias[j], gdn_norm[j], gdn_w_out[j])
        else:
            h = _rwkv7_mixer(x, rwkv_mix[j], rwkv_w_rkv[j], rwkv_w0[j], rwkv_w1[j], rwkv_w2[j],
                             rwkv_a0[j], rwkv_a1[j], rwkv_a2[j], rwkv_g1[j], rwkv_g2[j],
                             rwkv_k_k[j], rwkv_k_a[j], rwkv_r_k[j], rwkv_ln_w[j], rwkv_ln_b[j], rwkv_w_o[j])
        x = _layernorm(DN_ALPHA * x + h, ln_g[i, 0], ln_b[i, 0])
        x = _layernorm(DN_ALPHA * x + _sqrelu_mlp(x, mlp_w_up[i], mlp_w_down[i]), ln_g[i, 1], ln_b[i, 1])
        x = x + jax.nn.sigmoid(x @ ple_w_gate[i]) * (p[i] @ ple_w_proj[i])
    return x


def setup_inputs(seed: int = 0) -> dict:
    key = jax.random.key(seed)
    ks = iter(jax.random.split(key, 48))
    nrm = lambda shape, scale=1.0: scale * jax.random.normal(next(ks), shape, jnp.float32)
    uni = lambda shape, lo, hi: jax.random.uniform(next(ks), shape, jnp.float32, lo, hi)
    dt = jnp.exp(uni((N_A_LAYERS, 2, GDN_HV), math.log(1e-3), math.log(1e-1)))
    return {
        "x_prompt": nrm((BATCH, SEQ, D_MODEL)),
        "x_sample": nrm((DEC_BATCH, DEC_SEQ, D_MODEL)),
        "p_prompt": nrm((DEPTH, BATCH, SEQ, PLE_DIM)),
        "p_sample": nrm((DEPTH, DEC_BATCH, DEC_SEQ, PLE_DIM)),
        "gdn_w_in": nrm((N_A_LAYERS, D_MODEL, GDN_PROJ), D_MODEL ** -0.5),
        "gdn_conv": nrm((N_A_LAYERS, GDN_CONV, 2 * GDN_KD + GDN_VD), GDN_CONV ** -0.5),
        "gdn_a_log": jnp.log(uni((N_A_LAYERS, 2, GDN_HV), 1.0, 16.0)),
        "gdn_dt_bias": dt + jnp.log(-jnp.expm1(-dt)),
        "gdn_norm": 1.0 + nrm((N_A_LAYERS, GDN_DV), 0.1),
        "gdn_w_out": nrm((N_A_LAYERS, GDN_VD, D_MODEL), DN_BETA * GDN_VD ** -0.5),
        "rwkv_mix": uni((N_B_LAYERS, 6, D_MODEL), 0.0, 1.0),
        "rwkv_w_rkv": nrm((N_B_LAYERS, 3, D_MODEL, D_MODEL), D_MODEL ** -0.5),
        "rwkv_w0": uni((N_B_LAYERS, 2, D_MODEL), -6.0, -1.0),
        "rwkv_w1": nrm((N_B_LAYERS, 2, D_MODEL, RWKV_DECAY_LORA), D_MODEL ** -0.5),
        "rwkv_w2": nrm((N_B_LAYERS, 2, RWKV_DECAY_LORA, D_MODEL), 0.5 * RWKV_DECAY_LORA ** -0.5),
        "rwkv_a0": nrm((N_B_LAYERS, 2, D_MODEL), 0.1),
        "rwkv_a1": nrm((N_B_LAYERS, 2, D_MODEL, RWKV_A_LORA), D_MODEL ** -0.5),
        "rwkv_a2": nrm((N_B_LAYERS, 2, RWKV_A_LORA, D_MODEL), 0.5 * RWKV_A_LORA ** -0.5),
        "rwkv_g1": nrm((N_B_LAYERS, D_MODEL, RWKV_GATE_LORA), D_MODEL ** -0.5),
        "rwkv_g2": nrm((N_B_LAYERS, RWKV_GATE_LORA, D_MODEL), RWKV_GATE_LORA ** -0.5),
        "rwkv_k_k": 0.85 + nrm((N_B_LAYERS, D_MODEL), 0.05),
        "rwkv_k_a": 1.0 + nrm((N_B_LAYERS, D_MODEL), 0.05),
        "rwkv_r_k": nrm((N_B_LAYERS, RWKV_H, RWKV_N), 0.1),
        "rwkv_ln_w": 1.0 + nrm((N_B_LAYERS, D_MODEL), 0.1),
        "rwkv_ln_b": nrm((N_B_LAYERS, D_MODEL), 0.01),
        "rwkv_w_o": nrm((N_B_LAYERS, D_MODEL, D_MODEL), DN_BETA * D_MODEL ** -0.5),
        "ln_g": 1.0 + nrm((DEPTH, 2, D_MODEL), 0.1),
        "ln_b": nrm((DEPTH, 2, D_MODEL), 0.01),
        "mlp_w_up": nrm((DEPTH, D_MODEL, D_FF), D_MODEL ** -0.5),
        "mlp_w_down": nrm((DEPTH, D_FF, D_MODEL), DN_BETA * D_FF ** -0.5),
        "ple_w_proj": nrm((DEPTH, PLE_DIM, D_MODEL), PLE_DIM ** -0.5),
        "ple_w_gate": nrm((DEPTH, D_MODEL, D_MODEL), D_MODEL ** -0.5),
    }


def reference(x_prompt, x_sample, p_prompt, p_sample, gdn_w_in, gdn_conv, gdn_a_log, gdn_dt_bias, gdn_norm,
              gdn_w_out, rwkv_mix, rwkv_w_rkv, rwkv_w0, rwkv_w1, rwkv_w2, rwkv_a0, rwkv_a1, rwkv_a2,
              rwkv_g1, rwkv_g2, rwkv_k_k, rwkv_k_a, rwkv_r_k, rwkv_ln_w, rwkv_ln_b, rwkv_w_o,
              ln_g, ln_b, mlp_w_up, mlp_w_down, ple_w_proj, ple_w_gate):
    trunk = functools.partial(
        _trunk, gdn_w_in=gdn_w_in, gdn_conv=gdn_conv, gdn_a_log=gdn_a_log, gdn_dt_bias=gdn_dt_bias,
        gdn_norm=gdn_norm, gdn_w_out=gdn_w_out, rwkv_mix=rwkv_mix, rwkv_w_rkv=rwkv_w_rkv,
        rwkv_w0=rwkv_w0, rwkv_w1=rwkv_w1, rwkv_w2=rwkv_w2, rwkv_a0=rwkv_a0, rwkv_a1=rwkv_a1,
        rwkv_a2=rwkv_a2, rwkv_g1=rwkv_g1, rwkv_g2=rwkv_g2, rwkv_k_k=rwkv_k_k, rwkv_k_a=rwkv_k_a,
        rwkv_r_k=rwkv_r_k, rwkv_ln_w=rwkv_ln_w, rwkv_ln_b=rwkv_ln_b, rwkv_w_o=rwkv_w_o,
        ln_g=ln_g, ln_b=ln_b, mlp_w_up=mlp_w_up, mlp_w_down=mlp_w_down,
        ple_w_proj=ple_w_proj, ple_w_gate=ple_w_gate)
    y_prompt = trunk(x_prompt, p_prompt)
    y_sample = trunk(x_sample, p_sample)
    return (y_prompt, y_sample)
```

```python
import functools
import math

import jax
import jax.numpy as jnp
from jax import lax
from jax.experimental import pallas as pl
from jax.experimental.pallas import tpu as pltpu

F32 = jnp.float32
BF16 = jnp.bfloat16
HIGHEST = lax.Precision.HIGHEST

LANES = 128
HALO = 8
CHUNK = 64
TIME_TILE = 128
INV_BASE = 8
VMEM_LIMIT = 56 * 1024 * 1024

LN_EPS = 1e-5
RWKV_GN_EPS = 64e-5
RWKV_N = 64
GDN_CONV = 5


def _cparams(sem):
    return pltpu.CompilerParams(dimension_semantics=sem, vmem_limit_bytes=VMEM_LIMIT)


def _mm_kernel(*refs, na, nrk, ne, nrn, nout, nk, prologue, epilogue):
    a_refs = refs[:na]
    rk_refs = refs[na:na + nrk]
    w_ref = refs[na + nrk]
    e_refs = refs[na + nrk + 1:na + nrk + 1 + ne]
    rn_refs = refs[na + nrk + 1 + ne:na + nrk + 1 + ne + nrn]
    out_refs = refs[na + nrk + 1 + ne + nrn:na + nrk + 1 + ne + nrn + nout]

    a = prologue(*[r[...] for r in a_refs], *[r[...] for r in rk_refs])
    part = jnp.dot(a, w_ref[...], preferred_element_type=F32)

    def finish(acc):
        outs = epilogue(acc, *[r[...] for r in e_refs], *[r[...] for r in rn_refs])
        for o_ref, o in zip(out_refs, outs):
            o_ref[...] = o.astype(o_ref.dtype)

    if nk == 1:
        finish(part)
        return

    acc_ref = refs[-1]
    k = pl.program_id(2)

    @pl.when(k == 0)
    def _():
        acc_ref[...] = part

    @pl.when(k > 0)
    def _():
        acc_ref[...] += part

    @pl.when(k == nk - 1)
    def _():
        finish(acc_ref[...])


def _cast_bf16(a):
    return a.astype(BF16)


def _identity_epilogue(acc):
    return (acc,)


def _fused_matmul(a_ins, w, *, a_col0=None, rowk_ins=(), e_ins=(), rown_ins=(), prologue=_cast_bf16,
                  epilogue=_identity_epilogue, out_dtypes=(F32,), tm=512, tn=512, tk=512, name="mm"):
    m = a_ins[0].shape[0]
    kdim, n = w.shape
    tm, tn, tk = min(tm, m), min(tn, n), min(tk, kdim)
    assert m % tm == 0 and n % tn == 0 and kdim % tk == 0, (m, n, kdim, tm, tn, tk)
    nk = kdim // tk
    a_col0 = [0] * len(a_ins) if a_col0 is None else a_col0
    assert all(c0 % tk == 0 for c0 in a_col0)

    def a_spec(c0):
        return pl.BlockSpec((tm, tk), lambda i, j, k: (i, c0 // tk + k))

    in_specs = ([a_spec(c0) for c0 in a_col0]
                + [pl.BlockSpec((1, tk), lambda i, j, k: (0, k)) for _ in rowk_ins]
                + [pl.BlockSpec((tk, tn), lambda i, j, k: (k, j))]
                + [pl.BlockSpec((tm, tn), lambda i, j, k: (i, j)) for _ in e_ins]
                + [pl.BlockSpec((1, tn), lambda i, j, k: (0, j)) for _ in rown_ins])
    out_specs = [pl.BlockSpec((tm, tn), lambda i, j, k: (i, j)) for _ in out_dtypes]
    out_shape = [jax.ShapeDtypeStruct((m, n), dt) for dt in out_dtypes]
    kern = functools.partial(_mm_kernel, na=len(a_ins), nrk=len(rowk_ins), ne=len(e_ins), nrn=len(rown_ins),
                             nout=len(out_dtypes), nk=nk, prologue=prologue, epilogue=epilogue)
    outs = pl.pallas_call(
        kern,
        grid=(m // tm, n // tn, nk),
        in_specs=in_specs,
        out_specs=out_specs,
        out_shape=out_shape,
        scratch_shapes=[pltpu.VMEM((tm, tn), F32)] if nk > 1 else [],
        compiler_params=_cparams(("parallel", "parallel", "arbitrary")),
        name=name,
    )(*a_ins, *rowk_ins, w, *e_ins, *rown_ins)
    return outs


def _ln_epilogue(alpha):
    def epilogue(acc, xres, g, b):
        y = alpha * xres + acc
        mu = jnp.mean(y, axis=-1, keepdims=True)
        yc = y - mu
        var = jnp.mean(yc * yc, axis=-1, keepdims=True)
        return (yc * lax.rsqrt(var + LN_EPS) * g + b,)
    return epilogue


def _halo_specs(tt, tc, t_total):
    r = tt // HALO
    last = t_total // HALO - 1
    main = pl.BlockSpec((1, tt, tc), lambda b, i, c: (b, i, c))
    prev = pl.BlockSpec((1, HALO, tc), lambda b, i, c: (b, jnp.maximum(i * r - 1, 0), c))
    nxt = pl.BlockSpec((1, HALO, tc), lambda b, i, c: (b, jnp.minimum((i + 1) * r, last), c))
    return main, prev, nxt


def _fill_halo(scr, x_ref, xp_ref, xn_ref, tt, nt):
    i = pl.program_id(1)
    scr[0:HALO, :] = jnp.where(i > 0, xp_ref[0], 0.0)
    scr[HALO:HALO + tt, :] = x_ref[0]
    scr[HALO + tt:2 * HALO + tt, :] = jnp.where(i < nt - 1, xn_ref[0], 0.0)


def _gdn_conv_kernel(x_ref, xp_ref, xn_ref, w_ref, o_ref, scr, *, tt, tc, nt, nq_tiles, nqk_tiles, q_scale):
    _fill_halo(scr, x_ref, xp_ref, xn_ref, tt, nt)
    pad = (GDN_CONV - 1) // 2
    acc = None
    for j in range(GDN_CONV):
        term = w_ref[j:j + 1, :] * scr[pl.ds(HALO - pad + j, tt), :]
        acc = term if acc is None else acc + term
    y = acc * jax.nn.sigmoid(acc)
    c = pl.program_id(2)
    is_qk = c < nqk_tiles
    scale = jnp.where(c < nq_tiles, q_scale, 1.0).astype(F32)
    for hh in range(tc // LANES):
        blk = y[:, hh * LANES:(hh + 1) * LANES]
        ss = jnp.sum(blk * blk, axis=-1, keepdims=True)
        nrm = blk * lax.rsqrt(ss + 1e-6) * scale
        o_ref[0, :, hh * LANES:(hh + 1) * LANES] = jnp.where(is_qk, nrm, blk)


def _gdn_conv(qkv, conv_w, kd, vd, dk, tt=256, tc=512):
    b, t, _ = qkv.shape
    ch = 2 * kd + vd
    tt = min(tt, t)
    assert t % tt == 0 and ch % tc == 0 and kd % tc == 0 and dk == LANES
    main, prev, nxt = _halo_specs(tt, tc, t)
    kern = functools.partial(_gdn_conv_kernel, tt=tt, tc=tc, nt=t // tt, nq_tiles=kd // tc,
                             nqk_tiles=2 * kd // tc, q_scale=dk ** -0.5)
    return pl.pallas_call(
        kern,
        grid=(b, t // tt, ch // tc),
        in_specs=[main, prev, nxt, pl.BlockSpec((GDN_CONV, tc), lambda b_, i, c: (0, c))],
        out_specs=pl.BlockSpec((1, tt, tc), lambda b_, i, c: (b_, i, c)),
        out_shape=jax.ShapeDtypeStruct((b, t, ch), F32),
        scratch_shapes=[pltpu.VMEM((tt + 2 * HALO, tc), F32)],
        compiler_params=_cparams(("parallel", "parallel", "parallel")),
        name="gdn_conv",
    )(qkv, qkv, qkv, conv_w)


def _chunk_tri_masks(rows, chunk):
    ri = lax.broadcasted_iota(jnp.int32, (rows, rows), 0)
    ci = lax.broadcasted_iota(jnp.int32, (rows, rows), 1)
    same = (ri // chunk) == (ci // chunk)
    return same & (ci <= ri), same & (ci >= ri)


def _softplus(x):
    return jnp.maximum(x, 0.0) + jnp.log1p(jnp.exp(-jnp.abs(x)))


def _gdn_gates_kernel(g_ref, alog_ref, dtb_ref, o_ref, ot_ref, *, hv):
    x = g_ref[0]
    g = -jnp.exp(alog_ref[...]) * _softplus(x + dtb_ref[...])
    lower, upper = _chunk_tri_masks(TIME_TILE, CHUNK)
    gc_f = jnp.dot(lower.astype(F32), g, precision=HIGHEST, preferred_element_type=F32)
    gc_b = jnp.dot(upper.astype(F32), g, precision=HIGHEST, preferred_element_type=F32)
    lane = lax.broadcasted_iota(jnp.int32, x.shape, 1)
    out = jnp.where(lane < hv, gc_f, jnp.where(lane < 2 * hv, gc_b, jax.nn.sigmoid(x)))
    o_ref[0] = out
    ot_ref[0] = out.T


def _gdn_gates(gates, a_log, dt_bias, hv):
    b, t, ch = gates.shape
    assert ch == LANES == 4 * hv and t % TIME_TILE == 0
    pad = jnp.zeros((1, 2 * hv), F32)
    alog = jnp.concatenate([a_log.reshape(1, 2 * hv).astype(F32), pad], axis=1)
    dtb = jnp.concatenate([dt_bias.reshape(1, 2 * hv).astype(F32), pad], axis=1)
    row = pl.BlockSpec((1, LANES), lambda b_, i: (0, 0))
    return pl.pallas_call(
        functools.partial(_gdn_gates_kernel, hv=hv),
        grid=(b, t // TIME_TILE),
        in_specs=[pl.BlockSpec((1, TIME_TILE, LANES), lambda b_, i: (b_, i, 0)), row, row],
        out_specs=[pl.BlockSpec((1, TIME_TILE, LANES), lambda b_, i: (b_, i, 0)),
                   pl.BlockSpec((1, LANES, TIME_TILE), lambda b_, i: (b_, 0, i))],
        out_shape=[jax.ShapeDtypeStruct((b, t, LANES), F32), jax.ShapeDtypeStruct((b, LANES, t), F32)],
        compiler_params=_cparams(("parallel", "parallel")),
        name="gdn_gates",
    )(gates, alog, dtb)


def _mm(a, b):
    return jnp.dot(a.astype(BF16), b.astype(BF16), preferred_element_type=F32)


def _mm_nt(a, b):
    return lax.dot_general(a.astype(BF16), b.astype(BF16), (((1,), (1,)), ((), ())), preferred_element_type=F32)


def _mm_tn(a, b):
    return lax.dot_general(a.astype(BF16), b.astype(BF16), (((0,), (0,)), ((), ())), preferred_element_type=F32)


def _mm_hi(a, b):
    return jnp.dot(a, b, precision=HIGHEST, preferred_element_type=F32)


def _tri_inverse(m, size):
    ri = lax.broadcasted_iota(jnp.int32, (size, size), 0)
    ci = lax.broadcasted_iota(jnp.int32, (size, size), 1)
    eye = (ri == ci).astype(F32)
    base_bits = int(math.log2(INV_BASE))
    x = jnp.where((ri >> base_bits) == (ci >> base_bits), -m, 0.0)
    p = eye + x
    xp = x
    for _ in range(base_bits - 1):
        xp = _mm_hi(xp, xp)
        p = p + _mm_hi(p, xp)
    bits = base_bits
    while (1 << bits) < size:
        pair = ((ri >> (bits + 1)) == (ci >> (bits + 1))) & ((ri >> bits) != (ci >> bits))
        p = p - _mm_hi(p, _mm_hi(jnp.where(pair, m, 0.0), p))
        bits += 1
    return p


def _pick_lane(tile, lane_idx):
    lane = lax.broadcasted_iota(jnp.int32, tile.shape, 1)
    return jnp.sum(jnp.where(lane == lane_idx, tile, 0.0), axis=1, keepdims=True)


def _gdn_chunk_kernel(qf_ref, kf_ref, vf_ref, gf_ref, gtf_ref, qb_ref, kb_ref, vb_ref, gb_ref, gtb_ref,
                      of_ref, ob_ref, sf_ref, sb_ref, *, rep, hv):
    n = pl.program_id(2)
    hp = pl.program_id(1)

    @pl.when(n == 0)
    def _():
        sf_ref[...] = jnp.zeros_like(sf_ref)
        sb_ref[...] = jnp.zeros_like(sb_ref)

    c = CHUNK
    ri = lax.broadcasted_iota(jnp.int32, (c, c), 0)
    ci = lax.broadcasted_iota(jnp.int32, (c, c), 1)
    dirs = ((False, qf_ref, kf_ref, vf_ref, gf_ref, gtf_ref, of_ref, sf_ref),
            (True, qb_ref, kb_ref, vb_ref, gb_ref, gtb_ref, ob_ref, sb_ref))
    for rev, q_ref, k_ref, v_ref, g_ref, gt_ref, o_ref, s_ref in dirs:
        incl = (ci >= ri) if rev else (ci <= ri)
        strict = (ci > ri) if rev else (ci < ri)
        d = 1 if rev else 0
        q_tile = q_ref[0]
        k_tile = k_ref[0]
        g_tile = g_ref[0]
        order = range(TIME_TILE // c - 1, -1, -1) if rev else range(TIME_TILE // c)
        for r in range(rep):
            h = hp * rep + r
            gcol_t = _pick_lane(g_tile, d * hv + h)
            bcol_t = _pick_lane(g_tile, 2 * hv + d * hv + h)
            grow_t = gt_ref[0, pl.ds(d * hv + h, 1), :]
            s = s_ref[r]
            for cc in order:
                sl = slice(cc * c, (cc + 1) * c)
                qc, kc = q_tile[sl], k_tile[sl]
                vc = v_ref[0, sl, r * LANES:(r + 1) * LANES]
                gcol, bcol, grow = gcol_t[sl], bcol_t[sl], grow_t[:, sl]
                end = cc * c if rev else (cc + 1) * c - 1
                g_last = gcol_t[end:end + 1]
                diff = gcol - grow
                decay = jnp.where(incl, jnp.exp(jnp.where(incl, diff, 0.0)), 0.0)
                kb = kc * bcol
                a_kk = jnp.where(strict, _mm_nt(kb, kc) * decay, 0.0)
                t_inv = _tri_inverse(a_kk, c)
                egc = jnp.exp(gcol)
                u = _mm(t_inv, vc * bcol)
                w = _mm(t_inv, kb * egc)
                a_qk = _mm_nt(qc, kc) * decay
                v_new = u - _mm(w, s)
                o = _mm(qc * egc, s) + _mm(a_qk, v_new)
                s = s * jnp.exp(g_last) + _mm_tn(kc * jnp.exp(g_last - gcol), v_new)
                o_ref[0, sl, r * LANES:(r + 1) * LANES] = o
            s_ref[r] = s


def _gdn_chunk(qkv, gcb, gcbt, hk, hv, dk, dv):
    b, t, _ = qkv.shape
    rep = hv // hk
    assert dk == LANES and dv == LANES and t % TIME_TILE == 0 and (2 * hk) % rep == 0
    nt = t // TIME_TILE
    vblk0 = 2 * hk // rep

    def specs(tile):
        return [pl.BlockSpec((1, TIME_TILE, LANES), lambda b_, h, n: (b_, tile(n), h)),
                pl.BlockSpec((1, TIME_TILE, LANES), lambda b_, h, n: (b_, tile(n), hk + h)),
                pl.BlockSpec((1, TIME_TILE, rep * LANES), lambda b_, h, n: (b_, tile(n), vblk0 + h)),
                pl.BlockSpec((1, TIME_TILE, LANES), lambda b_, h, n: (b_, tile(n), 0)),
                pl.BlockSpec((1, LANES, TIME_TILE), lambda b_, h, n: (b_, 0, tile(n)))]

    fwd = lambda n: n
    bwd = lambda n: nt - 1 - n
    out_f = pl.BlockSpec((1, TIME_TILE, rep * LANES), lambda b_, h, n: (b_, n, h))
    out_b = pl.BlockSpec((1, TIME_TILE, rep * LANES), lambda b_, h, n: (b_, nt - 1 - n, h))
    shp = jax.ShapeDtypeStruct((b, t, hv * dv), F32)
    return pl.pallas_call(
        functools.partial(_gdn_chunk_kernel, rep=rep, hv=hv),
        grid=(b, hk, nt),
        in_specs=specs(fwd) + specs(bwd),
        out_specs=[out_f, out_b],
        out_shape=[shp, shp],
        scratch_shapes=[pltpu.VMEM((rep, dk, dv), F32), pltpu.VMEM((rep, dk, dv), F32)],
        compiler_params=_cparams(("parallel", "parallel", "arbitrary")),
        name="gdn_chunk",
    )(qkv, qkv, qkv, gcb, gcbt, qkv, qkv, qkv, gcb, gcbt)


def _gdn_out_prologue(o_f, o_b, z, nw):
    parts = []
    for hh in range(o_f.shape[1] // LANES):
        sl = slice(hh * LANES, (hh + 1) * LANES)
        o = o_f[:, sl] + o_b[:, sl]
        o = o * lax.rsqrt(jnp.mean(o * o, axis=-1, keepdims=True) + 1e-6) * nw[:, sl]
        zz = z[:, sl]
        parts.append((o * (zz * jax.nn.sigmoid(zz))).astype(BF16))
    return jnp.concatenate(parts, axis=1) if len(parts) > 1 else parts[0]


def _gdn_mixer_ln(x, w, alpha, ln_g, ln_b):
    b, t, dm = x.shape
    m = b * t
    x2 = x.reshape(m, dm)
    kd, vd, hk, hv, dk, dv = w["kd"], w["vd"], w["hk"], w["hv"], w["dk"], w["dv"]
    qkvz, = _fused_matmul([x2], w["w_qkvz"], tm=1024, tn=1024, tk=dm, name="gdn_in")
    gates, = _fused_matmul([x2], w["w_gates"], tm=1024, tn=LANES, tk=dm, name="gdn_in_gates")
    qkv = _gdn_conv(qkvz.reshape(b, t, -1), w["conv"], kd, vd, dk)
    gcb, gcbt = _gdn_gates(gates.reshape(b, t, -1), w["a_log"], w["dt_bias"], hv)
    o_f, o_b = _gdn_chunk(qkv, gcb, gcbt, hk, hv, dk, dv)
    out, = _fused_matmul(
        [o_f.reshape(m, vd), o_b.reshape(m, vd), qkvz], w["w_out"], a_col0=[0, 0, 2 * kd + vd],
        rowk_ins=[w["norm_row"]], e_ins=[x2], rown_ins=[ln_g, ln_b],
        prologue=_gdn_out_prologue, epilogue=_ln_epilogue(alpha), tm=512, tn=dm, tk=512, name="gdn_out")
    return out


def _shift_kernel(x_ref, xp_ref, xn_ref, o_ref, scr, *, tt, nt):
    _fill_halo(scr, x_ref, xp_ref, xn_ref, tt, nt)
    o_ref[0] = 0.5 * (scr[pl.ds(HALO - 1, tt), :] + scr[pl.ds(HALO + 1, tt), :]) - x_ref[0]


def _token_shift(x, tt=256, tc=512):
    b, t, ch = x.shape
    tt, tc = min(tt, t), min(tc, ch)
    assert t % tt == 0 and ch % tc == 0
    main, prev, nxt = _halo_specs(tt, tc, t)
    return pl.pallas_call(
        functools.partial(_shift_kernel, tt=tt, nt=t // tt),
        grid=(b, t // tt, ch // tc),
        in_specs=[main, prev, nxt],
        out_specs=pl.BlockSpec((1, tt, tc), lambda b_, i, c: (b_, i, c)),
        out_shape=jax.ShapeDtypeStruct((b, t, ch), F32),
        scratch_shapes=[pltpu.VMEM((tt + 2 * HALO, tc), F32)],
        compiler_params=_cparams(("parallel", "parallel", "parallel")),
        name="rwkv_shift",
    )(x, x, x)


def _mix_prologue(x, xx, mix):
    return (x + xx * mix).astype(BF16)


def _half_sum(x, lo):
    s0 = jnp.sum(jnp.where(lo, x, 0.0), axis=-1, keepdims=True)
    s1 = jnp.sum(jnp.where(lo, 0.0, x), axis=-1, keepdims=True)
    return jnp.where(lo, s0, s1)


def _rwkv_chunk_kernel(rf_ref, kf_ref, vf_ref, wf_ref, af_ref, rb_ref, kb_ref, vb_ref, wb_ref, ab_ref,
                       kk_ref, ka_ref, rk_ref, yf_ref, bf_ref, yb_ref, bb_ref, gf_ref, gb_ref):
    n = pl.program_id(2)

    @pl.when(n == 0)
    def _():
        gf_ref[...] = jnp.zeros_like(gf_ref)
        gb_ref[...] = jnp.zeros_like(gb_ref)

    c, nn = CHUNK, RWKV_N
    lane = lax.broadcasted_iota(jnp.int32, (TIME_TILE, LANES), 1)
    lo = lane < nn
    lower, upper = _chunk_tri_masks(TIME_TILE, c)
    ri = lax.broadcasted_iota(jnp.int32, (c, c), 0)
    ci = lax.broadcasted_iota(jnp.int32, (c, c), 1)
    dirs = ((False, rf_ref, kf_ref, vf_ref, wf_ref, af_ref, yf_ref, bf_ref, gf_ref),
            (True, rb_ref, kb_ref, vb_ref, wb_ref, ab_ref, yb_ref, bb_ref, gb_ref))
    for rev, r_ref, k_ref, v_ref, w_ref, a_ref, y_ref, bonus_ref, g_ref in dirs:
        incl = (ci >= ri) if rev else (ci <= ri)
        strict = (ci > ri) if rev else (ci < ri)
        r, k, v, lw, a = r_ref[0], k_ref[0], v_ref[0], w_ref[0], a_ref[0]
        kkk = k * kk_ref[...]
        kk = kkk * lax.rsqrt(_half_sum(kkk * kkk, lo) + 1e-6)
        kd = k * (1.0 + (a - 1.0) * ka_ref[...])
        av = -kk
        bv = kk * a
        bonus_ref[0] = _half_sum(r * kd * rk_ref[...], lo) * v
        gc = _mm_hi((upper if rev else lower).astype(F32), lw)
        ei = jnp.exp(-gc)
        at = av * jnp.exp(gc - lw)
        kt = kd * ei
        bt = bv * ei
        rt = r * jnp.exp(gc)
        order = range(TIME_TILE // c - 1, -1, -1) if rev else range(TIME_TILE // c)
        for cc in order:
            sl = slice(cc * c, (cc + 1) * c)
            end = cc * c if rev else (cc + 1) * c - 1
            gend = gc[end:end + 1]
            ee = jnp.exp(gend - gc[sl])
            kh = kd[sl] * ee
            bh = bv[sl] * ee
            egend = jnp.exp(gend)
            ys = []
            for j in range(LANES // nn):
                hs = slice(j * nn, (j + 1) * nn)
                g = g_ref[j]
                atc, ktc, btc, rtc, vc = at[sl, hs], kt[sl, hs], bt[sl, hs], rt[sl, hs], v[sl, hs]
                a_ab = jnp.where(strict, _mm_nt(atc, btc), 0.0)
                a_ak = jnp.where(strict, _mm_nt(atc, ktc), 0.0)
                t_inv = _tri_inverse(-a_ab, c)
                u = _mm(t_inv, _mm_nt(atc, g) + _mm(a_ak, vc))
                y = (_mm_nt(rtc, g) + _mm(jnp.where(incl, _mm_nt(rtc, ktc), 0.0), vc)
                     + _mm(jnp.where(incl, _mm_nt(rtc, btc), 0.0), u))
                g_ref[j] = g * egend[:, hs] + _mm_tn(vc, kh[:, hs]) + _mm_tn(u, bh[:, hs])
                ys.append(y)
            y_ref[0, sl, :] = jnp.concatenate(ys, axis=1)


def _rwkv_chunk(r, k, v, logw, a, k_k, k_a, r_k):
    b, t, dm = r.shape
    assert t % TIME_TILE == 0 and dm % LANES == 0 and LANES % RWKV_N == 0
    nt = t // TIME_TILE
    fwd = lambda n: n
    bwd = lambda n: nt - 1 - n

    def tile(order):
        return pl.BlockSpec((1, TIME_TILE, LANES), lambda b_, h, n: (b_, order(n), h))

    row = pl.BlockSpec((1, LANES), lambda b_, h, n: (0, h))
    shp = jax.ShapeDtypeStruct((b, t, dm), F32)
    heads = LANES // RWKV_N
    return pl.pallas_call(
        _rwkv_chunk_kernel,
        grid=(b, dm // LANES, nt),
        in_specs=[tile(fwd)] * 5 + [tile(bwd)] * 5 + [row] * 3,
        out_specs=[tile(fwd), tile(fwd), tile(bwd), tile(bwd)],
        out_shape=[shp] * 4,
        scratch_shapes=[pltpu.VMEM((heads, RWKV_N, RWKV_N), F32), pltpu.VMEM((heads, RWKV_N, RWKV_N), F32)],
        compiler_params=_cparams(("parallel", "parallel", "arbitrary")),
        name="rwkv_chunk",
    )(r, k, v, logw[0], a[0], r, k, v, logw[1], a[1], k_k, k_a, r_k)


def _rwkv_out_prologue(y_f, y_b, bn_f, bn_b, g, ln_w, ln_b):
    parts = []
    for hh in range(y_f.shape[1] // LANES):
        sl = slice(hh * LANES, (hh + 1) * LANES)
        wkv = y_f[:, sl] + y_b[:, sl]
        lo = lax.broadcasted_iota(jnp.int32, wkv.shape, 1) < RWKV_N
        mu = _half_sum(wkv, lo) * (1.0 / RWKV_N)
        wc = wkv - mu
        var = _half_sum(wc * wc, lo) * (1.0 / RWKV_N)
        o = wc * lax.rsqrt(var + RWKV_GN_EPS) * ln_w[:, sl] + ln_b[:, sl] + (bn_f[:, sl] + bn_b[:, sl])
        parts.append((o * g[:, sl]).astype(BF16))
    return jnp.concatenate(parts, axis=1) if len(parts) > 1 else parts[0]


def _rwkv_mixer_ln(x, w, alpha, ln_g, ln_b):
    b, t, dm = x.shape
    m = b * t
    x2 = x.reshape(m, dm)
    xx = _token_shift(x).reshape(m, dm)
    mix = w["mix"]

    def proj(mi, wmat, epilogue=_identity_epilogue, dtype=F32, name="rwkv_proj"):
        out, = _fused_matmul([x2, xx], wmat, rowk_ins=[mix[mi:mi + 1]], prologue=_mix_prologue,
                             epilogue=epilogue, out_dtypes=(dtype,), tm=1024, tn=1024, tk=dm, name=name)
        return out

    r = proj(0, w["w_r"], name="rwkv_r")
    k = proj(2, w["w_k"], name="rwkv_k")
    v = proj(3, w["w_v"], name="rwkv_v")
    lw = proj(1, w["w1"], lambda acc: (jnp.tanh(acc),), BF16, "rwkv_w1")
    la = proj(4, w["a1"], _identity_epilogue, BF16, "rwkv_a1")
    lg = proj(5, w["g1"], lambda acc: (jax.nn.sigmoid(acc),), BF16, "rwkv_g1")

    def decay_epilogue(acc, w0):
        return (-jnp.exp(-_softplus(-(w0 + acc)) - 0.5),)

    def rate_epilogue(acc, a0):
        return (jax.nn.sigmoid(a0 + acc),)

    logw, a = [], []
    for d in range(2):
        out, = _fused_matmul([lw], w["w2"][d], a_col0=[d * LANES], rown_ins=[w["w0"][d:d + 1]],
                             epilogue=decay_epilogue, tm=1024, tn=1024, tk=LANES, name="rwkv_w2")
        logw.append(out.reshape(b, t, dm))
        out, = _fused_matmul([la], w["a2"][d], a_col0=[d * LANES], rown_ins=[w["a0"][d:d + 1]],
                             epilogue=rate_epilogue, tm=1024, tn=1024, tk=LANES, name="rwkv_a2")
        a.append(out.reshape(b, t, dm))
    g, = _fused_matmul([lg], w["g2"], tm=1024, tn=1024, tk=lg.shape[1], name="rwkv_g2")

    r3, k3, v3 = (z.reshape(b, t, dm) for z in (r, k, v))
    y_f, bn_f, y_b, bn_b = _rwkv_chunk(r3, k3, v3, logw, a, w["k_k"], w["k_a"], w["r_k"])
    out, = _fused_matmul(
        [z.reshape(m, dm) for z in (y_f, y_b, bn_f, bn_b)] + [g], w["w_o"],
        rowk_ins=[w["ln_w"], w["ln_b"]], e_ins=[x2], rown_ins=[ln_g, ln_b],
        prologue=_rwkv_out_prologue, epilogue=_ln_epilogue(alpha), tm=512, tn=dm, tk=512, name="rwkv_out")
    return out


def _relu2_epilogue(acc):
    return (jnp.square(jnp.maximum(acc, 0.0)),)


def _mlp_ln(x2, w_up, w_down, alpha, ln_g, ln_b):
    dm = x2.shape[1]
    h, = _fused_matmul([x2], w_up, epilogue=_relu2_epilogue, out_dtypes=(BF16,), tm=1024, tn=1024, tk=dm,
                       name="mlp_up")
    out, = _fused_matmul([h], w_down, e_ins=[x2], rown_ins=[ln_g, ln_b], epilogue=_ln_epilogue(alpha),
                         tm=512, tn=dm, tk=512, name="mlp_down")
    return out


def _ple_epilogue(acc, x, pp):
    return (x + jax.nn.sigmoid(acc) * pp,)


def _ple(x2, p2, w_proj, w_gate):
    dm = x2.shape[1]
    pp, = _fused_matmul([p2], w_proj, tm=1024, tn=1024, tk=p2.shape[1], name="ple_proj")
    out, = _fused_matmul([x2], w_gate, e_ins=[x2, pp], epilogue=_ple_epilogue, tm=512, tn=1024, tk=dm,
                         name="ple_gate")
    return out


def _pad_rows(w, rows):
    return jnp.pad(w, ((0, rows - w.shape[0]), (0, 0)))


def _pad_cols(w, cols):
    return jnp.pad(w, ((0, 0), (0, cols - w.shape[1])))


def _gdn_weights(w_in, conv, a_log, dt_bias, norm, w_out):
    hv, dv = a_log.shape[-1], norm.shape[-1]
    vd = hv * dv
    kd = (conv.shape[-1] - vd) // 2
    dk = dv
    assert w_in.shape[1] == 2 * kd + 2 * vd + 4 * hv and w_out.shape[0] == vd
    nz = 2 * kd + 2 * vd
    return dict(kd=kd, vd=vd, hk=kd // dk, hv=hv, dk=dk, dv=dv,
                w_qkvz=w_in[:, :nz].astype(BF16), w_gates=w_in[:, nz:].astype(BF16), conv=conv.astype(F32),
                a_log=a_log, dt_bias=dt_bias, norm_row=jnp.tile(norm.astype(F32), hv).reshape(1, vd),
                w_out=w_out.astype(BF16))


def _rwkv_weights(mix, w_rkv, w0, w1, w2, a0, a1, a2, g1, g2, k_k, k_a, r_k, ln_w, ln_b, w_o):
    dm = w_o.shape[0]
    assert w1.shape[-1] <= LANES and a1.shape[-1] <= LANES and r_k.size == dm and r_k.shape[-1] == RWKV_N
    cat = lambda ws: jnp.concatenate([_pad_cols(ws[d], LANES) for d in range(2)], axis=1).astype(BF16)
    row = lambda z: z.reshape(1, dm).astype(F32)
    return dict(mix=mix.astype(F32), w_r=w_rkv[0].astype(BF16), w_k=w_rkv[1].astype(BF16),
                w_v=w_rkv[2].astype(BF16), w1=cat(w1), a1=cat(a1), g1=g1.astype(BF16),
                w2=[_pad_rows(w2[d], LANES).astype(BF16) for d in range(2)],
                a2=[_pad_rows(a2[d], LANES).astype(BF16) for d in range(2)],
                g2=g2.astype(BF16), w0=w0.astype(F32), a0=a0.astype(F32),
                k_k=row(k_k), k_a=row(k_a), r_k=row(r_k), ln_w=row(ln_w), ln_b=row(ln_b), w_o=w_o.astype(BF16))


def _trunk(x, p, layers, alpha):
    b, t, dm = x.shape
    m = b * t
    for i, lyr in enumerate(layers):
        mixer = _gdn_mixer_ln if lyr["kind"] == "gdn" else _rwkv_mixer_ln
        x2 = mixer(x, lyr["mixer"], alpha, lyr["ln_g"][0:1], lyr["ln_b"][0:1])
        x2 = _mlp_ln(x2, lyr["w_up"], lyr["w_down"], alpha, lyr["ln_g"][1:2], lyr["ln_b"][1:2])
        x2 = _ple(x2, p[i].reshape(m, -1), lyr["w_proj"], lyr["w_gate"])
        x = x2.reshape(b, t, dm)
    return x


def kernel(x_prompt, x_sample, p_prompt, p_sample, gdn_w_in, gdn_conv, gdn_a_log, gdn_dt_bias, gdn_norm, gdn_w_out, rwkv_mix, rwkv_w_rkv, rwkv_w0, rwkv_w1, rwkv_w2, rwkv_a0, rwkv_a1, rwkv_a2, rwkv_g1, rwkv_g2, rwkv_k_k, rwkv_k_a, rwkv_r_k, rwkv_ln_w, rwkv_ln_b, rwkv_w_o, ln_g, ln_b, mlp_w_up, mlp_w_down, ple_w_proj, ple_w_gate):
    depth = ln_g.shape[0]
    alpha = (2 * depth) ** 0.25
    layers = []
    for i in range(depth):
        j = i // 2
        if i % 2 == 0:
            kind = "gdn"
            mixer = _gdn_weights(gdn_w_in[j], gdn_conv[j], gdn_a_log[j], gdn_dt_bias[j], gdn_norm[j], gdn_w_out[j])
        else:
            kind = "rwkv"
            mixer = _rwkv_weights(rwkv_mix[j], rwkv_w_rkv[j], rwkv_w0[j], rwkv_w1[j], rwkv_w2[j], rwkv_a0[j],
                                  rwkv_a1[j], rwkv_a2[j], rwkv_g1[j], rwkv_g2[j], rwkv_k_k[j], rwkv_k_a[j],
                                  rwkv_r_k[j], rwkv_ln_w[j], rwkv_ln_b[j], rwkv_w_o[j])
        layers.append(dict(kind=kind, mixer=mixer, ln_g=ln_g[i].astype(F32), ln_b=ln_b[i].astype(F32),
                           w_up=mlp_w_up[i].astype(BF16), w_down=mlp_w_down[i].astype(BF16),
                           w_proj=ple_w_proj[i].astype(BF16), w_gate=ple_w_gate[i].astype(BF16)))
    y_prompt = _trunk(x_prompt, p_prompt, layers, alpha)
    y_sample = _trunk(x_sample, p_sample, layers, alpha)
    return (y_prompt, y_sample)
```

```python
import functools
import math

import jax
import jax.numpy as jnp
from jax import lax
from jax.experimental import pallas as pl
from jax.experimental.pallas import tpu as pltpu

F32 = jnp.float32
BF16 = jnp.bfloat16
HIGHEST = lax.Precision.HIGHEST

LANES = 128
HALO = 8
TIME_TILE = 128
GDN_CHUNK = TIME_TILE
RWKV_CHUNK = 64
GDN_PAIRS = 2
RWKV_BLOCKS = 2
INV_BASE = 8
VMEM_LIMIT = 56 * 1024 * 1024

LN_EPS = 1e-5
RWKV_GN_EPS = 64e-5
RWKV_N = 64
GDN_CONV = 5


def _cparams(sem):
    return pltpu.CompilerParams(dimension_semantics=sem, vmem_limit_bytes=VMEM_LIMIT)


def _mm_kernel(*refs, na, nrk, ne, nrn, nout, nk, prologue, epilogue):
    a_refs = refs[:na]
    rk_refs = refs[na:na + nrk]
    w_ref = refs[na + nrk]
    e_refs = refs[na + nrk + 1:na + nrk + 1 + ne]
    rn_refs = refs[na + nrk + 1 + ne:na + nrk + 1 + ne + nrn]
    out_refs = refs[na + nrk + 1 + ne + nrn:na + nrk + 1 + ne + nrn + nout]

    a = prologue(*[r[...] for r in a_refs], *[r[...] for r in rk_refs])
    part = jnp.dot(a, w_ref[...], preferred_element_type=F32)

    def finish(acc):
        outs = epilogue(acc, *[r[...] for r in e_refs], *[r[...] for r in rn_refs])
        for o_ref, o in zip(out_refs, outs):
            o_ref[...] = o.astype(o_ref.dtype)

    if nk == 1:
        finish(part)
        return

    acc_ref = refs[-1]
    k = pl.program_id(2)

    @pl.when(k == 0)
    def _():
        acc_ref[...] = part

    @pl.when(k > 0)
    def _():
        acc_ref[...] += part

    @pl.when(k == nk - 1)
    def _():
        finish(acc_ref[...])


def _cast_bf16(a):
    return a.astype(BF16)


def _identity_epilogue(acc):
    return (acc,)


def _fused_matmul(a_ins, w, *, a_col0=None, rowk_ins=(), e_ins=(), rown_ins=(), prologue=_cast_bf16,
                  epilogue=_identity_epilogue, out_dtypes=(F32,), tm=512, tn=512, tk=512, name="mm"):
    m = a_ins[0].shape[0]
    kdim, n = w.shape
    tm, tn, tk = min(tm, m), min(tn, n), min(tk, kdim)
    assert m % tm == 0 and n % tn == 0 and kdim % tk == 0, (m, n, kdim, tm, tn, tk)
    nk = kdim // tk
    a_col0 = [0] * len(a_ins) if a_col0 is None else a_col0
    assert all(c0 % tk == 0 for c0 in a_col0)

    def a_spec(c0):
        return pl.BlockSpec((tm, tk), lambda i, j, k: (i, c0 // tk + k))

    in_specs = ([a_spec(c0) for c0 in a_col0]
                + [pl.BlockSpec((1, tk), lambda i, j, k: (0, k)) for _ in rowk_ins]
                + [pl.BlockSpec((tk, tn), lambda i, j, k: (k, j))]
                + [pl.BlockSpec((tm, tn), lambda i, j, k: (i, j)) for _ in e_ins]
                + [pl.BlockSpec((1, tn), lambda i, j, k: (0, j)) for _ in rown_ins])
    out_specs = [pl.BlockSpec((tm, tn), lambda i, j, k: (i, j)) for _ in out_dtypes]
    out_shape = [jax.ShapeDtypeStruct((m, n), dt) for dt in out_dtypes]
    kern = functools.partial(_mm_kernel, na=len(a_ins), nrk=len(rowk_ins), ne=len(e_ins), nrn=len(rown_ins),
                             nout=len(out_dtypes), nk=nk, prologue=prologue, epilogue=epilogue)
    outs = pl.pallas_call(
        kern,
        grid=(m // tm, n // tn, nk),
        in_specs=in_specs,
        out_specs=out_specs,
        out_shape=out_shape,
        scratch_shapes=[pltpu.VMEM((tm, tn), F32)] if nk > 1 else [],
        compiler_params=_cparams(("parallel", "parallel", "arbitrary")),
        name=name,
    )(*a_ins, *rowk_ins, w, *e_ins, *rown_ins)
    return outs


def _ln_epilogue(alpha):
    def epilogue(acc, xres, g, b):
        y = alpha * xres + acc
        mu = jnp.mean(y, axis=-1, keepdims=True)
        yc = y - mu
        var = jnp.mean(yc * yc, axis=-1, keepdims=True)
        return (yc * lax.rsqrt(var + LN_EPS) * g + b,)
    return epilogue


def _halo_specs(tt, tc, t_total):
    r = tt // HALO
    last = t_total // HALO - 1
    main = pl.BlockSpec((1, tt, tc), lambda b, i, c: (b, i, c))
    prev = pl.BlockSpec((1, HALO, tc), lambda b, i, c: (b, jnp.maximum(i * r - 1, 0), c))
    nxt = pl.BlockSpec((1, HALO, tc), lambda b, i, c: (b, jnp.minimum((i + 1) * r, last), c))
    return main, prev, nxt


def _fill_halo(scr, x_ref, xp_ref, xn_ref, tt, nt):
    i = pl.program_id(1)
    scr[0:HALO, :] = jnp.where(i > 0, xp_ref[0], 0.0)
    scr[HALO:HALO + tt, :] = x_ref[0]
    scr[HALO + tt:2 * HALO + tt, :] = jnp.where(i < nt - 1, xn_ref[0], 0.0)


def _gdn_conv_kernel(x_ref, xp_ref, xn_ref, w_ref, o_ref, scr, *, tt, tc, nt, nq_tiles, nqk_tiles, q_scale):
    _fill_halo(scr, x_ref, xp_ref, xn_ref, tt, nt)
    pad = (GDN_CONV - 1) // 2
    acc = None
    for j in range(GDN_CONV):
        term = w_ref[j:j + 1, :] * scr[pl.ds(HALO - pad + j, tt), :]
        acc = term if acc is None else acc + term
    y = acc * jax.nn.sigmoid(acc)
    c = pl.program_id(2)
    is_qk = c < nqk_tiles
    scale = jnp.where(c < nq_tiles, q_scale, 1.0).astype(F32)
    for hh in range(tc // LANES):
        blk = y[:, hh * LANES:(hh + 1) * LANES]
        ss = jnp.sum(blk * blk, axis=-1, keepdims=True)
        nrm = blk * lax.rsqrt(ss + 1e-6) * scale
        o_ref[0, :, hh * LANES:(hh + 1) * LANES] = jnp.where(is_qk, nrm, blk)


def _gdn_conv(qkv, conv_w, kd, vd, dk, tt=256, tc=512):
    b, t, _ = qkv.shape
    ch = 2 * kd + vd
    tt = min(tt, t)
    assert t % tt == 0 and ch % tc == 0 and kd % tc == 0 and dk == LANES
    main, prev, nxt = _halo_specs(tt, tc, t)
    kern = functools.partial(_gdn_conv_kernel, tt=tt, tc=tc, nt=t // tt, nq_tiles=kd // tc,
                             nqk_tiles=2 * kd // tc, q_scale=dk ** -0.5)
    return pl.pallas_call(
        kern,
        grid=(b, t // tt, ch // tc),
        in_specs=[main, prev, nxt, pl.BlockSpec((GDN_CONV, tc), lambda b_, i, c: (0, c))],
        out_specs=pl.BlockSpec((1, tt, tc), lambda b_, i, c: (b_, i, c)),
        out_shape=jax.ShapeDtypeStruct((b, t, ch), F32),
        scratch_shapes=[pltpu.VMEM((tt + 2 * HALO, tc), F32)],
        compiler_params=_cparams(("parallel", "parallel", "parallel")),
        name="gdn_conv",
    )(qkv, qkv, qkv, conv_w)


def _chunk_tri_masks(rows, chunk):
    ri = lax.broadcasted_iota(jnp.int32, (rows, rows), 0)
    ci = lax.broadcasted_iota(jnp.int32, (rows, rows), 1)
    same = (ri // chunk) == (ci // chunk)
    return same & (ci <= ri), same & (ci >= ri)


def _softplus(x):
    return jnp.maximum(x, 0.0) + jnp.log1p(jnp.exp(-jnp.abs(x)))


def _gdn_gates_kernel(g_ref, alog_ref, dtb_ref, o_ref, ot_ref, *, hv):
    x = g_ref[0]
    g = -jnp.exp(alog_ref[...]) * _softplus(x + dtb_ref[...])
    lower, upper = _chunk_tri_masks(TIME_TILE, GDN_CHUNK)
    gc_f = jnp.dot(lower.astype(F32), g, precision=HIGHEST, preferred_element_type=F32)
    gc_b = jnp.dot(upper.astype(F32), g, precision=HIGHEST, preferred_element_type=F32)
    lane = lax.broadcasted_iota(jnp.int32, x.shape, 1)
    out = jnp.where(lane < hv, gc_f, jnp.where(lane < 2 * hv, gc_b, jax.nn.sigmoid(x)))
    o_ref[0] = out
    ot_ref[0] = out.T


def _gdn_gates(gates, a_log, dt_bias, hv):
    b, t, ch = gates.shape
    assert ch == LANES == 4 * hv and t % TIME_TILE == 0
    pad = jnp.zeros((1, 2 * hv), F32)
    alog = jnp.concatenate([a_log.reshape(1, 2 * hv).astype(F32), pad], axis=1)
    dtb = jnp.concatenate([dt_bias.reshape(1, 2 * hv).astype(F32), pad], axis=1)
    row = pl.BlockSpec((1, LANES), lambda b_, i: (0, 0))
    return pl.pallas_call(
        functools.partial(_gdn_gates_kernel, hv=hv),
        grid=(b, t // TIME_TILE),
        in_specs=[pl.BlockSpec((1, TIME_TILE, LANES), lambda b_, i: (b_, i, 0)), row, row],
        out_specs=[pl.BlockSpec((1, TIME_TILE, LANES), lambda b_, i: (b_, i, 0)),
                   pl.BlockSpec((1, LANES, TIME_TILE), lambda b_, i: (b_, 0, i))],
        out_shape=[jax.ShapeDtypeStruct((b, t, LANES), F32), jax.ShapeDtypeStruct((b, LANES, t), F32)],
        compiler_params=_cparams(("parallel", "parallel")),
        name="gdn_gates",
    )(gates, alog, dtb)


def _mm(a, b):
    return jnp.dot(a.astype(BF16), b.astype(BF16), preferred_element_type=F32)


def _mm_nt(a, b):
    return lax.dot_general(a.astype(BF16), b.astype(BF16), (((1,), (1,)), ((), ())), preferred_element_type=F32)


def _mm_tn(a, b):
    return lax.dot_general(a.astype(BF16), b.astype(BF16), (((0,), (0,)), ((), ())), preferred_element_type=F32)


def _mm_hi(a, b):
    return jnp.dot(a, b, precision=HIGHEST, preferred_element_type=F32)


def _tri_inverses(ms, size, top):
    ri = lax.broadcasted_iota(jnp.int32, (size, size), 0)
    ci = lax.broadcasted_iota(jnp.int32, (size, size), 1)
    eye = (ri == ci).astype(F32)
    base_bits = int(math.log2(INV_BASE))
    base = (ri >> base_bits) == (ci >> base_bits)
    xps = [jnp.where(base, -m, 0.0) for m in ms]
    ps = [eye + x for x in xps]
    for _ in range(base_bits - 1):
        xps = [_mm(xp, xp) for xp in xps]
        ps = [p + _mm(p, xp) for p, xp in zip(ps, xps)]
    bits = base_bits
    while (1 << bits) < top:
        pair = ((ri >> (bits + 1)) == (ci >> (bits + 1))) & ((ri >> bits) != (ci >> bits))
        qs = [_mm(jnp.where(pair, m, 0.0), p) for m, p in zip(ms, ps)]
        ps = [p - _mm(p, q) for p, q in zip(ps, qs)]
        bits += 1
    return ps


def _pick_lane(tile, lane_idx):
    lane = lax.broadcasted_iota(jnp.int32, tile.shape, 1)
    return jnp.sum(jnp.where(lane == lane_idx, tile, 0.0), axis=1, keepdims=True)


def _gdn_chunk_kernel(qf_ref, kf_ref, vf_ref, gf_ref, gtf_ref, qb_ref, kb_ref, vb_ref, gb_ref, gtb_ref,
                      of_ref, ob_ref, sf_ref, sb_ref, *, rep, hv, npairs):
    n = pl.program_id(2)
    hg = pl.program_id(1)

    @pl.when(n == 0)
    def _():
        sf_ref[...] = jnp.zeros_like(sf_ref)
        sb_ref[...] = jnp.zeros_like(sb_ref)

    c = GDN_CHUNK
    ri = lax.broadcasted_iota(jnp.int32, (c, c), 0)
    ci = lax.broadcasted_iota(jnp.int32, (c, c), 1)
    dirs = ((False, qf_ref, kf_ref, vf_ref, gf_ref, gtf_ref, of_ref, sf_ref),
            (True, qb_ref, kb_ref, vb_ref, gb_ref, gtb_ref, ob_ref, sb_ref))

    pairs = []
    for rev, q_ref, k_ref, *_ in dirs:
        for i in range(npairs):
            pairs.append((q_ref[0, :, i * LANES:(i + 1) * LANES], k_ref[0, :, i * LANES:(i + 1) * LANES]))
    qks = [_mm_nt(q, k) for q, k in pairs]
    kks = [_mm_nt(k, k) for _, k in pairs]

    probs = []
    for d, (rev, q_ref, k_ref, v_ref, g_ref, gt_ref, o_ref, s_ref) in enumerate(dirs):
        incl = (ci >= ri) if rev else (ci <= ri)
        strict = (ci > ri) if rev else (ci < ri)
        g_tile = g_ref[0]
        end = 0 if rev else c - 1
        for i in range(npairs):
            q, k = pairs[d * npairs + i]
            for r in range(rep):
                slot = i * rep + r
                h = (hg * npairs + i) * rep + r
                gcol = _pick_lane(g_tile, d * hv + h)
                bcol = _pick_lane(g_tile, 2 * hv + d * hv + h)
                grow = gt_ref[0, pl.ds(d * hv + h, 1), :]
                decay = jnp.where(incl, jnp.exp(jnp.where(incl, gcol - grow, 0.0)), 0.0)
                g_last = gcol[end:end + 1]
                egc = jnp.exp(gcol)
                v = v_ref[0, :, slot * LANES:(slot + 1) * LANES]
                probs.append(dict(
                    q=q, k=k, egc=egc, g_last=g_last, slot=slot, o_ref=o_ref, s_ref=s_ref,
                    a_kk=jnp.where(strict, kks[d * npairs + i] * bcol * decay, 0.0),
                    a_qk=qks[d * npairs + i] * decay,
                    rhs=jnp.concatenate([v * bcol, k * (bcol * egc)], axis=1),
                    k_dec=k * jnp.exp(g_last - gcol)))

    t_invs = _tri_inverses([p["a_kk"] for p in probs], c, c)
    uws = [_mm(t, p["rhs"]) for t, p in zip(t_invs, probs)]
    aus = [_mm(p["a_qk"], uw) for p, uw in zip(probs, uws)]
    nms = [_mm_tn(p["k_dec"], uw) for p, uw in zip(probs, uws)]

    states = [p["s_ref"][p["slot"]] for p in probs]
    lhss = [jnp.concatenate([p["q"] * p["egc"] - au[:, LANES:], nm[:, LANES:]], axis=0)
            for p, au, nm in zip(probs, aus, nms)]
    ress = [_mm(lhs, s) for lhs, s in zip(lhss, states)]
    for p, s, au, nm, res in zip(probs, states, aus, nms, ress):
        slot = p["slot"]
        p["o_ref"][0, :, slot * LANES:(slot + 1) * LANES] = res[:c] + au[:, :LANES]
        p["s_ref"][slot] = s * jnp.exp(p["g_last"]) - res[c:] + nm[:, :LANES]


def _gdn_chunk(qkv, gcb, gcbt, hk, hv, dk, dv):
    b, t, _ = qkv.shape
    rep = hv // hk
    npairs = math.gcd(GDN_PAIRS, hk)
    assert dk == LANES and dv == LANES and t % TIME_TILE == 0 and GDN_CHUNK == TIME_TILE
    assert (2 * hk) % (npairs * rep) == 0
    nt = t // TIME_TILE
    qw, vw = npairs * LANES, npairs * rep * LANES
    kblk0 = hk // npairs
    vblk0 = 2 * hk // (npairs * rep)

    def specs(tile):
        return [pl.BlockSpec((1, TIME_TILE, qw), lambda b_, h, n: (b_, tile(n), h)),
                pl.BlockSpec((1, TIME_TILE, qw), lambda b_, h, n: (b_, tile(n), kblk0 + h)),
                pl.BlockSpec((1, TIME_TILE, vw), lambda b_, h, n: (b_, tile(n), vblk0 + h)),
                pl.BlockSpec((1, TIME_TILE, LANES), lambda b_, h, n: (b_, tile(n), 0)),
                pl.BlockSpec((1, LANES, TIME_TILE), lambda b_, h, n: (b_, 0, tile(n)))]

    fwd = lambda n: n
    bwd = lambda n: nt - 1 - n
    out_f = pl.BlockSpec((1, TIME_TILE, vw), lambda b_, h, n: (b_, n, h))
    out_b = pl.BlockSpec((1, TIME_TILE, vw), lambda b_, h, n: (b_, nt - 1 - n, h))
    shp = jax.ShapeDtypeStruct((b, t, hv * dv), F32)
    return pl.pallas_call(
        functools.partial(_gdn_chunk_kernel, rep=rep, hv=hv, npairs=npairs),
        grid=(b, hk // npairs, nt),
        in_specs=specs(fwd) + specs(bwd),
        out_specs=[out_f, out_b],
        out_shape=[shp, shp],
        scratch_shapes=[pltpu.VMEM((npairs * rep, dk, dv), F32), pltpu.VMEM((npairs * rep, dk, dv), F32)],
        compiler_params=_cparams(("parallel", "parallel", "arbitrary")),
        name="gdn_chunk",
    )(qkv, qkv, qkv, gcb, gcbt, qkv, qkv, qkv, gcb, gcbt)


def _gdn_out_prologue(o_f, o_b, z, nw):
    parts = []
    for hh in range(o_f.shape[1] // LANES):
        sl = slice(hh * LANES, (hh + 1) * LANES)
        o = o_f[:, sl] + o_b[:, sl]
        o = o * lax.rsqrt(jnp.mean(o * o, axis=-1, keepdims=True) + 1e-6) * nw[:, sl]
        zz = z[:, sl]
        parts.append((o * (zz * jax.nn.sigmoid(zz))).astype(BF16))
    return jnp.concatenate(parts, axis=1) if len(parts) > 1 else parts[0]


def _gdn_mixer_ln(x, w, alpha, ln_g, ln_b):
    b, t, dm = x.shape
    m = b * t
    x2 = x.reshape(m, dm)
    kd, vd, hk, hv, dk, dv = w["kd"], w["vd"], w["hk"], w["hv"], w["dk"], w["dv"]
    qkvz, = _fused_matmul([x2], w["w_qkvz"], tm=1024, tn=1024, tk=dm, name="gdn_in")
    gates, = _fused_matmul([x2], w["w_gates"], tm=1024, tn=LANES, tk=dm, name="gdn_in_gates")
    qkv = _gdn_conv(qkvz.reshape(b, t, -1), w["conv"], kd, vd, dk)
    gcb, gcbt = _gdn_gates(gates.reshape(b, t, -1), w["a_log"], w["dt_bias"], hv)
    o_f, o_b = _gdn_chunk(qkv, gcb, gcbt, hk, hv, dk, dv)
    out, = _fused_matmul(
        [o_f.reshape(m, vd), o_b.reshape(m, vd), qkvz], w["w_out"], a_col0=[0, 0, 2 * kd + vd],
        rowk_ins=[w["norm_row"]], e_ins=[x2], rown_ins=[ln_g, ln_b],
        prologue=_gdn_out_prologue, epilogue=_ln_epilogue(alpha), tm=512, tn=dm, tk=512, name="gdn_out")
    return out


def _shift_kernel(x_ref, xp_ref, xn_ref, o_ref, scr, *, tt, nt):
    _fill_halo(scr, x_ref, xp_ref, xn_ref, tt, nt)
    o_ref[0] = 0.5 * (scr[pl.ds(HALO - 1, tt), :] + scr[pl.ds(HALO + 1, tt), :]) - x_ref[0]


def _token_shift(x, tt=256, tc=512):
    b, t, ch = x.shape
    tt, tc = min(tt, t), min(tc, ch)
    assert t % tt == 0 and ch % tc == 0
    main, prev, nxt = _halo_specs(tt, tc, t)
    return pl.pallas_call(
        functools.partial(_shift_kernel, tt=tt, nt=t // tt),
        grid=(b, t // tt, ch // tc),
        in_specs=[main, prev, nxt],
        out_specs=pl.BlockSpec((1, tt, tc), lambda b_, i, c: (b_, i, c)),
        out_shape=jax.ShapeDtypeStruct((b, t, ch), F32),
        scratch_shapes=[pltpu.VMEM((tt + 2 * HALO, tc), F32)],
        compiler_params=_cparams(("parallel", "parallel", "parallel")),
        name="rwkv_shift",
    )(x, x, x)


def _mix_prologue(x, xx, mix):
    return (x + xx * mix).astype(BF16)


def _half_sum(x, lo):
    s0 = jnp.sum(jnp.where(lo, x, 0.0), axis=-1, keepdims=True)
    s1 = jnp.sum(jnp.where(lo, 0.0, x), axis=-1, keepdims=True)
    return jnp.where(lo, s0, s1)


def _rwkv_chunk_kernel(rf_ref, kf_ref, vf_ref, wf_ref, af_ref, rb_ref, kb_ref, vb_ref, wb_ref, ab_ref,
                       kk_ref, ka_ref, rk_ref, yf_ref, bf_ref, yb_ref, bb_ref, gf_ref, gb_ref, *, nblk):
    n = pl.program_id(2)

    @pl.when(n == 0)
    def _():
        gf_ref[...] = jnp.zeros_like(gf_ref)
        gb_ref[...] = jnp.zeros_like(gb_ref)

    c, nn, tt = RWKV_CHUNK, RWKV_N, TIME_TILE
    nch, nh = tt // c, LANES // nn
    lane = lax.broadcasted_iota(jnp.int32, (tt, LANES), 1)
    row = lax.broadcasted_iota(jnp.int32, (tt, LANES), 0)
    lo = lane < nn
    ri = lax.broadcasted_iota(jnp.int32, (tt, tt), 0)
    ci = lax.broadcasted_iota(jnp.int32, (tt, tt), 1)
    same = (ri // c) == (ci // c)
    dirs = ((False, rf_ref, kf_ref, vf_ref, wf_ref, af_ref, yf_ref, bf_ref, gf_ref),
            (True, rb_ref, kb_ref, vb_ref, wb_ref, ab_ref, yb_ref, bb_ref, gb_ref))

    probs = []
    for rev, r_ref, k_ref, v_ref, w_ref, a_ref, y_ref, bonus_ref, g_ref in dirs:
        incl = same & ((ci >= ri) if rev else (ci <= ri))
        strict = same & ((ci > ri) if rev else (ci < ri))
        ends = [cc * c if rev else (cc + 1) * c - 1 for cc in range(nch)]
        for blk in range(nblk):
            ls = slice(blk * LANES, (blk + 1) * LANES)
            r, k, v, lw, a = r_ref[0, :, ls], k_ref[0, :, ls], v_ref[0, :, ls], w_ref[0, :, ls], a_ref[0, :, ls]
            kkk = k * kk_ref[:, ls]
            kk = kkk * lax.rsqrt(_half_sum(kkk * kkk, lo) + 1e-6)
            kd = k * (1.0 + (a - 1.0) * ka_ref[:, ls])
            av = -kk
            bv = kk * a
            bonus_ref[0, :, ls] = _half_sum(r * kd * rk_ref[:, ls], lo) * v
            gc = _mm_hi(incl.astype(F32), lw)
            gends = [gc[e:e + 1] for e in ends]
            gend_tile = gends[0]
            for cc in range(1, nch):
                gend_tile = jnp.where(row >= cc * c, gends[cc], gend_tile)
            ee = jnp.exp(gend_tile - gc)
            ei = jnp.exp(-gc)
            at, kt, bt, rt = av * jnp.exp(gc - lw), kd * ei, bv * ei, r * jnp.exp(gc)
            kh, bh = kd * ee, bv * ee
            egends = [jnp.exp(g) for g in gends]
            for j in range(nh):
                hs = slice(j * nn, (j + 1) * nn)
                probs.append(dict(
                    rev=rev, incl=incl, strict=strict, y_ref=y_ref, g_ref=g_ref, blk=blk, j=j, slot=blk * nh + j,
                    at=at[:, hs], kt=kt[:, hs], bt=bt[:, hs], rt=rt[:, hs], v=v[:, hs], kh=kh[:, hs], bh=bh[:, hs],
                    egend=[e[:, hs] for e in egends]))

    bigs = [_mm_nt(jnp.concatenate([p["at"], p["rt"]], axis=0), jnp.concatenate([p["bt"], p["kt"]], axis=0))
            for p in probs]
    for p, big in zip(probs, bigs):
        p["a_ab"] = jnp.where(p["strict"], big[:tt, :tt], 0.0)
        p["a_ak"] = jnp.where(p["strict"], big[:tt, tt:], 0.0)
        p["m_rb"] = jnp.where(p["incl"], big[tt:, :tt], 0.0)
        p["m_rk"] = jnp.where(p["incl"], big[tt:, tt:], 0.0)
    avyvs = [_mm(jnp.concatenate([p["a_ak"], p["m_rk"]], axis=0), p["v"]) for p in probs]
    t_invs = _tri_inverses([-p["a_ab"] for p in probs], tt, c)
    tauvs = [_mm(t, jnp.concatenate([p["at"], av[:tt]], axis=1)) for t, p, av in zip(t_invs, probs, avyvs)]
    mtmus = [_mm(p["m_rb"], tauv) for p, tauv in zip(probs, tauvs)]
    for p, av, tauv, mtmu in zip(probs, avyvs, tauvs, mtmus):
        p["rq"] = p["rt"] + mtmu[:, :nn]
        p["yc"] = av[tt:] + mtmu[:, nn:]
        p["tauv"] = tauv

    def rows(cc):
        return slice(cc * c, (cc + 1) * c)

    mpns = [[_mm_tn(p["tauv"][rows(cc)], p["bh"][rows(cc)]) for cc in range(nch)] for p in probs]
    n1s = [[_mm_tn(p["v"][rows(cc)], p["kh"][rows(cc)]) for cc in range(nch)] for p in probs]
    states = [p["g_ref"][p["slot"]] for p in probs]
    ys = [[None] * nch for _ in probs]
    for step in range(nch):
        ccs = [nch - 1 - step if p["rev"] else step for p in probs]
        yparts = [_mm_nt(p["rq"][rows(cc)], g) for p, g, cc in zip(probs, states, ccs)]
        gms = [_mm(g, mpn[cc][:nn]) for g, mpn, cc in zip(states, mpns, ccs)]
        for i, (p, cc) in enumerate(zip(probs, ccs)):
            ys[i][cc] = yparts[i] + p["yc"][rows(cc)]
            states[i] = states[i] * p["egend"][cc] + gms[i] + (n1s[i][cc] + mpns[i][cc][nn:])
    for p, g in zip(probs, states):
        p["g_ref"][p["slot"]] = g
    for i in range(0, len(probs), nh):
        p = probs[i]
        tile = jnp.concatenate([jnp.concatenate([ys[i + j][cc] for j in range(nh)], axis=1) for cc in range(nch)],
                               axis=0)
        p["y_ref"][0, :, p["blk"] * LANES:(p["blk"] + 1) * LANES] = tile


def _rwkv_chunk(r, k, v, logw, a, k_k, k_a, r_k):
    b, t, dm = r.shape
    assert t % TIME_TILE == 0 and dm % LANES == 0 and LANES % RWKV_N == 0 and TIME_TILE % RWKV_CHUNK == 0
    nblk = math.gcd(RWKV_BLOCKS, dm // LANES)
    nt = t // TIME_TILE
    width = nblk * LANES
    fwd = lambda n: n
    bwd = lambda n: nt - 1 - n

    def tile(order):
        return pl.BlockSpec((1, TIME_TILE, width), lambda b_, h, n: (b_, order(n), h))

    row = pl.BlockSpec((1, width), lambda b_, h, n: (0, h))
    shp = jax.ShapeDtypeStruct((b, t, dm), F32)
    heads = width // RWKV_N
    return pl.pallas_call(
        functools.partial(_rwkv_chunk_kernel, nblk=nblk),
        grid=(b, dm // width, nt),
        in_specs=[tile(fwd)] * 5 + [tile(bwd)] * 5 + [row] * 3,
        out_specs=[tile(fwd), tile(fwd), tile(bwd), tile(bwd)],
        out_shape=[shp] * 4,
        scratch_shapes=[pltpu.VMEM((heads, RWKV_N, RWKV_N), F32), pltpu.VMEM((heads, RWKV_N, RWKV_N), F32)],
        compiler_params=_cparams(("parallel", "parallel", "arbitrary")),
        name="rwkv_chunk",
    )(r, k, v, logw[0], a[0], r, k, v, logw[1], a[1], k_k, k_a, r_k)


def _rwkv_out_prologue(y_f, y_b, bn_f, bn_b, g, ln_w, ln_b):
    parts = []
    for hh in range(y_f.shape[1] // LANES):
        sl = slice(hh * LANES, (hh + 1) * LANES)
        wkv = y_f[:, sl] + y_b[:, sl]
        lo = lax.broadcasted_iota(jnp.int32, wkv.shape, 1) < RWKV_N
        mu = _half_sum(wkv, lo) * (1.0 / RWKV_N)
        wc = wkv - mu
        var = _half_sum(wc * wc, lo) * (1.0 / RWKV_N)
        o = wc * lax.rsqrt(var + RWKV_GN_EPS) * ln_w[:, sl] + ln_b[:, sl] + (bn_f[:, sl] + bn_b[:, sl])
        parts.append((o * g[:, sl]).astype(BF16))
    return jnp.concatenate(parts, axis=1) if len(parts) > 1 else parts[0]


def _rwkv_mixer_ln(x, w, alpha, ln_g, ln_b):
    b, t, dm = x.shape
    m = b * t
    x2 = x.reshape(m, dm)
    xx = _token_shift(x).reshape(m, dm)
    mix = w["mix"]

    def proj(mi, wmat, epilogue=_identity_epilogue, dtype=F32, name="rwkv_proj"):
        out, = _fused_matmul([x2, xx], wmat, rowk_ins=[mix[mi:mi + 1]], prologue=_mix_prologue,
                             epilogue=epilogue, out_dtypes=(dtype,), tm=1024, tn=1024, tk=dm, name=name)
        return out

    r = proj(0, w["w_r"], name="rwkv_r")
    k = proj(2, w["w_k"], name="rwkv_k")
    v = proj(3, w["w_v"], name="rwkv_v")
    lw = proj(1, w["w1"], lambda acc: (jnp.tanh(acc),), BF16, "rwkv_w1")
    la = proj(4, w["a1"], _identity_epilogue, BF16, "rwkv_a1")
    lg = proj(5, w["g1"], lambda acc: (jax.nn.sigmoid(acc),), BF16, "rwkv_g1")

    def decay_epilogue(acc, w0):
        return (-jnp.exp(-_softplus(-(w0 + acc)) - 0.5),)

    def rate_epilogue(acc, a0):
        return (jax.nn.sigmoid(a0 + acc),)

    logw, a = [], []
    for d in range(2):
        out, = _fused_matmul([lw], w["w2"][d], a_col0=[d * LANES], rown_ins=[w["w0"][d:d + 1]],
                             epilogue=decay_epilogue, tm=1024, tn=1024, tk=LANES, name="rwkv_w2")
        logw.append(out.reshape(b, t, dm))
        out, = _fused_matmul([la], w["a2"][d], a_col0=[d * LANES], rown_ins=[w["a0"][d:d + 1]],
                             epilogue=rate_epilogue, tm=1024, tn=1024, tk=LANES, name="rwkv_a2")
        a.append(out.reshape(b, t, dm))
    g, = _fused_matmul([lg], w["g2"], tm=1024, tn=1024, tk=lg.shape[1], name="rwkv_g2")

    r3, k3, v3 = (z.reshape(b, t, dm) for z in (r, k, v))
    y_f, bn_f, y_b, bn_b = _rwkv_chunk(r3, k3, v3, logw, a, w["k_k"], w["k_a"], w["r_k"])
    out, = _fused_matmul(
        [z.reshape(m, dm) for z in (y_f, y_b, bn_f, bn_b)] + [g], w["w_o"],
        rowk_ins=[w["ln_w"], w["ln_b"]], e_ins=[x2], rown_ins=[ln_g, ln_b],
        prologue=_rwkv_out_prologue, epilogue=_ln_epilogue(alpha), tm=512, tn=dm, tk=512, name="rwkv_out")
    return out


def _relu2_epilogue(acc):
    return (jnp.square(jnp.maximum(acc, 0.0)),)


def _mlp_ln(x2, w_up, w_down, alpha, ln_g, ln_b):
    dm = x2.shape[1]
    h, = _fused_matmul([x2], w_up, epilogue=_relu2_epilogue, out_dtypes=(BF16,), tm=1024, tn=1024, tk=dm,
                       name="mlp_up")
    out, = _fused_matmul([h], w_down, e_ins=[x2], rown_ins=[ln_g, ln_b], epilogue=_ln_epilogue(alpha),
                         tm=512, tn=dm, tk=512, name="mlp_down")
    return out


def _ple_epilogue(acc, x, pp):
    return (x + jax.nn.sigmoid(acc) * pp,)


def _ple(x2, p2, w_proj, w_gate):
    dm = x2.shape[1]
    pp, = _fused_matmul([p2], w_proj, tm=1024, tn=1024, tk=p2.shape[1], name="ple_proj")
    out, = _fused_matmul([x2], w_gate, e_ins=[x2, pp], epilogue=_ple_epilogue, tm=512, tn=1024, tk=dm,
                         name="ple_gate")
    return out


def _pad_rows(w, rows):
    return jnp.pad(w, ((0, rows - w.shape[0]), (0, 0)))


def _pad_cols(w, cols):
    return jnp.pad(w, ((0, 0), (0, cols - w.shape[1])))


def _gdn_weights(w_in, conv, a_log, dt_bias, norm, w_out):
    hv, dv = a_log.shape[-1], norm.shape[-1]
    vd = hv * dv
    kd = (conv.shape[-1] - vd) // 2
    dk = dv
    assert w_in.shape[1] == 2 * kd + 2 * vd + 4 * hv and w_out.shape[0] == vd
    nz = 2 * kd + 2 * vd
    return dict(kd=kd, vd=vd, hk=kd // dk, hv=hv, dk=dk, dv=dv,
                w_qkvz=w_in[:, :nz].astype(BF16), w_gates=w_in[:, nz:].astype(BF16), conv=conv.astype(F32),
                a_log=a_log, dt_bias=dt_bias, norm_row=jnp.tile(norm.astype(F32), hv).reshape(1, vd),
                w_out=w_out.astype(BF16))


def _rwkv_weights(mix, w_rkv, w0, w1, w2, a0, a1, a2, g1, g2, k_k, k_a, r_k, ln_w, ln_b, w_o):
    dm = w_o.shape[0]
    assert w1.shape[-1] <= LANES and a1.shape[-1] <= LANES and r_k.size == dm and r_k.shape[-1] == RWKV_N
    cat = lambda ws: jnp.concatenate([_pad_cols(ws[d], LANES) for d in range(2)], axis=1).astype(BF16)
    row = lambda z: z.reshape(1, dm).astype(F32)
    return dict(mix=mix.astype(F32), w_r=w_rkv[0].astype(BF16), w_k=w_rkv[1].astype(BF16),
                w_v=w_rkv[2].astype(BF16), w1=cat(w1), a1=cat(a1), g1=g1.astype(BF16),
                w2=[_pad_rows(w2[d], LANES).astype(BF16) for d in range(2)],
                a2=[_pad_rows(a2[d], LANES).astype(BF16) for d in range(2)],
                g2=g2.astype(BF16), w0=w0.astype(F32), a0=a0.astype(F32),
                k_k=row(k_k), k_a=row(k_a), r_k=row(r_k), ln_w=row(ln_w), ln_b=row(ln_b), w_o=w_o.astype(BF16))


def _trunk(x, p, layers, alpha):
    b, t, dm = x.shape
    m = b * t
    for i, lyr in enumerate(layers):
        mixer = _gdn_mixer_ln if lyr["kind"] == "gdn" else _rwkv_mixer_ln
        x2 = mixer(x, lyr["mixer"], alpha, lyr["ln_g"][0:1], lyr["ln_b"][0:1])
        x2 = _mlp_ln(x2, lyr["w_up"], lyr["w_down"], alpha, lyr["ln_g"][1:2], lyr["ln_b"][1:2])
        x2 = _ple(x2, p[i].reshape(m, -1), lyr["w_proj"], lyr["w_gate"])
        x = x2.reshape(b, t, dm)
    return x


def kernel(x_prompt, x_sample, p_prompt, p_sample, gdn_w_in, gdn_conv, gdn_a_log, gdn_dt_bias, gdn_norm, gdn_w_out, rwkv_mix, rwkv_w_rkv, rwkv_w0, rwkv_w1, rwkv_w2, rwkv_a0, rwkv_a1, rwkv_a2, rwkv_g1, rwkv_g2, rwkv_k_k, rwkv_k_a, rwkv_r_k, rwkv_ln_w, rwkv_ln_b, rwkv_w_o, ln_g, ln_b, mlp_w_up, mlp_w_down, ple_w_proj, ple_w_gate):
    depth = ln_g.shape[0]
    alpha = (2 * depth) ** 0.25
    layers = []
    for i in range(depth):
        j = i // 2
        if i % 2 == 0:
            kind = "gdn"
            mixer = _gdn_weights(gdn_w_in[j], gdn_conv[j], gdn_a_log[j], gdn_dt_bias[j], gdn_norm[j], gdn_w_out[j])
        else:
            kind = "rwkv"
            mixer = _rwkv_weights(rwkv_mix[j], rwkv_w_rkv[j], rwkv_w0[j], rwkv_w1[j], rwkv_w2[j], rwkv_a0[j],
                                  rwkv_a1[j], rwkv_a2[j], rwkv_g1[j], rwkv_g2[j], rwkv_k_k[j], rwkv_k_a[j],
                                  rwkv_r_k[j], rwkv_ln_w[j], rwkv_ln_b[j], rwkv_w_o[j])
        layers.append(dict(kind=kind, mixer=mixer, ln_g=ln_g[i].astype(F32), ln_b=ln_b[i].astype(F32),
                           w_up=mlp_w_up[i].astype(BF16), w_down=mlp_w_down[i].astype(BF16),
                           w_proj=ple_w_proj[i].astype(BF16), w_gate=ple_w_gate[i].astype(BF16)))
    y_prompt = _trunk(x_prompt, p_prompt, layers, alpha)
    y_sample = _trunk(x_sample, p_sample, layers, alpha)
    return (y_prompt, y_sample)
```

```python
import functools
import math

import jax
import jax.numpy as jnp
from jax import lax
from jax.experimental import pallas as pl
from jax.experimental.pallas import tpu as pltpu

F32 = jnp.float32
BF16 = jnp.bfloat16
HIGHEST = lax.Precision.HIGHEST

LANES = 128
HALO = 8
TIME_TILE = 128
GDN_CHUNK = TIME_TILE
RWKV_CHUNK = 64
GDN_PAIRS = 4
RWKV_BLOCKS = 4
INV_BASE = 8
VMEM_LIMIT = 56 * 1024 * 1024

LN_EPS = 1e-5
RWKV_GN_EPS = 64e-5
RWKV_N = 64
GDN_CONV = 5


def _cparams(sem):
    return pltpu.CompilerParams(dimension_semantics=sem, vmem_limit_bytes=VMEM_LIMIT)


def _mm_kernel(*refs, na, nrk, ne, nrn, nout, nk, prologue, epilogue):
    a_refs = refs[:na]
    rk_refs = refs[na:na + nrk]
    w_ref = refs[na + nrk]
    e_refs = refs[na + nrk + 1:na + nrk + 1 + ne]
    rn_refs = refs[na + nrk + 1 + ne:na + nrk + 1 + ne + nrn]
    out_refs = refs[na + nrk + 1 + ne + nrn:na + nrk + 1 + ne + nrn + nout]

    a = prologue(*[r[...] for r in a_refs], *[r[...] for r in rk_refs])
    part = jnp.dot(a, w_ref[...], preferred_element_type=F32)

    def finish(acc):
        outs = epilogue(acc, *[r[...] for r in e_refs], *[r[...] for r in rn_refs])
        for o_ref, o in zip(out_refs, outs):
            o_ref[...] = o.astype(o_ref.dtype)

    if nk == 1:
        finish(part)
        return

    acc_ref = refs[-1]
    k = pl.program_id(2)

    @pl.when(k == 0)
    def _():
        acc_ref[...] = part

    @pl.when(k > 0)
    def _():
        acc_ref[...] += part

    @pl.when(k == nk - 1)
    def _():
        finish(acc_ref[...])


def _cast_bf16(a):
    return a.astype(BF16)


def _identity_epilogue(acc):
    return (acc,)


def _fused_matmul(a_ins, w, *, a_col0=None, rowk_ins=(), rowk_seg_tiles=None, e_ins=(), rown_ins=(),
                  prologue=_cast_bf16, epilogue=_identity_epilogue, out_dtypes=(F32,), tm=512, tn=512, tk=512,
                  name="mm"):
    m = a_ins[0].shape[0]
    kdim, n = w.shape
    tm, tn, tk = min(tm, m), min(tn, n), min(tk, kdim)
    assert m % tm == 0 and n % tn == 0 and kdim % tk == 0, (m, n, kdim, tm, tn, tk)
    nk = kdim // tk
    a_col0 = [0] * len(a_ins) if a_col0 is None else a_col0
    assert all(c0 % tk == 0 for c0 in a_col0)

    def a_spec(c0):
        return pl.BlockSpec((tm, tk), lambda i, j, k: (i, c0 // tk + k))

    if rowk_seg_tiles is None:
        rowk_spec = pl.BlockSpec((1, tk), lambda i, j, k: (0, k))
    else:
        rowk_spec = pl.BlockSpec((None, 1, tk), lambda i, j, k: (j // rowk_seg_tiles, 0, k))
    in_specs = ([a_spec(c0) for c0 in a_col0]
                + [rowk_spec for _ in rowk_ins]
                + [pl.BlockSpec((tk, tn), lambda i, j, k: (k, j))]
                + [pl.BlockSpec((tm, tn), lambda i, j, k: (i, j)) for _ in e_ins]
                + [pl.BlockSpec((1, tn), lambda i, j, k: (0, j)) for _ in rown_ins])
    out_specs = [pl.BlockSpec((tm, tn), lambda i, j, k: (i, j)) for _ in out_dtypes]
    out_shape = [jax.ShapeDtypeStruct((m, n), dt) for dt in out_dtypes]
    kern = functools.partial(_mm_kernel, na=len(a_ins), nrk=len(rowk_ins), ne=len(e_ins), nrn=len(rown_ins),
                             nout=len(out_dtypes), nk=nk, prologue=prologue, epilogue=epilogue)
    outs = pl.pallas_call(
        kern,
        grid=(m // tm, n // tn, nk),
        in_specs=in_specs,
        out_specs=out_specs,
        out_shape=out_shape,
        scratch_shapes=[pltpu.VMEM((tm, tn), F32)] if nk > 1 else [],
        compiler_params=_cparams(("parallel", "parallel", "arbitrary")),
        name=name,
    )(*a_ins, *rowk_ins, w, *e_ins, *rown_ins)
    return outs


def _ln_epilogue(alpha):
    def epilogue(acc, xres, g, b):
        y = alpha * xres + acc
        mu = jnp.mean(y, axis=-1, keepdims=True)
        yc = y - mu
        var = jnp.mean(yc * yc, axis=-1, keepdims=True)
        return (yc * lax.rsqrt(var + LN_EPS) * g + b,)
    return epilogue


def _halo_specs(tt, tc, t_total):
    r = tt // HALO
    last = t_total // HALO - 1
    main = pl.BlockSpec((1, tt, tc), lambda b, i, c: (b, i, c))
    prev = pl.BlockSpec((1, HALO, tc), lambda b, i, c: (b, jnp.maximum(i * r - 1, 0), c))
    nxt = pl.BlockSpec((1, HALO, tc), lambda b, i, c: (b, jnp.minimum((i + 1) * r, last), c))
    return main, prev, nxt


def _fill_halo(scr, x_ref, xp_ref, xn_ref, tt, nt):
    i = pl.program_id(1)
    scr[0:HALO, :] = jnp.where(i > 0, xp_ref[0], 0.0)
    scr[HALO:HALO + tt, :] = x_ref[0]
    scr[HALO + tt:2 * HALO + tt, :] = jnp.where(i < nt - 1, xn_ref[0], 0.0)


def _gdn_conv_kernel(x_ref, xp_ref, xn_ref, w_ref, o_ref, scr, *, tt, tc, nt, nq_tiles, nqk_tiles, q_scale):
    _fill_halo(scr, x_ref, xp_ref, xn_ref, tt, nt)
    pad = (GDN_CONV - 1) // 2
    acc = None
    for j in range(GDN_CONV):
        term = w_ref[j:j + 1, :] * scr[pl.ds(HALO - pad + j, tt), :]
        acc = term if acc is None else acc + term
    y = acc * jax.nn.sigmoid(acc)
    c = pl.program_id(2)
    is_qk = c < nqk_tiles
    scale = jnp.where(c < nq_tiles, q_scale, 1.0).astype(F32)
    for hh in range(tc // LANES):
        blk = y[:, hh * LANES:(hh + 1) * LANES]
        ss = jnp.sum(blk * blk, axis=-1, keepdims=True)
        nrm = blk * lax.rsqrt(ss + 1e-6) * scale
        o_ref[0, :, hh * LANES:(hh + 1) * LANES] = jnp.where(is_qk, nrm, blk)


def _gdn_conv(qkv, conv_w, kd, vd, dk, tt=512, tc=1024):
    b, t, _ = qkv.shape
    ch = 2 * kd + vd
    tt = min(tt, t)
    assert t % tt == 0 and ch % tc == 0 and kd % tc == 0 and dk == LANES
    main, prev, nxt = _halo_specs(tt, tc, t)
    kern = functools.partial(_gdn_conv_kernel, tt=tt, tc=tc, nt=t // tt, nq_tiles=kd // tc,
                             nqk_tiles=2 * kd // tc, q_scale=dk ** -0.5)
    return pl.pallas_call(
        kern,
        grid=(b, t // tt, ch // tc),
        in_specs=[main, prev, nxt, pl.BlockSpec((GDN_CONV, tc), lambda b_, i, c: (0, c))],
        out_specs=pl.BlockSpec((1, tt, tc), lambda b_, i, c: (b_, i, c)),
        out_shape=jax.ShapeDtypeStruct((b, t, ch), F32),
        scratch_shapes=[pltpu.VMEM((tt + 2 * HALO, tc), F32)],
        compiler_params=_cparams(("parallel", "parallel", "parallel")),
        name="gdn_conv",
    )(qkv, qkv, qkv, conv_w)


def _chunk_tri_masks(rows, chunk):
    ri = lax.broadcasted_iota(jnp.int32, (rows, rows), 0)
    ci = lax.broadcasted_iota(jnp.int32, (rows, rows), 1)
    same = (ri // chunk) == (ci // chunk)
    return same & (ci <= ri), same & (ci >= ri)


def _softplus(x):
    return jnp.maximum(x, 0.0) + jnp.log1p(jnp.exp(-jnp.abs(x)))


def _gdn_gates_kernel(g_ref, alog_ref, dtb_ref, o_ref, ot_ref, *, hv):
    x = g_ref[0]
    g = -jnp.exp(alog_ref[...]) * _softplus(x + dtb_ref[...])
    lower, upper = _chunk_tri_masks(TIME_TILE, GDN_CHUNK)
    gc_f = jnp.dot(lower.astype(F32), g, precision=HIGHEST, preferred_element_type=F32)
    gc_b = jnp.dot(upper.astype(F32), g, precision=HIGHEST, preferred_element_type=F32)
    lane = lax.broadcasted_iota(jnp.int32, x.shape, 1)
    out = jnp.where(lane < hv, gc_f, jnp.where(lane < 2 * hv, gc_b, jax.nn.sigmoid(x)))
    o_ref[0] = out
    ot_ref[0] = out.T


def _gdn_gates(gates, a_log, dt_bias, hv):
    b, t, ch = gates.shape
    assert ch == LANES == 4 * hv and t % TIME_TILE == 0
    pad = jnp.zeros((1, 2 * hv), F32)
    alog = jnp.concatenate([a_log.reshape(1, 2 * hv).astype(F32), pad], axis=1)
    dtb = jnp.concatenate([dt_bias.reshape(1, 2 * hv).astype(F32), pad], axis=1)
    row = pl.BlockSpec((1, LANES), lambda b_, i: (0, 0))
    return pl.pallas_call(
        functools.partial(_gdn_gates_kernel, hv=hv),
        grid=(b, t // TIME_TILE),
        in_specs=[pl.BlockSpec((1, TIME_TILE, LANES), lambda b_, i: (b_, i, 0)), row, row],
        out_specs=[pl.BlockSpec((1, TIME_TILE, LANES), lambda b_, i: (b_, i, 0)),
                   pl.BlockSpec((1, LANES, TIME_TILE), lambda b_, i: (b_, 0, i))],
        out_shape=[jax.ShapeDtypeStruct((b, t, LANES), F32), jax.ShapeDtypeStruct((b, LANES, t), F32)],
        compiler_params=_cparams(("parallel", "parallel")),
        name="gdn_gates",
    )(gates, alog, dtb)


def _mm(a, b):
    return jnp.dot(a.astype(BF16), b.astype(BF16), preferred_element_type=F32)


def _mm_nt(a, b):
    return lax.dot_general(a.astype(BF16), b.astype(BF16), (((1,), (1,)), ((), ())), preferred_element_type=F32)


def _mm_tn(a, b):
    return lax.dot_general(a.astype(BF16), b.astype(BF16), (((0,), (0,)), ((), ())), preferred_element_type=F32)


def _mm_hi(a, b):
    return jnp.dot(a, b, precision=HIGHEST, preferred_element_type=F32)


def _tri_inverses(ms, size, top):
    ri = lax.broadcasted_iota(jnp.int32, (size, size), 0)
    ci = lax.broadcasted_iota(jnp.int32, (size, size), 1)
    eye = (ri == ci).astype(F32)
    base_bits = int(math.log2(INV_BASE))
    base = (ri >> base_bits) == (ci >> base_bits)
    xps = [jnp.where(base, -m, 0.0) for m in ms]
    ps = [eye + x for x in xps]
    for _ in range(base_bits - 1):
        xps = [_mm(xp, xp) for xp in xps]
        ps = [p + _mm(p, xp) for p, xp in zip(ps, xps)]
    bits = base_bits
    while (1 << bits) < top:
        pair = ((ri >> (bits + 1)) == (ci >> (bits + 1))) & ((ri >> bits) != (ci >> bits))
        qs = [_mm(jnp.where(pair, m, 0.0), p) for m, p in zip(ms, ps)]
        ps = [p - _mm(p, q) for p, q in zip(ps, qs)]
        bits += 1
    return ps


def _pick_lane(tile, lane_idx):
    lane = lax.broadcasted_iota(jnp.int32, tile.shape, 1)
    return jnp.sum(jnp.where(lane == lane_idx, tile, 0.0), axis=1, keepdims=True)


def _gdn_chunk_kernel(qf_ref, kf_ref, vf_ref, gf_ref, gtf_ref, qb_ref, kb_ref, vb_ref, gb_ref, gtb_ref,
                      of_ref, ob_ref, sf_ref, sb_ref, *, rep, hv, npairs):
    n = pl.program_id(2)
    hg = pl.program_id(1)

    @pl.when(n == 0)
    def _():
        sf_ref[...] = jnp.zeros_like(sf_ref)
        sb_ref[...] = jnp.zeros_like(sb_ref)

    c = GDN_CHUNK
    ri = lax.broadcasted_iota(jnp.int32, (c, c), 0)
    ci = lax.broadcasted_iota(jnp.int32, (c, c), 1)
    dirs = ((False, qf_ref, kf_ref, vf_ref, gf_ref, gtf_ref, of_ref, sf_ref),
            (True, qb_ref, kb_ref, vb_ref, gb_ref, gtb_ref, ob_ref, sb_ref))

    pairs = []
    for rev, q_ref, k_ref, *_ in dirs:
        for i in range(npairs):
            pairs.append((q_ref[0, :, i * LANES:(i + 1) * LANES], k_ref[0, :, i * LANES:(i + 1) * LANES]))
    qks = [_mm_nt(q, k) for q, k in pairs]
    kks = [_mm_nt(k, k) for _, k in pairs]

    probs = []
    for d, (rev, q_ref, k_ref, v_ref, g_ref, gt_ref, o_ref, s_ref) in enumerate(dirs):
        incl = (ci >= ri) if rev else (ci <= ri)
        strict = (ci > ri) if rev else (ci < ri)
        g_tile = g_ref[0]
        end = 0 if rev else c - 1
        for i in range(npairs):
            q, k = pairs[d * npairs + i]
            for r in range(rep):
                slot = i * rep + r
                h = (hg * npairs + i) * rep + r
                gcol = _pick_lane(g_tile, d * hv + h)
                bcol = _pick_lane(g_tile, 2 * hv + d * hv + h)
                grow = gt_ref[0, pl.ds(d * hv + h, 1), :]
                decay = jnp.where(incl, jnp.exp(jnp.where(incl, gcol - grow, 0.0)), 0.0)
                g_last = gcol[end:end + 1]
                egc = jnp.exp(gcol)
                v = v_ref[0, :, slot * LANES:(slot + 1) * LANES]
                probs.append(dict(
                    q=q, k=k, egc=egc, g_last=g_last, slot=slot, o_ref=o_ref, s_ref=s_ref,
                    a_kk=jnp.where(strict, kks[d * npairs + i] * bcol * decay, 0.0),
                    a_qk=qks[d * npairs + i] * decay,
                    rhs=jnp.concatenate([v * bcol, k * (bcol * egc)], axis=1),
                    k_dec=k * jnp.exp(g_last - gcol)))

    t_invs = _tri_inverses([p["a_kk"] for p in probs], c, c)
    uws = [_mm(t, p["rhs"]) for t, p in zip(t_invs, probs)]
    aus = [_mm(p["a_qk"], uw) for p, uw in zip(probs, uws)]
    nms = [_mm_tn(p["k_dec"], uw) for p, uw in zip(probs, uws)]

    states = [p["s_ref"][p["slot"]] for p in probs]
    lhss = [jnp.concatenate([p["q"] * p["egc"] - au[:, LANES:], nm[:, LANES:]], axis=0)
            for p, au, nm in zip(probs, aus, nms)]
    ress = [_mm(lhs, s) for lhs, s in zip(lhss, states)]
    for p, s, au, nm, res in zip(probs, states, aus, nms, ress):
        slot = p["slot"]
        p["o_ref"][0, :, slot * LANES:(slot + 1) * LANES] = res[:c] + au[:, :LANES]
        p["s_ref"][slot] = s * jnp.exp(p["g_last"]) - res[c:] + nm[:, :LANES]


def _gdn_chunk(qkv, gcb, gcbt, hk, hv, dk, dv):
    b, t, _ = qkv.shape
    rep = hv // hk
    npairs = math.gcd(GDN_PAIRS, hk)
    assert dk == LANES and dv == LANES and t % TIME_TILE == 0 and GDN_CHUNK == TIME_TILE
    assert (2 * hk) % (npairs * rep) == 0
    nt = t // TIME_TILE
    qw, vw = npairs * LANES, npairs * rep * LANES
    kblk0 = hk // npairs
    vblk0 = 2 * hk // (npairs * rep)

    def specs(tile):
        return [pl.BlockSpec((1, TIME_TILE, qw), lambda b_, h, n: (b_, tile(n), h)),
                pl.BlockSpec((1, TIME_TILE, qw), lambda b_, h, n: (b_, tile(n), kblk0 + h)),
                pl.BlockSpec((1, TIME_TILE, vw), lambda b_, h, n: (b_, tile(n), vblk0 + h)),
                pl.BlockSpec((1, TIME_TILE, LANES), lambda b_, h, n: (b_, tile(n), 0)),
                pl.BlockSpec((1, LANES, TIME_TILE), lambda b_, h, n: (b_, 0, tile(n)))]

    fwd = lambda n: n
    bwd = lambda n: nt - 1 - n
    out_f = pl.BlockSpec((1, TIME_TILE, vw), lambda b_, h, n: (b_, n, h))
    out_b = pl.BlockSpec((1, TIME_TILE, vw), lambda b_, h, n: (b_, nt - 1 - n, h))
    shp = jax.ShapeDtypeStruct((b, t, hv * dv), F32)
    return pl.pallas_call(
        functools.partial(_gdn_chunk_kernel, rep=rep, hv=hv, npairs=npairs),
        grid=(b, hk // npairs, nt),
        in_specs=specs(fwd) + specs(bwd),
        out_specs=[out_f, out_b],
        out_shape=[shp, shp],
        scratch_shapes=[pltpu.VMEM((npairs * rep, dk, dv), F32), pltpu.VMEM((npairs * rep, dk, dv), F32)],
        compiler_params=_cparams(("parallel", "parallel", "arbitrary")),
        name="gdn_chunk",
    )(qkv, qkv, qkv, gcb, gcbt, qkv, qkv, qkv, gcb, gcbt)


def _gdn_out_prologue(o_f, o_b, z, nw):
    parts = []
    for hh in range(o_f.shape[1] // LANES):
        sl = slice(hh * LANES, (hh + 1) * LANES)
        o = o_f[:, sl] + o_b[:, sl]
        o = o * lax.rsqrt(jnp.mean(o * o, axis=-1, keepdims=True) + 1e-6) * nw[:, sl]
        zz = z[:, sl]
        parts.append((o * (zz * jax.nn.sigmoid(zz))).astype(BF16))
    return jnp.concatenate(parts, axis=1) if len(parts) > 1 else parts[0]


def _gdn_mixer_ln(x, w, alpha, ln_g, ln_b):
    b, t, dm = x.shape
    m = b * t
    x2 = x.reshape(m, dm)
    kd, vd, hk, hv, dk, dv = w["kd"], w["vd"], w["hk"], w["hv"], w["dk"], w["dv"]
    qkvz, = _fused_matmul([x2], w["w_qkvz"], tm=1024, tn=1024, tk=dm, name="gdn_in")
    gates, = _fused_matmul([x2], w["w_gates"], tm=1024, tn=LANES, tk=dm, name="gdn_in_gates")
    qkv = _gdn_conv(qkvz.reshape(b, t, -1), w["conv"], kd, vd, dk)
    gcb, gcbt = _gdn_gates(gates.reshape(b, t, -1), w["a_log"], w["dt_bias"], hv)
    o_f, o_b = _gdn_chunk(qkv, gcb, gcbt, hk, hv, dk, dv)
    out, = _fused_matmul(
        [o_f.reshape(m, vd), o_b.reshape(m, vd), qkvz], w["w_out"], a_col0=[0, 0, 2 * kd + vd],
        rowk_ins=[w["norm_row"]], e_ins=[x2], rown_ins=[ln_g, ln_b],
        prologue=_gdn_out_prologue, epilogue=_ln_epilogue(alpha), tm=512, tn=dm, tk=1024, name="gdn_out")
    return out


def _shift_kernel(x_ref, xp_ref, xn_ref, o_ref, scr, *, tt, nt):
    _fill_halo(scr, x_ref, xp_ref, xn_ref, tt, nt)
    o_ref[0] = 0.5 * (scr[pl.ds(HALO - 1, tt), :] + scr[pl.ds(HALO + 1, tt), :]) - x_ref[0]


def _token_shift(x, tt=256, tc=512):
    b, t, ch = x.shape
    tt, tc = min(tt, t), min(tc, ch)
    assert t % tt == 0 and ch % tc == 0
    main, prev, nxt = _halo_specs(tt, tc, t)
    return pl.pallas_call(
        functools.partial(_shift_kernel, tt=tt, nt=t // tt),
        grid=(b, t // tt, ch // tc),
        in_specs=[main, prev, nxt],
        out_specs=pl.BlockSpec((1, tt, tc), lambda b_, i, c: (b_, i, c)),
        out_shape=jax.ShapeDtypeStruct((b, t, ch), F32),
        scratch_shapes=[pltpu.VMEM((tt + 2 * HALO, tc), F32)],
        compiler_params=_cparams(("parallel", "parallel", "parallel")),
        name="rwkv_shift",
    )(x, x, x)


def _mix_prologue(x, xx, mix):
    return (x + xx * mix).astype(BF16)


def _half_sum(x, lo):
    s0 = jnp.sum(jnp.where(lo, x, 0.0), axis=-1, keepdims=True)
    s1 = jnp.sum(jnp.where(lo, 0.0, x), axis=-1, keepdims=True)
    return jnp.where(lo, s0, s1)


def _rwkv_chunk_kernel(rf_ref, kf_ref, vf_ref, lwf_ref, laf_ref, rb_ref, kb_ref, vb_ref, lwb_ref, lab_ref,
                       w2f_ref, a2f_ref, w0f_ref, a0f_ref, w2b_ref, a2b_ref, w0b_ref, a0b_ref,
                       kk_ref, ka_ref, rk_ref, yf_ref, bf_ref, yb_ref, bb_ref, gf_ref, gb_ref, *, nblk):
    n = pl.program_id(2)

    @pl.when(n == 0)
    def _():
        gf_ref[...] = jnp.zeros_like(gf_ref)
        gb_ref[...] = jnp.zeros_like(gb_ref)

    c, nn, tt = RWKV_CHUNK, RWKV_N, TIME_TILE
    nch, nh = tt // c, LANES // nn
    lane = lax.broadcasted_iota(jnp.int32, (tt, LANES), 1)
    row = lax.broadcasted_iota(jnp.int32, (tt, LANES), 0)
    lo = lane < nn
    ri = lax.broadcasted_iota(jnp.int32, (tt, tt), 0)
    ci = lax.broadcasted_iota(jnp.int32, (tt, tt), 1)
    same = (ri // c) == (ci // c)
    dirs = ((False, rf_ref, kf_ref, vf_ref, lwf_ref, laf_ref, w2f_ref, a2f_ref, w0f_ref, a0f_ref,
             yf_ref, bf_ref, gf_ref),
            (True, rb_ref, kb_ref, vb_ref, lwb_ref, lab_ref, w2b_ref, a2b_ref, w0b_ref, a0b_ref,
             yb_ref, bb_ref, gb_ref))

    probs = []
    for rev, r_ref, k_ref, v_ref, lw_ref, la_ref, w2_ref, a2_ref, w0_ref, a0_ref, y_ref, bonus_ref, g_ref in dirs:
        incl = same & ((ci >= ri) if rev else (ci <= ri))
        strict = same & ((ci > ri) if rev else (ci < ri))
        ends = [cc * c if rev else (cc + 1) * c - 1 for cc in range(nch)]
        lw_all = -jnp.exp(-_softplus(-(w0_ref[...] + jnp.dot(lw_ref[0], w2_ref[...], preferred_element_type=F32)))
                          - 0.5)
        a_all = jax.nn.sigmoid(a0_ref[...] + jnp.dot(la_ref[0], a2_ref[...], preferred_element_type=F32))
        for blk in range(nblk):
            ls = slice(blk * LANES, (blk + 1) * LANES)
            r, k, v, lw, a = r_ref[0, :, ls], k_ref[0, :, ls], v_ref[0, :, ls], lw_all[:, ls], a_all[:, ls]
            kkk = k * kk_ref[:, ls]
            kk = kkk * lax.rsqrt(_half_sum(kkk * kkk, lo) + 1e-6)
            kd = k * (1.0 + (a - 1.0) * ka_ref[:, ls])
            av = -kk
            bv = kk * a
            bonus_ref[0, :, ls] = _half_sum(r * kd * rk_ref[:, ls], lo) * v
            gc = _mm_hi(incl.astype(F32), lw)
            gends = [gc[e:e + 1] for e in ends]
            gend_tile = gends[0]
            for cc in range(1, nch):
                gend_tile = jnp.where(row >= cc * c, gends[cc], gend_tile)
            ee = jnp.exp(gend_tile - gc)
            ei = jnp.exp(-gc)
            at, kt, bt, rt = av * jnp.exp(gc - lw), kd * ei, bv * ei, r * jnp.exp(gc)
            kh, bh = kd * ee, bv * ee
            egends = [jnp.exp(g) for g in gends]
            for j in range(nh):
                hs = slice(j * nn, (j + 1) * nn)
                probs.append(dict(
                    rev=rev, incl=incl, strict=strict, y_ref=y_ref, g_ref=g_ref, blk=blk, j=j, slot=blk * nh + j,
                    at=at[:, hs], kt=kt[:, hs], bt=bt[:, hs], rt=rt[:, hs], v=v[:, hs], kh=kh[:, hs], bh=bh[:, hs],
                    egend=[e[:, hs] for e in egends]))

    bigs = [_mm_nt(jnp.concatenate([p["at"], p["rt"]], axis=0), jnp.concatenate([p["bt"], p["kt"]], axis=0))
            for p in probs]
    for p, big in zip(probs, bigs):
        p["a_ab"] = jnp.where(p["strict"], big[:tt, :tt], 0.0)
        p["a_ak"] = jnp.where(p["strict"], big[:tt, tt:], 0.0)
        p["m_rb"] = jnp.where(p["incl"], big[tt:, :tt], 0.0)
        p["m_rk"] = jnp.where(p["incl"], big[tt:, tt:], 0.0)
    avyvs = [_mm(jnp.concatenate([p["a_ak"], p["m_rk"]], axis=0), p["v"]) for p in probs]
    t_invs = _tri_inverses([-p["a_ab"] for p in probs], tt, c)
    tauvs = [_mm(t, jnp.concatenate([p["at"], av[:tt]], axis=1)) for t, p, av in zip(t_invs, probs, avyvs)]
    mtmus = [_mm(p["m_rb"], tauv) for p, tauv in zip(probs, tauvs)]
    for p, av, tauv, mtmu in zip(probs, avyvs, tauvs, mtmus):
        p["rq"] = p["rt"] + mtmu[:, :nn]
        p["yc"] = av[tt:] + mtmu[:, nn:]
        p["tauv"] = tauv

    def rows(cc):
        return slice(cc * c, (cc + 1) * c)

    mpns = [[_mm_tn(p["tauv"][rows(cc)], p["bh"][rows(cc)]) for cc in range(nch)] for p in probs]
    n1s = [[_mm_tn(p["v"][rows(cc)], p["kh"][rows(cc)]) for cc in range(nch)] for p in probs]
    states = [p["g_ref"][p["slot"]] for p in probs]
    ys = [[None] * nch for _ in probs]
    for step in range(nch):
        ccs = [nch - 1 - step if p["rev"] else step for p in probs]
        yparts = [_mm_nt(p["rq"][rows(cc)], g) for p, g, cc in zip(probs, states, ccs)]
        gms = [_mm(g, mpn[cc][:nn]) for g, mpn, cc in zip(states, mpns, ccs)]
        for i, (p, cc) in enumerate(zip(probs, ccs)):
            ys[i][cc] = yparts[i] + p["yc"][rows(cc)]
            states[i] = states[i] * p["egend"][cc] + gms[i] + (n1s[i][cc] + mpns[i][cc][nn:])
    for p, g in zip(probs, states):
        p["g_ref"][p["slot"]] = g
    for i in range(0, len(probs), nh):
        p = probs[i]
        tile = jnp.concatenate([jnp.concatenate([ys[i + j][cc] for j in range(nh)], axis=1) for cc in range(nch)],
                               axis=0)
        p["y_ref"][0, :, p["blk"] * LANES:(p["blk"] + 1) * LANES] = tile


def _rwkv_chunk(rkv, lora, w, dm):
    b, t, _ = rkv.shape
    assert t % TIME_TILE == 0 and dm % LANES == 0 and LANES % RWKV_N == 0 and TIME_TILE % RWKV_CHUNK == 0
    nblk = math.gcd(RWKV_BLOCKS, dm // LANES)
    nt = t // TIME_TILE
    width = nblk * LANES
    nw = dm // width
    seg_blocks = w["lora_seg"] // LANES
    fwd = lambda n: n
    bwd = lambda n: nt - 1 - n

    def tiles(order, d):
        rkv_specs = [pl.BlockSpec((1, TIME_TILE, width), lambda b_, h, n, s=s: (b_, order(n), s * nw + h))
                     for s in range(3)]
        lora_specs = [pl.BlockSpec((1, TIME_TILE, LANES), lambda b_, h, n, s=s: (b_, order(n), s * seg_blocks + d))
                      for s in range(2)]
        return rkv_specs + lora_specs

    def out_tile(order):
        return pl.BlockSpec((1, TIME_TILE, width), lambda b_, h, n: (b_, order(n), h))

    wblk = pl.BlockSpec((LANES, width), lambda b_, h, n: (0, h))
    row = pl.BlockSpec((1, width), lambda b_, h, n: (0, h))
    shp = jax.ShapeDtypeStruct((b, t, dm), F32)
    heads = width // RWKV_N
    dir_weights = [z for d in range(2) for z in (w["w2"][d], w["a2"][d], w["w0"][d:d + 1], w["a0"][d:d + 1])]
    return pl.pallas_call(
        functools.partial(_rwkv_chunk_kernel, nblk=nblk),
        grid=(b, nw, nt),
        in_specs=tiles(fwd, 0) + tiles(bwd, 1) + [wblk, wblk, row, row] * 2 + [row] * 3,
        out_specs=[out_tile(fwd), out_tile(fwd), out_tile(bwd), out_tile(bwd)],
        out_shape=[shp] * 4,
        scratch_shapes=[pltpu.VMEM((heads, RWKV_N, RWKV_N), F32), pltpu.VMEM((heads, RWKV_N, RWKV_N), F32)],
        compiler_params=_cparams(("parallel", "parallel", "arbitrary")),
        name="rwkv_chunk",
    )(rkv, rkv, rkv, lora, lora, rkv, rkv, rkv, lora, lora, *dir_weights, w["k_k"], w["k_a"], w["r_k"])


def _rwkv_out_prologue(y_f, y_b, bn_f, bn_b, g, ln_w, ln_b):
    parts = []
    for hh in range(y_f.shape[1] // LANES):
        sl = slice(hh * LANES, (hh + 1) * LANES)
        wkv = y_f[:, sl] + y_b[:, sl]
        lo = lax.broadcasted_iota(jnp.int32, wkv.shape, 1) < RWKV_N
        mu = _half_sum(wkv, lo) * (1.0 / RWKV_N)
        wc = wkv - mu
        var = _half_sum(wc * wc, lo) * (1.0 / RWKV_N)
        o = wc * lax.rsqrt(var + RWKV_GN_EPS) * ln_w[:, sl] + ln_b[:, sl] + (bn_f[:, sl] + bn_b[:, sl])
        parts.append((o * g[:, sl]).astype(BF16))
    return jnp.concatenate(parts, axis=1) if len(parts) > 1 else parts[0]


def _rwkv_mixer_ln(x, w, alpha, ln_g, ln_b):
    b, t, dm = x.shape
    m = b * t
    x2 = x.reshape(m, dm)
    xx = _token_shift(x).reshape(m, dm)
    seg = w["lora_seg"]
    tn = min(1024, dm)
    rkv, = _fused_matmul([x2, xx], w["w_rkv"], rowk_ins=[w["mix_rkv"]], rowk_seg_tiles=dm // tn,
                         prologue=_mix_prologue, tm=1024, tn=tn, tk=dm, name="rwkv_rkv")

    def lora_epilogue(acc):
        j = pl.program_id(1)
        return (jnp.where(j == 0, jnp.tanh(acc), jnp.where(j == 2, jax.nn.sigmoid(acc), acc)),)

    lora, = _fused_matmul([x2, xx], w["w_lora"], rowk_ins=[w["mix_lora"]], rowk_seg_tiles=1,
                          prologue=_mix_prologue, epilogue=lora_epilogue, out_dtypes=(BF16,),
                          tm=1024, tn=seg, tk=dm, name="rwkv_lora")
    g, = _fused_matmul([lora], w["g2"], a_col0=[2 * seg], tm=1024, tn=1024, tk=seg, name="rwkv_g2")

    y_f, bn_f, y_b, bn_b = _rwkv_chunk(rkv.reshape(b, t, 3 * dm), lora.reshape(b, t, 3 * seg), w, dm)
    out, = _fused_matmul(
        [z.reshape(m, dm) for z in (y_f, y_b, bn_f, bn_b)] + [g], w["w_o"],
        rowk_ins=[w["ln_w"], w["ln_b"]], e_ins=[x2], rown_ins=[ln_g, ln_b],
        prologue=_rwkv_out_prologue, epilogue=_ln_epilogue(alpha), tm=512, tn=dm, tk=512, name="rwkv_out")
    return out


def _relu2_epilogue(acc):
    return (jnp.square(jnp.maximum(acc, 0.0)),)


def _mlp_ln(x2, w_up, w_down, alpha, ln_g, ln_b):
    dm = x2.shape[1]
    h, = _fused_matmul([x2], w_up, epilogue=_relu2_epilogue, out_dtypes=(BF16,), tm=1024, tn=1024, tk=dm,
                       name="mlp_up")
    out, = _fused_matmul([h], w_down, e_ins=[x2], rown_ins=[ln_g, ln_b], epilogue=_ln_epilogue(alpha),
                         tm=512, tn=dm, tk=2048, name="mlp_down")
    return out


def _ple_epilogue(acc, x, pp):
    return (x + jax.nn.sigmoid(acc) * pp,)


def _ple(x2, p2, w_proj, w_gate):
    dm = x2.shape[1]
    pp, = _fused_matmul([p2], w_proj, tm=1024, tn=1024, tk=p2.shape[1], name="ple_proj")
    out, = _fused_matmul([x2], w_gate, e_ins=[x2, pp], epilogue=_ple_epilogue, tm=512, tn=1024, tk=dm,
                         name="ple_gate")
    return out


def _pad_rows(w, rows):
    return jnp.pad(w, ((0, rows - w.shape[0]), (0, 0)))


def _pad_cols(w, cols):
    return jnp.pad(w, ((0, 0), (0, cols - w.shape[1])))


def _gdn_weights(w_in, conv, a_log, dt_bias, norm, w_out):
    hv, dv = a_log.shape[-1], norm.shape[-1]
    vd = hv * dv
    kd = (conv.shape[-1] - vd) // 2
    dk = dv
    assert w_in.shape[1] == 2 * kd + 2 * vd + 4 * hv and w_out.shape[0] == vd
    nz = 2 * kd + 2 * vd
    return dict(kd=kd, vd=vd, hk=kd // dk, hv=hv, dk=dk, dv=dv,
                w_qkvz=w_in[:, :nz].astype(BF16), w_gates=w_in[:, nz:].astype(BF16), conv=conv.astype(F32),
                a_log=a_log, dt_bias=dt_bias, norm_row=jnp.tile(norm.astype(F32), hv).reshape(1, vd),
                w_out=w_out.astype(BF16))


def _rwkv_weights(mix, w_rkv, w0, w1, w2, a0, a1, a2, g1, g2, k_k, k_a, r_k, ln_w, ln_b, w_o):
    dm = w_o.shape[0]
    assert w1.shape[-1] <= LANES and a1.shape[-1] <= LANES and r_k.size == dm and r_k.shape[-1] == RWKV_N
    seg = max(2 * LANES, -(-g1.shape[1] // LANES) * LANES)
    cat = lambda ws: _pad_cols(jnp.concatenate([_pad_cols(ws[d], LANES) for d in range(2)], axis=1), seg)
    row = lambda z: z.reshape(1, dm).astype(F32)
    mix = mix.astype(F32)
    return dict(lora_seg=seg,
                mix_rkv=jnp.stack([mix[0], mix[2], mix[3]]).reshape(3, 1, dm),
                mix_lora=jnp.stack([mix[1], mix[4], mix[5]]).reshape(3, 1, dm),
                w_rkv=jnp.concatenate([w_rkv[0], w_rkv[1], w_rkv[2]], axis=1).astype(BF16),
                w_lora=jnp.concatenate([cat(w1), cat(a1), _pad_cols(g1, seg)], axis=1).astype(BF16),
                w2=[_pad_rows(w2[d], LANES).astype(BF16) for d in range(2)],
                a2=[_pad_rows(a2[d], LANES).astype(BF16) for d in range(2)],
                g2=_pad_rows(g2, seg).astype(BF16), w0=w0.astype(F32), a0=a0.astype(F32),
                k_k=row(k_k), k_a=row(k_a), r_k=row(r_k), ln_w=row(ln_w), ln_b=row(ln_b), w_o=w_o.astype(BF16))


def _trunk(x, p, layers, alpha):
    b, t, dm = x.shape
    m = b * t
    for i, lyr in enumerate(layers):
        mixer = _gdn_mixer_ln if lyr["kind"] == "gdn" else _rwkv_mixer_ln
        x2 = mixer(x, lyr["mixer"], alpha, lyr["ln_g"][0:1], lyr["ln_b"][0:1])
        x2 = _mlp_ln(x2, lyr["w_up"], lyr["w_down"], alpha, lyr["ln_g"][1:2], lyr["ln_b"][1:2])
        x2 = _ple(x2, p[i].reshape(m, -1), lyr["w_proj"], lyr["w_gate"])
        x = x2.reshape(b, t, dm)
    return x


def kernel(x_prompt, x_sample, p_prompt, p_sample, gdn_w_in, gdn_conv, gdn_a_log, gdn_dt_bias, gdn_norm, gdn_w_out, rwkv_mix, rwkv_w_rkv, rwkv_w0, rwkv_w1, rwkv_w2, rwkv_a0, rwkv_a1, rwkv_a2, rwkv_g1, rwkv_g2, rwkv_k_k, rwkv_k_a, rwkv_r_k, rwkv_ln_w, rwkv_ln_b, rwkv_w_o, ln_g, ln_b, mlp_w_up, mlp_w_down, ple_w_proj, ple_w_gate):
    depth = ln_g.shape[0]
    alpha = (2 * depth) ** 0.25
    layers = []
    for i in range(depth):
        j = i // 2
        if i % 2 == 0:
            kind = "gdn"
            mixer = _gdn_weights(gdn_w_in[j], gdn_conv[j], gdn_a_log[j], gdn_dt_bias[j], gdn_norm[j], gdn_w_out[j])
        else:
            kind = "rwkv"
            mixer = _rwkv_weights(rwkv_mix[j], rwkv_w_rkv[j], rwkv_w0[j], rwkv_w1[j], rwkv_w2[j], rwkv_a0[j],
                                  rwkv_a1[j], rwkv_a2[j], rwkv_g1[j], rwkv_g2[j], rwkv_k_k[j], rwkv_k_a[j],
                                  rwkv_r_k[j], rwkv_ln_w[j], rwkv_ln_b[j], rwkv_w_o[j])
        layers.append(dict(kind=kind, mixer=mixer, ln_g=ln_g[i].astype(F32), ln_b=ln_b[i].astype(F32),
                           w_up=mlp_w_up[i].astype(BF16), w_down=mlp_w_down[i].astype(BF16),
                           w_proj=ple_w_proj[i].astype(BF16), w_gate=ple_w_gate[i].astype(BF16)))
    y_prompt = _trunk(x_prompt, p_prompt, layers, alpha)
    y_sample = _trunk(x_sample, p_sample, layers, alpha)
    return (y_prompt, y_sample)
```

```python
import functools
import math

import jax
import jax.numpy as jnp
from jax import lax
from jax.experimental import pallas as pl
from jax.experimental.pallas import tpu as pltpu

F32 = jnp.float32
BF16 = jnp.bfloat16
HIGHEST = lax.Precision.HIGHEST

LANES = 128
HALO = 8
TIME_TILE = 128
GDN_CHUNK = TIME_TILE
RWKV_CHUNK = 64
GDN_PAIRS = 4
RWKV_BLOCKS = 4
INV_BASE = 8
VMEM_LIMIT = 56 * 1024 * 1024

LN_EPS = 1e-5
RWKV_GN_EPS = 64e-5
RWKV_N = 64
GDN_CONV = 5


def _cparams(sem):
    return pltpu.CompilerParams(dimension_semantics=sem, vmem_limit_bytes=VMEM_LIMIT)


def _mm_kernel(*refs, na, nrk, ne, nrn, nout, nk, prologue, epilogue):
    a_refs = refs[:na]
    rk_refs = refs[na:na + nrk]
    w_ref = refs[na + nrk]
    e_refs = refs[na + nrk + 1:na + nrk + 1 + ne]
    rn_refs = refs[na + nrk + 1 + ne:na + nrk + 1 + ne + nrn]
    out_refs = refs[na + nrk + 1 + ne + nrn:na + nrk + 1 + ne + nrn + nout]

    a = prologue(*[r[...] for r in a_refs], *[r[...] for r in rk_refs])
    part = jnp.dot(a, w_ref[...], preferred_element_type=F32)

    def finish(acc):
        outs = epilogue(acc, *[r[...] for r in e_refs], *[r[...] for r in rn_refs])
        for o_ref, o in zip(out_refs, outs):
            o_ref[...] = o.astype(o_ref.dtype)

    if nk == 1:
        finish(part)
        return

    acc_ref = refs[-1]
    k = pl.program_id(2)

    @pl.when(k == 0)
    def _():
        acc_ref[...] = part

    @pl.when(k > 0)
    def _():
        acc_ref[...] += part

    @pl.when(k == nk - 1)
    def _():
        finish(acc_ref[...])


def _cast_bf16(a):
    return a.astype(BF16)


def _identity_epilogue(acc):
    return (acc,)


def _fused_matmul(a_ins, w, *, a_col0=None, rowk_ins=(), rowk_seg_tiles=None, e_ins=(), rown_ins=(),
                  extra_ins=(), prologue=_cast_bf16, epilogue=_identity_epilogue, out_dtypes=(F32,),
                  tm=512, tn=512, tk=512, name="mm"):
    m = a_ins[0].shape[0]
    kdim, n = w.shape
    tm, tn, tk = min(tm, m), min(tn, n), min(tk, kdim)
    assert m % tm == 0 and n % tn == 0 and kdim % tk == 0, (m, n, kdim, tm, tn, tk)
    nk = kdim // tk
    a_col0 = [0] * len(a_ins) if a_col0 is None else a_col0
    assert all(c0 % tk == 0 for c0 in a_col0)

    def a_spec(c0):
        return pl.BlockSpec((tm, tk), lambda i, j, k: (i, c0 // tk + k))

    if rowk_seg_tiles is None:
        rowk_spec = pl.BlockSpec((1, tk), lambda i, j, k: (0, k))
    else:
        rowk_spec = pl.BlockSpec((None, 1, tk), lambda i, j, k: (j // rowk_seg_tiles, 0, k))
    in_specs = ([a_spec(c0) for c0 in a_col0]
                + [rowk_spec for _ in rowk_ins]
                + [pl.BlockSpec((tk, tn), lambda i, j, k: (k, j))]
                + [pl.BlockSpec((tm, tn), lambda i, j, k: (i, j)) for _ in e_ins]
                + [pl.BlockSpec((1, tn), lambda i, j, k: (0, j)) for _ in rown_ins]
                + [pl.BlockSpec(shape(tm, tn), imap) for _, shape, imap in extra_ins])
    out_specs = [pl.BlockSpec((tm, tn), lambda i, j, k: (i, j)) for _ in out_dtypes]
    out_shape = [jax.ShapeDtypeStruct((m, n), dt) for dt in out_dtypes]
    kern = functools.partial(_mm_kernel, na=len(a_ins), nrk=len(rowk_ins), ne=len(e_ins), nrn=len(rown_ins) + len(extra_ins),
                             nout=len(out_dtypes), nk=nk, prologue=prologue, epilogue=epilogue)
    outs = pl.pallas_call(
        kern,
        grid=(m // tm, n // tn, nk),
        in_specs=in_specs,
        out_specs=out_specs,
        out_shape=out_shape,
        scratch_shapes=[pltpu.VMEM((tm, tn), F32)] if nk > 1 else [],
        compiler_params=_cparams(("parallel", "parallel", "arbitrary")),
        name=name,
    )(*a_ins, *rowk_ins, w, *e_ins, *rown_ins, *[x for x, _, _ in extra_ins])
    return outs


def _ln_epilogue(alpha):
    def epilogue(acc, xres, g, b):
        y = alpha * xres + acc
        mu = jnp.mean(y, axis=-1, keepdims=True)
        yc = y - mu
        var = jnp.mean(yc * yc, axis=-1, keepdims=True)
        return (yc * lax.rsqrt(var + LN_EPS) * g + b,)
    return epilogue


def _halo_specs(tt, tc, t_total):
    r = tt // HALO
    last = t_total // HALO - 1
    main = pl.BlockSpec((1, tt, tc), lambda b, i, c: (b, i, c))
    prev = pl.BlockSpec((1, HALO, tc), lambda b, i, c: (b, jnp.maximum(i * r - 1, 0), c))
    nxt = pl.BlockSpec((1, HALO, tc), lambda b, i, c: (b, jnp.minimum((i + 1) * r, last), c))
    return main, prev, nxt


def _fill_halo(scr, x_ref, xp_ref, xn_ref, tt, nt):
    i = pl.program_id(1)
    scr[0:HALO, :] = jnp.where(i > 0, xp_ref[0], 0.0)
    scr[HALO:HALO + tt, :] = x_ref[0]
    scr[HALO + tt:2 * HALO + tt, :] = jnp.where(i < nt - 1, xn_ref[0], 0.0)


def _gdn_conv_kernel(x_ref, xp_ref, xn_ref, w_ref, o_ref, scr, *, tt, tc, nt, nq_tiles, nqk_tiles, q_scale):
    _fill_halo(scr, x_ref, xp_ref, xn_ref, tt, nt)
    pad = (GDN_CONV - 1) // 2
    acc = None
    for j in range(GDN_CONV):
        term = w_ref[j:j + 1, :] * scr[pl.ds(HALO - pad + j, tt), :]
        acc = term if acc is None else acc + term
    y = acc * jax.nn.sigmoid(acc)
    c = pl.program_id(2)
    is_qk = c < nqk_tiles
    scale = jnp.where(c < nq_tiles, q_scale, 1.0).astype(F32)
    for hh in range(tc // LANES):
        blk = y[:, hh * LANES:(hh + 1) * LANES]
        ss = jnp.sum(blk * blk, axis=-1, keepdims=True)
        nrm = blk * lax.rsqrt(ss + 1e-6) * scale
        o_ref[0, :, hh * LANES:(hh + 1) * LANES] = jnp.where(is_qk, nrm, blk)


def _gdn_conv(qkv, conv_w, kd, vd, dk, tt=512, tc=1024):
    b, t, _ = qkv.shape
    ch = 2 * kd + vd
    tt = min(tt, t)
    assert t % tt == 0 and ch % tc == 0 and kd % tc == 0 and dk == LANES
    main, prev, nxt = _halo_specs(tt, tc, t)
    kern = functools.partial(_gdn_conv_kernel, tt=tt, tc=tc, nt=t // tt, nq_tiles=kd // tc,
                             nqk_tiles=2 * kd // tc, q_scale=dk ** -0.5)
    return pl.pallas_call(
        kern,
        grid=(b, t // tt, ch // tc),
        in_specs=[main, prev, nxt, pl.BlockSpec((GDN_CONV, tc), lambda b_, i, c: (0, c))],
        out_specs=pl.BlockSpec((1, tt, tc), lambda b_, i, c: (b_, i, c)),
        out_shape=jax.ShapeDtypeStruct((b, t, ch), F32),
        scratch_shapes=[pltpu.VMEM((tt + 2 * HALO, tc), F32)],
        compiler_params=_cparams(("parallel", "parallel", "parallel")),
        name="gdn_conv",
    )(qkv, qkv, qkv, conv_w)


def _chunk_tri_masks(rows, chunk):
    ri = lax.broadcasted_iota(jnp.int32, (rows, rows), 0)
    ci = lax.broadcasted_iota(jnp.int32, (rows, rows), 1)
    same = (ri // chunk) == (ci // chunk)
    return same & (ci <= ri), same & (ci >= ri)


def _softplus(x):
    return jnp.maximum(x, 0.0) + jnp.log1p(jnp.exp(-jnp.abs(x)))


def _gdn_gates_kernel(g_ref, alog_ref, dtb_ref, o_ref, ot_ref, *, hv):
    x = g_ref[0]
    g = -jnp.exp(alog_ref[...]) * _softplus(x + dtb_ref[...])
    lower, upper = _chunk_tri_masks(TIME_TILE, GDN_CHUNK)
    gc_f = jnp.dot(lower.astype(F32), g, precision=HIGHEST, preferred_element_type=F32)
    gc_b = jnp.dot(upper.astype(F32), g, precision=HIGHEST, preferred_element_type=F32)
    lane = lax.broadcasted_iota(jnp.int32, x.shape, 1)
    out = jnp.where(lane < hv, gc_f, jnp.where(lane < 2 * hv, gc_b, jax.nn.sigmoid(x)))
    o_ref[0] = out
    ot_ref[0] = out.T


def _gdn_gates(gates, a_log, dt_bias, hv):
    b, t, ch = gates.shape
    assert ch == LANES == 4 * hv and t % TIME_TILE == 0
    pad = jnp.zeros((1, 2 * hv), F32)
    alog = jnp.concatenate([a_log.reshape(1, 2 * hv).astype(F32), pad], axis=1)
    dtb = jnp.concatenate([dt_bias.reshape(1, 2 * hv).astype(F32), pad], axis=1)
    row = pl.BlockSpec((1, LANES), lambda b_, i: (0, 0))
    return pl.pallas_call(
        functools.partial(_gdn_gates_kernel, hv=hv),
        grid=(b, t // TIME_TILE),
        in_specs=[pl.BlockSpec((1, TIME_TILE, LANES), lambda b_, i: (b_, i, 0)), row, row],
        out_specs=[pl.BlockSpec((1, TIME_TILE, LANES), lambda b_, i: (b_, i, 0)),
                   pl.BlockSpec((1, LANES, TIME_TILE), lambda b_, i: (b_, 0, i))],
        out_shape=[jax.ShapeDtypeStruct((b, t, LANES), F32), jax.ShapeDtypeStruct((b, LANES, t), F32)],
        compiler_params=_cparams(("parallel", "parallel")),
        name="gdn_gates",
    )(gates, alog, dtb)


def _mm(a, b):
    return jnp.dot(a.astype(BF16), b.astype(BF16), preferred_element_type=F32)


def _mm_nt(a, b):
    return lax.dot_general(a.astype(BF16), b.astype(BF16), (((1,), (1,)), ((), ())), preferred_element_type=F32)


def _mm_tn(a, b):
    return lax.dot_general(a.astype(BF16), b.astype(BF16), (((0,), (0,)), ((), ())), preferred_element_type=F32)


def _mm_hi(a, b):
    return jnp.dot(a, b, precision=HIGHEST, preferred_element_type=F32)


def _half_rows(x, s, half):
    return jnp.concatenate([x[(2 * b + half) * s:(2 * b + half + 1) * s] for b in range(x.shape[0] // (2 * s))],
                           axis=0)


def _with_half_rows(x, new, s, half):
    parts = []
    for b in range(x.shape[0] // (2 * s)):
        keep = x[(2 * b + 1 - half) * s:(2 * b + 2 - half) * s]
        repl = new[b * s:(b + 1) * s]
        parts += [keep, repl] if half else [repl, keep]
    return jnp.concatenate(parts, axis=0)


def _tri_inverses(ms, uppers, size, top):
    ri = lax.broadcasted_iota(jnp.int32, (size, size), 0)
    ci = lax.broadcasted_iota(jnp.int32, (size, size), 1)
    eye = (ri == ci).astype(F32)
    zeros = jnp.zeros((size, size), F32)
    base_bits = int(math.log2(INV_BASE))
    base = (ri >> base_bits) == (ci >> base_bits)
    xps = [jnp.where(base, -m, 0.0) for m in ms]
    ps = [eye + x for x in xps]
    xps = [_mm(xp, xp) for xp in xps]
    for it in range(base_bits - 1):
        if it < base_bits - 2:
            both = [_mm(jnp.concatenate([p, xp], axis=0), xp) for p, xp in zip(ps, xps)]
            ps = [p + b[:size] for p, b in zip(ps, both)]
            xps = [b[size:] for b in both]
        else:
            ps = [p + _mm(p, xp) for p, xp in zip(ps, xps)]
    halves = [0 if up else 1 for up in uppers]
    bits = base_bits
    while (1 << bits) < top:
        s = 1 << bits
        pair = ((ri >> (bits + 1)) == (ci >> (bits + 1))) & ((ri >> bits) != (ci >> bits))
        qs = [_mm(_half_rows(jnp.where(pair, m, 0.0), s, h), p) for m, p, h in zip(ms, ps, halves)]
        rs = [_mm(_half_rows(p, s, h), _with_half_rows(zeros, q, s, h)) for p, q, h in zip(ps, qs, halves)]
        ps = [_with_half_rows(p, _half_rows(p, s, h) - r, s, h) for p, r, h in zip(ps, rs, halves)]
        bits += 1
    return ps


def _pick_lane(tile, lane_idx):
    lane = lax.broadcasted_iota(jnp.int32, tile.shape, 1)
    return jnp.sum(jnp.where(lane == lane_idx, tile, 0.0), axis=1, keepdims=True)


def _gdn_chunk_kernel(qf_ref, kf_ref, vf_ref, gf_ref, gtf_ref, qb_ref, kb_ref, vb_ref, gb_ref, gtb_ref,
                      of_ref, ob_ref, sf_ref, sb_ref, *, rep, hv, npairs):
    n = pl.program_id(2)
    hg = pl.program_id(1)

    @pl.when(n == 0)
    def _():
        sf_ref[...] = jnp.zeros_like(sf_ref)
        sb_ref[...] = jnp.zeros_like(sb_ref)

    c = GDN_CHUNK
    ri = lax.broadcasted_iota(jnp.int32, (c, c), 0)
    ci = lax.broadcasted_iota(jnp.int32, (c, c), 1)
    dirs = ((False, qf_ref, kf_ref, vf_ref, gf_ref, gtf_ref, of_ref, sf_ref),
            (True, qb_ref, kb_ref, vb_ref, gb_ref, gtb_ref, ob_ref, sb_ref))

    pairs = []
    for rev, q_ref, k_ref, *_ in dirs:
        for i in range(npairs):
            pairs.append((q_ref[0, :, i * LANES:(i + 1) * LANES], k_ref[0, :, i * LANES:(i + 1) * LANES]))
    qks = [_mm_nt(q, k) for q, k in pairs]
    kks = [_mm_nt(k, k) for _, k in pairs]

    probs = []
    for d, (rev, q_ref, k_ref, v_ref, g_ref, gt_ref, o_ref, s_ref) in enumerate(dirs):
        incl = (ci >= ri) if rev else (ci <= ri)
        strict = (ci > ri) if rev else (ci < ri)
        g_tile = g_ref[0]
        end = 0 if rev else c - 1
        for i in range(npairs):
            q, k = pairs[d * npairs + i]
            for r in range(rep):
                slot = i * rep + r
                h = (hg * npairs + i) * rep + r
                gcol = _pick_lane(g_tile, d * hv + h)
                bcol = _pick_lane(g_tile, 2 * hv + d * hv + h)
                grow = gt_ref[0, pl.ds(d * hv + h, 1), :]
                decay = jnp.where(incl, jnp.exp(jnp.where(incl, gcol - grow, 0.0)), 0.0)
                g_last = gcol[end:end + 1]
                egc = jnp.exp(gcol)
                v = v_ref[0, :, slot * LANES:(slot + 1) * LANES]
                probs.append(dict(
                    q=q, k=k, egc=egc, g_last=g_last, slot=slot, o_ref=o_ref, s_ref=s_ref, rev=rev,
                    a_kk=jnp.where(strict, kks[d * npairs + i] * bcol * decay, 0.0),
                    a_qk=qks[d * npairs + i] * decay,
                    rhs=jnp.concatenate([v * bcol, k * (bcol * egc)], axis=1),
                    k_dec=k * jnp.exp(g_last - gcol)))

    t_invs = _tri_inverses([p["a_kk"] for p in probs], [p["rev"] for p in probs], c, c)
    uws = [_mm(t, p["rhs"]) for t, p in zip(t_invs, probs)]
    aunms = [_mm(jnp.concatenate([p["a_qk"], p["k_dec"].T], axis=0), uw) for p, uw in zip(probs, uws)]
    aus = [x[:c] for x in aunms]
    nms = [x[c:] for x in aunms]

    states = [p["s_ref"][p["slot"]] for p in probs]
    lhss = [jnp.concatenate([p["q"] * p["egc"] - au[:, LANES:], nm[:, LANES:]], axis=0)
            for p, au, nm in zip(probs, aus, nms)]
    ress = [_mm(lhs, s) for lhs, s in zip(lhss, states)]
    for p, s, au, nm, res in zip(probs, states, aus, nms, ress):
        slot = p["slot"]
        p["o_ref"][0, :, slot * LANES:(slot + 1) * LANES] = res[:c] + au[:, :LANES]
        p["s_ref"][slot] = s * jnp.exp(p["g_last"]) - res[c:] + nm[:, :LANES]


def _gdn_chunk(qkv, gcb, gcbt, hk, hv, dk, dv):
    b, t, _ = qkv.shape
    rep = hv // hk
    npairs = math.gcd(GDN_PAIRS, hk)
    assert dk == LANES and dv == LANES and t % TIME_TILE == 0 and GDN_CHUNK == TIME_TILE
    assert (2 * hk) % (npairs * rep) == 0
    nt = t // TIME_TILE
    qw, vw = npairs * LANES, npairs * rep * LANES
    kblk0 = hk // npairs
    vblk0 = 2 * hk // (npairs * rep)

    def specs(tile):
        return [pl.BlockSpec((1, TIME_TILE, qw), lambda b_, h, n: (b_, tile(n), h)),
                pl.BlockSpec((1, TIME_TILE, qw), lambda b_, h, n: (b_, tile(n), kblk0 + h)),
                pl.BlockSpec((1, TIME_TILE, vw), lambda b_, h, n: (b_, tile(n), vblk0 + h)),
                pl.BlockSpec((1, TIME_TILE, LANES), lambda b_, h, n: (b_, tile(n), 0)),
                pl.BlockSpec((1, LANES, TIME_TILE), lambda b_, h, n: (b_, 0, tile(n)))]

    fwd = lambda n: n
    bwd = lambda n: nt - 1 - n
    out_f = pl.BlockSpec((1, TIME_TILE, vw), lambda b_, h, n: (b_, n, h))
    out_b = pl.BlockSpec((1, TIME_TILE, vw), lambda b_, h, n: (b_, nt - 1 - n, h))
    shp = jax.ShapeDtypeStruct((b, t, hv * dv), F32)
    return pl.pallas_call(
        functools.partial(_gdn_chunk_kernel, rep=rep, hv=hv, npairs=npairs),
        grid=(b, hk // npairs, nt),
        in_specs=specs(fwd) + specs(bwd),
        out_specs=[out_f, out_b],
        out_shape=[shp, shp],
        scratch_shapes=[pltpu.VMEM((npairs * rep, dk, dv), F32), pltpu.VMEM((npairs * rep, dk, dv), F32)],
        compiler_params=_cparams(("parallel", "parallel", "arbitrary")),
        name="gdn_chunk",
    )(qkv, qkv, qkv, gcb, gcbt, qkv, qkv, qkv, gcb, gcbt)


def _gdn_out_prologue(o_f, o_b, z, nw):
    parts = []
    for hh in range(o_f.shape[1] // LANES):
        sl = slice(hh * LANES, (hh + 1) * LANES)
        o = o_f[:, sl] + o_b[:, sl]
        o = o * lax.rsqrt(jnp.mean(o * o, axis=-1, keepdims=True) + 1e-6) * nw[:, sl]
        zz = z[:, sl]
        parts.append((o * (zz * jax.nn.sigmoid(zz))).astype(BF16))
    return jnp.concatenate(parts, axis=1) if len(parts) > 1 else parts[0]


def _gdn_mixer_ln(x, w, alpha, ln_g, ln_b):
    b, t, dm = x.shape
    m = b * t
    x2 = x.reshape(m, dm)
    kd, vd, hk, hv, dk, dv = w["kd"], w["vd"], w["hk"], w["hv"], w["dk"], w["dv"]
    qkvz, = _fused_matmul([x2], w["w_qkvz"], tm=1024, tn=1024, tk=dm, name="gdn_in")
    gates, = _fused_matmul([x2], w["w_gates"], tm=1024, tn=LANES, tk=dm, name="gdn_in_gates")
    qkv = _gdn_conv(qkvz.reshape(b, t, -1), w["conv"], kd, vd, dk)
    gcb, gcbt = _gdn_gates(gates.reshape(b, t, -1), w["a_log"], w["dt_bias"], hv)
    o_f, o_b = _gdn_chunk(qkv, gcb, gcbt, hk, hv, dk, dv)
    out, = _fused_matmul(
        [o_f.reshape(m, vd), o_b.reshape(m, vd), qkvz], w["w_out"], a_col0=[0, 0, 2 * kd + vd],
        rowk_ins=[w["norm_row"]], e_ins=[x2], rown_ins=[ln_g, ln_b],
        prologue=_gdn_out_prologue, epilogue=_ln_epilogue(alpha), tm=512, tn=dm, tk=1024, name="gdn_out")
    return out


def _shift_kernel(x_ref, xp_ref, xn_ref, o_ref, scr, *, tt, nt):
    _fill_halo(scr, x_ref, xp_ref, xn_ref, tt, nt)
    o_ref[0] = 0.5 * (scr[pl.ds(HALO - 1, tt), :] + scr[pl.ds(HALO + 1, tt), :]) - x_ref[0]


def _token_shift(x, tt=256, tc=512):
    b, t, ch = x.shape
    tt, tc = min(tt, t), min(tc, ch)
    assert t % tt == 0 and ch % tc == 0
    main, prev, nxt = _halo_specs(tt, tc, t)
    return pl.pallas_call(
        functools.partial(_shift_kernel, tt=tt, nt=t // tt),
        grid=(b, t // tt, ch // tc),
        in_specs=[main, prev, nxt],
        out_specs=pl.BlockSpec((1, tt, tc), lambda b_, i, c: (b_, i, c)),
        out_shape=jax.ShapeDtypeStruct((b, t, ch), F32),
        scratch_shapes=[pltpu.VMEM((tt + 2 * HALO, tc), F32)],
        compiler_params=_cparams(("parallel", "parallel", "parallel")),
        name="rwkv_shift",
    )(x, x, x)


def _mix_prologue(x, xx, mix):
    return (x + xx * mix).astype(BF16)


def _head_sum(x):
    rl = lax.broadcasted_iota(jnp.int32, (LANES, LANES), 0)
    cl = lax.broadcasted_iota(jnp.int32, (LANES, LANES), 1)
    ones = jnp.where((rl // RWKV_N) == (cl // RWKV_N), 1.0, 0.0).astype(BF16)
    hi = x.astype(BF16)
    lo = (x - hi.astype(F32)).astype(BF16)
    return (jnp.dot(hi, ones, preferred_element_type=F32) + jnp.dot(lo, ones, preferred_element_type=F32))


def _rwkv_chunk_kernel(rf_ref, kf_ref, vf_ref, lwf_ref, laf_ref, rb_ref, kb_ref, vb_ref, lwb_ref, lab_ref,
                       w2f_ref, a2f_ref, w0f_ref, a0f_ref, w2b_ref, a2b_ref, w0b_ref, a0b_ref,
                       kk_ref, ka_ref, rk_ref, yf_ref, bf_ref, yb_ref, bb_ref, gf_ref, gb_ref, *, nblk):
    n = pl.program_id(2)

    @pl.when(n == 0)
    def _():
        gf_ref[...] = jnp.zeros_like(gf_ref)
        gb_ref[...] = jnp.zeros_like(gb_ref)

    c, nn, tt = RWKV_CHUNK, RWKV_N, TIME_TILE
    nch, nh = tt // c, LANES // nn
    row = lax.broadcasted_iota(jnp.int32, (tt, LANES), 0)
    ri = lax.broadcasted_iota(jnp.int32, (tt, tt), 0)
    ci = lax.broadcasted_iota(jnp.int32, (tt, tt), 1)
    same = (ri // c) == (ci // c)
    dirs = ((False, rf_ref, kf_ref, vf_ref, lwf_ref, laf_ref, w2f_ref, a2f_ref, w0f_ref, a0f_ref,
             yf_ref, bf_ref, gf_ref),
            (True, rb_ref, kb_ref, vb_ref, lwb_ref, lab_ref, w2b_ref, a2b_ref, w0b_ref, a0b_ref,
             yb_ref, bb_ref, gb_ref))

    probs, blocks = [], []
    for rev, r_ref, k_ref, v_ref, lw_ref, la_ref, w2_ref, a2_ref, w0_ref, a0_ref, y_ref, bonus_ref, g_ref in dirs:
        incl = same & ((ci >= ri) if rev else (ci <= ri))
        strict = same & ((ci > ri) if rev else (ci < ri))
        ends = [cc * c if rev else (cc + 1) * c - 1 for cc in range(nch)]
        lw_all = -jnp.exp(-_softplus(-(w0_ref[...] + jnp.dot(lw_ref[0], w2_ref[...], preferred_element_type=F32)))
                          - 0.5)
        a_all = jax.nn.sigmoid(a0_ref[...] + jnp.dot(la_ref[0], a2_ref[...], preferred_element_type=F32))
        for blk in range(nblk):
            ls = slice(blk * LANES, (blk + 1) * LANES)
            r, k, v, lw, a = r_ref[0, :, ls], k_ref[0, :, ls], v_ref[0, :, ls], lw_all[:, ls], a_all[:, ls]
            kkk = k * kk_ref[:, ls]
            kk = kkk * lax.rsqrt(_head_sum(kkk * kkk) + 1e-6)
            kd = k * (1.0 + (a - 1.0) * ka_ref[:, ls])
            av = -kk
            bv = kk * a
            bonus_ref[0, :, ls] = _head_sum(r * kd * rk_ref[:, ls]) * v
            gc = _mm_hi(incl.astype(F32), lw)
            gends = [gc[e:e + 1] for e in ends]
            gend_tile = gends[0]
            for cc in range(1, nch):
                gend_tile = jnp.where(row >= cc * c, gends[cc], gend_tile)
            ee = jnp.exp(gend_tile - gc)
            ei = jnp.exp(-gc)
            at, kt, bt, rt = av * jnp.exp(gc - lw), kd * ei, bv * ei, r * jnp.exp(gc)
            kh, bh = kd * ee, bv * ee
            blocks.append(dict(rev=rev, y_ref=y_ref, g_ref=g_ref, blk=blk, rt=rt, v=v, kh=kh, bh=bh,
                               egend=[jnp.exp(g) for g in gends], first=len(probs)))
            for j in range(nh):
                hs = slice(j * nn, (j + 1) * nn)
                probs.append(dict(rev=rev, incl=incl, strict=strict, at=at[:, hs], kt=kt[:, hs], bt=bt[:, hs],
                                  rt=rt[:, hs], v=v[:, hs]))

    bigs = [_mm_nt(jnp.concatenate([p["at"], p["rt"]], axis=0), jnp.concatenate([p["bt"], p["kt"]], axis=0))
            for p in probs]
    for p, big in zip(probs, bigs):
        p["a_ab"] = jnp.where(p["strict"], big[:tt, :tt], 0.0)
        p["a_ak"] = jnp.where(p["strict"], big[:tt, tt:], 0.0)
        p["m_rb"] = jnp.where(p["incl"], big[tt:, :tt], 0.0)
        p["m_rk"] = jnp.where(p["incl"], big[tt:, tt:], 0.0)
    avyvs = [_mm(jnp.concatenate([p["a_ak"], p["m_rk"]], axis=0), p["v"]) for p in probs]
    t_invs = _tri_inverses([-p["a_ab"] for p in probs], [p["rev"] for p in probs], tt, c)
    tauvs = [_mm(t, jnp.concatenate([p["at"], av[:tt]], axis=1)) for t, p, av in zip(t_invs, probs, avyvs)]
    mtmus = [_mm(p["m_rb"], tauv) for p, tauv in zip(probs, tauvs)]

    for blk in blocks:
        hd = range(blk["first"], blk["first"] + nh)
        blk["ta"] = jnp.concatenate([tauvs[i][:, :nn] for i in hd], axis=1)
        blk["uv"] = jnp.concatenate([tauvs[i][:, nn:] for i in hd], axis=1)
        blk["rq"] = blk["rt"] + jnp.concatenate([mtmus[i][:, :nn] for i in hd], axis=1)
        blk["yc"] = jnp.concatenate([avyvs[i][tt:] + mtmus[i][:, nn:] for i in hd], axis=1)

    def rows(cc):
        return slice(cc * c, (cc + 1) * c)

    rl = lax.broadcasted_iota(jnp.int32, (LANES, LANES), 0)
    cl = lax.broadcasted_iota(jnp.int32, (LANES, LANES), 1)
    same_head = (rl // nn) == (cl // nn)
    mps = [[_mm_tn(blk["ta"][rows(cc)], blk["bh"][rows(cc)]) for cc in range(nch)] for blk in blocks]
    n2s = [[_mm_tn(jnp.concatenate([blk["v"][rows(cc)], blk["uv"][rows(cc)]], axis=0),
                   jnp.concatenate([blk["kh"][rows(cc)], blk["bh"][rows(cc)]], axis=0)) for cc in range(nch)]
           for blk in blocks]
    states = [blk["g_ref"][blk["blk"]] for blk in blocks]
    ys = [[None] * nch for _ in blocks]
    for step in range(nch):
        ccs = [nch - 1 - step if blk["rev"] else step for blk in blocks]
        yparts = [_mm_nt(blk["rq"][rows(cc)], g) for blk, g, cc in zip(blocks, states, ccs)]
        gms = [_mm(g, jnp.where(same_head, mp[cc], 0.0)) for g, mp, cc in zip(states, mps, ccs)]
        for i, (blk, cc) in enumerate(zip(blocks, ccs)):
            ys[i][cc] = yparts[i] + blk["yc"][rows(cc)]
            states[i] = states[i] * blk["egend"][cc] + gms[i] + jnp.where(same_head, n2s[i][cc], 0.0)
    for blk, g, y in zip(blocks, states, ys):
        blk["g_ref"][blk["blk"]] = g
        blk["y_ref"][0, :, blk["blk"] * LANES:(blk["blk"] + 1) * LANES] = jnp.concatenate(y, axis=0)


def _rwkv_chunk(rkv, lora, w, dm):
    b, t, _ = rkv.shape
    assert t % TIME_TILE == 0 and dm % LANES == 0 and LANES % RWKV_N == 0 and TIME_TILE % RWKV_CHUNK == 0
    nblk = math.gcd(RWKV_BLOCKS, dm // LANES)
    nt = t // TIME_TILE
    width = nblk * LANES
    nw = dm // width
    seg_blocks = w["lora_seg"] // LANES
    fwd = lambda n: n
    bwd = lambda n: nt - 1 - n

    def tiles(order, d):
        rkv_specs = [pl.BlockSpec((1, TIME_TILE, width), lambda b_, h, n, s=s: (b_, order(n), s * nw + h))
                     for s in range(3)]
        lora_specs = [pl.BlockSpec((1, TIME_TILE, LANES), lambda b_, h, n, s=s: (b_, order(n), s * seg_blocks + d))
                      for s in range(2)]
        return rkv_specs + lora_specs

    def out_tile(order):
        return pl.BlockSpec((1, TIME_TILE, width), lambda b_, h, n: (b_, order(n), h))

    wblk = pl.BlockSpec((LANES, width), lambda b_, h, n: (0, h))
    row = pl.BlockSpec((1, width), lambda b_, h, n: (0, h))
    shp = jax.ShapeDtypeStruct((b, t, dm), F32)
    dir_weights = [z for d in range(2) for z in (w["w2"][d], w["a2"][d], w["w0"][d:d + 1], w["a0"][d:d + 1])]
    return pl.pallas_call(
        functools.partial(_rwkv_chunk_kernel, nblk=nblk),
        grid=(b, nw, nt),
        in_specs=tiles(fwd, 0) + tiles(bwd, 1) + [wblk, wblk, row, row] * 2 + [row] * 3,
        out_specs=[out_tile(fwd), out_tile(fwd), out_tile(bwd), out_tile(bwd)],
        out_shape=[shp] * 4,
        scratch_shapes=[pltpu.VMEM((nblk, LANES, LANES), F32), pltpu.VMEM((nblk, LANES, LANES), F32)],
        compiler_params=_cparams(("parallel", "parallel", "arbitrary")),
        name="rwkv_chunk",
    )(rkv, rkv, rkv, lora, lora, rkv, rkv, rkv, lora, lora, *dir_weights, w["k_k"], w["k_a"], w["r_k"])


def _rwkv_out_prologue(y_f, y_b, bn_f, bn_b, g, ln_w, ln_b):
    parts = []
    for hh in range(y_f.shape[1] // LANES):
        sl = slice(hh * LANES, (hh + 1) * LANES)
        wkv = y_f[:, sl] + y_b[:, sl]
        mu = _head_sum(wkv) * (1.0 / RWKV_N)
        wc = wkv - mu
        var = _head_sum(wc * wc) * (1.0 / RWKV_N)
        o = wc * lax.rsqrt(var + RWKV_GN_EPS) * ln_w[:, sl] + ln_b[:, sl] + (bn_f[:, sl] + bn_b[:, sl])
        parts.append((o * g[:, sl]).astype(BF16))
    return jnp.concatenate(parts, axis=1) if len(parts) > 1 else parts[0]


def _rwkv_mixer_ln(x, w, alpha, ln_g, ln_b):
    b, t, dm = x.shape
    m = b * t
    x2 = x.reshape(m, dm)
    xx = _token_shift(x).reshape(m, dm)
    seg = w["lora_seg"]
    tn = min(1024, dm)
    rkv, = _fused_matmul([x2, xx], w["w_rkv"], rowk_ins=[w["mix_rkv"]], rowk_seg_tiles=dm // tn,
                         prologue=_mix_prologue, tm=1024, tn=tn, tk=dm, name="rwkv_rkv")

    def lora_epilogue(acc):
        j = pl.program_id(1)
        return (jnp.where(j == 0, jnp.tanh(acc), jnp.where(j == 2, jax.nn.sigmoid(acc), acc)),)

    lora, = _fused_matmul([x2, xx], w["w_lora"], rowk_ins=[w["mix_lora"]], rowk_seg_tiles=1,
                          prologue=_mix_prologue, epilogue=lora_epilogue, out_dtypes=(BF16,),
                          tm=1024, tn=seg, tk=dm, name="rwkv_lora")
    g, = _fused_matmul([lora], w["g2"], a_col0=[2 * seg], tm=1024, tn=1024, tk=seg, name="rwkv_g2")

    y_f, bn_f, y_b, bn_b = _rwkv_chunk(rkv.reshape(b, t, 3 * dm), lora.reshape(b, t, 3 * seg), w, dm)
    out, = _fused_matmul(
        [z.reshape(m, dm) for z in (y_f, y_b, bn_f, bn_b)] + [g], w["w_o"],
        rowk_ins=[w["ln_w"], w["ln_b"]], e_ins=[x2], rown_ins=[ln_g, ln_b],
        prologue=_rwkv_out_prologue, epilogue=_ln_epilogue(alpha), tm=512, tn=dm, tk=512, name="rwkv_out")
    return out


def _relu2_epilogue(acc):
    return (jnp.square(jnp.maximum(acc, 0.0)),)


def _mlp_ln(x2, w_up, w_down, alpha, ln_g, ln_b):
    dm = x2.shape[1]
    h, = _fused_matmul([x2], w_up, epilogue=_relu2_epilogue, out_dtypes=(BF16,), tm=1024, tn=1024, tk=dm,
                       name="mlp_up")
    out, = _fused_matmul([h], w_down, e_ins=[x2], rown_ins=[ln_g, ln_b], epilogue=_ln_epilogue(alpha),
                         tm=512, tn=dm, tk=2048, name="mlp_down")
    return out


def _ple_epilogue(acc, x, p, w_proj):
    return (x + jax.nn.sigmoid(acc) * jnp.dot(p.astype(BF16), w_proj, preferred_element_type=F32),)


def _ple(x2, p2, w_proj, w_gate):
    dm = x2.shape[1]
    pdim = p2.shape[1]
    extra = [(p2, lambda tm, tn: (tm, pdim), lambda i, j, k: (i, 0)),
             (w_proj, lambda tm, tn: (pdim, tn), lambda i, j, k: (0, j))]
    out, = _fused_matmul([x2], w_gate, e_ins=[x2], extra_ins=extra, epilogue=_ple_epilogue,
                         tm=1024, tn=1024, tk=dm, name="ple_gate")
    return out


def _pad_rows(w, rows):
    return jnp.pad(w, ((0, rows - w.shape[0]), (0, 0)))


def _pad_cols(w, cols):
    return jnp.pad(w, ((0, 0), (0, cols - w.shape[1])))


def _gdn_weights(w_in, conv, a_log, dt_bias, norm, w_out):
    hv, dv = a_log.shape[-1], norm.shape[-1]
    vd = hv * dv
    kd = (conv.shape[-1] - vd) // 2
    dk = dv
    assert w_in.shape[1] == 2 * kd + 2 * vd + 4 * hv and w_out.shape[0] == vd
    nz = 2 * kd + 2 * vd
    return dict(kd=kd, vd=vd, hk=kd // dk, hv=hv, dk=dk, dv=dv,
                w_qkvz=w_in[:, :nz].astype(BF16), w_gates=w_in[:, nz:].astype(BF16), conv=conv.astype(F32),
                a_log=a_log, dt_bias=dt_bias, norm_row=jnp.tile(norm.astype(F32), hv).reshape(1, vd),
                w_out=w_out.astype(BF16))


def _rwkv_weights(mix, w_rkv, w0, w1, w2, a0, a1, a2, g1, g2, k_k, k_a, r_k, ln_w, ln_b, w_o):
    dm = w_o.shape[0]
    assert w1.shape[-1] <= LANES and a1.shape[-1] <= LANES and r_k.size == dm and r_k.shape[-1] == RWKV_N
    seg = max(2 * LANES, -(-g1.shape[1] // LANES) * LANES)
    cat = lambda ws: _pad_cols(jnp.concatenate([_pad_cols(ws[d], LANES) for d in range(2)], axis=1), seg)
    row = lambda z: z.reshape(1, dm).astype(F32)
    mix = mix.astype(F32)
    return dict(lora_seg=seg,
                mix_rkv=jnp.stack([mix[0], mix[2], mix[3]]).reshape(3, 1, dm),
                mix_lora=jnp.stack([mix[1], mix[4], mix[5]]).reshape(3, 1, dm),
                w_rkv=jnp.concatenate([w_rkv[0], w_rkv[1], w_rkv[2]], axis=1).astype(BF16),
                w_lora=jnp.concatenate([cat(w1), cat(a1), _pad_cols(g1, seg)], axis=1).astype(BF16),
                w2=[_pad_rows(w2[d], LANES).astype(BF16) for d in range(2)],
                a2=[_pad_rows(a2[d], LANES).astype(BF16) for d in range(2)],
                g2=_pad_rows(g2, seg).astype(BF16), w0=w0.astype(F32), a0=a0.astype(F32),
                k_k=row(k_k), k_a=row(k_a), r_k=row(r_k), ln_w=row(ln_w), ln_b=row(ln_b), w_o=w_o.astype(BF16))


def _trunk(x, p, layers, alpha):
    b, t, dm = x.shape
    m = b * t
    for i, lyr in enumerate(layers):
        mixer = _gdn_mixer_ln if lyr["kind"] == "gdn" else _rwkv_mixer_ln
        x2 = mixer(x, lyr["mixer"], alpha, lyr["ln_g"][0:1], lyr["ln_b"][0:1])
        x2 = _mlp_ln(x2, lyr["w_up"], lyr["w_down"], alpha, lyr["ln_g"][1:2], lyr["ln_b"][1:2])
        x2 = _ple(x2, p[i].reshape(m, -1), lyr["w_proj"], lyr["w_gate"])
        x = x2.reshape(b, t, dm)
    return x


def kernel(x_prompt, x_sample, p_prompt, p_sample, gdn_w_in, gdn_conv, gdn_a_log, gdn_dt_bias, gdn_norm, gdn_w_out, rwkv_mix, rwkv_w_rkv, rwkv_w0, rwkv_w1, rwkv_w2, rwkv_a0, rwkv_a1, rwkv_a2, rwkv_g1, rwkv_g2, rwkv_k_k, rwkv_k_a, rwkv_r_k, rwkv_ln_w, rwkv_ln_b, rwkv_w_o, ln_g, ln_b, mlp_w_up, mlp_w_down, ple_w_proj, ple_w_gate):
    depth = ln_g.shape[0]
    alpha = (2 * depth) ** 0.25
    layers = []
    for i in range(depth):
        j = i // 2
        if i % 2 == 0:
            kind = "gdn"
            mixer = _gdn_weights(gdn_w_in[j], gdn_conv[j], gdn_a_log[j], gdn_dt_bias[j], gdn_norm[j], gdn_w_out[j])
        else:
            kind = "rwkv"
            mixer = _rwkv_weights(rwkv_mix[j], rwkv_w_rkv[j], rwkv_w0[j], rwkv_w1[j], rwkv_w2[j], rwkv_a0[j],
                                  rwkv_a1[j], rwkv_a2[j], rwkv_g1[j], rwkv_g2[j], rwkv_k_k[j], rwkv_k_a[j],
                                  rwkv_r_k[j], rwkv_ln_w[j], rwkv_ln_b[j], rwkv_w_o[j])
        layers.append(dict(kind=kind, mixer=mixer, ln_g=ln_g[i].astype(F32), ln_b=ln_b[i].astype(F32),
                           w_up=mlp_w_up[i].astype(BF16), w_down=mlp_w_down[i].astype(BF16),
                           w_proj=ple_w_proj[i].astype(BF16), w_gate=ple_w_gate[i].astype(BF16)))
    y_prompt = _trunk(x_prompt, p_prompt, layers, alpha)
    y_sample = _trunk(x_sample, p_sample, layers, alpha)
    return (y_prompt, y_sample)
```

```python
import functools
import math

import jax
import jax.numpy as jnp
from jax import lax
from jax.experimental import pallas as pl
from jax.experimental.pallas import tpu as pltpu

F32 = jnp.float32
BF16 = jnp.bfloat16

LANES = 128
HALO = 8
TIME_TILE = 128
GDN_CHUNK = TIME_TILE
RWKV_CHUNK = 64
GDN_PAIRS = 4
RWKV_BLOCKS = 4
INV_BASE = 8
VMEM_LIMIT = 56 * 1024 * 1024

LN_EPS = 1e-5
RWKV_GN_EPS = 64e-5
RWKV_N = 64
GDN_CONV = 5


def _cparams(sem):
    return pltpu.CompilerParams(dimension_semantics=sem, vmem_limit_bytes=VMEM_LIMIT)


def _mm_kernel(*refs, na, nrk, ne, nrn, nout, nk, prologue, epilogue):
    a_refs = refs[:na]
    rk_refs = refs[na:na + nrk]
    w_ref = refs[na + nrk]
    e_refs = refs[na + nrk + 1:na + nrk + 1 + ne]
    rn_refs = refs[na + nrk + 1 + ne:na + nrk + 1 + ne + nrn]
    out_refs = refs[na + nrk + 1 + ne + nrn:na + nrk + 1 + ne + nrn + nout]

    a = prologue(*[r[...] for r in a_refs], *[r[...] for r in rk_refs])
    part = jnp.dot(a, w_ref[...], preferred_element_type=F32)

    def finish(acc):
        outs = epilogue(acc, *[r[...] for r in e_refs], *[r[...] for r in rn_refs])
        for o_ref, o in zip(out_refs, outs):
            o_ref[...] = o.astype(o_ref.dtype)

    if nk == 1:
        finish(part)
        return

    acc_ref = refs[-1]
    k = pl.program_id(2)

    @pl.when(k == 0)
    def _():
        acc_ref[...] = part

    @pl.when(k > 0)
    def _():
        acc_ref[...] += part

    @pl.when(k == nk - 1)
    def _():
        finish(acc_ref[...])


def _cast_bf16(a):
    return a.astype(BF16)


def _identity_epilogue(acc):
    return (acc,)


def _fused_matmul(a_ins, w, *, a_col0=None, rowk_ins=(), rowk_seg_tiles=None, e_ins=(), rown_ins=(),
                  extra_ins=(), prologue=_cast_bf16, epilogue=_identity_epilogue, out_dtypes=(F32,),
                  tm=512, tn=512, tk=512, name="mm"):
    m = a_ins[0].shape[0]
    kdim, n = w.shape
    tm, tn, tk = min(tm, m), min(tn, n), min(tk, kdim)
    assert m % tm == 0 and n % tn == 0 and kdim % tk == 0, (m, n, kdim, tm, tn, tk)
    nk = kdim // tk
    a_col0 = [0] * len(a_ins) if a_col0 is None else a_col0
    assert all(c0 % tk == 0 for c0 in a_col0)

    def a_spec(c0):
        return pl.BlockSpec((tm, tk), lambda i, j, k: (i, c0 // tk + k))

    if rowk_seg_tiles is None:
        rowk_spec = pl.BlockSpec((1, tk), lambda i, j, k: (0, k))
    else:
        rowk_spec = pl.BlockSpec((None, 1, tk), lambda i, j, k: (j // rowk_seg_tiles, 0, k))
    in_specs = ([a_spec(c0) for c0 in a_col0]
                + [rowk_spec for _ in rowk_ins]
                + [pl.BlockSpec((tk, tn), lambda i, j, k: (k, j))]
                + [pl.BlockSpec((tm, tn), lambda i, j, k: (i, j)) for _ in e_ins]
                + [pl.BlockSpec((1, tn), lambda i, j, k: (0, j)) for _ in rown_ins]
                + [pl.BlockSpec(shape(tm, tn), imap) for _, shape, imap in extra_ins])
    out_specs = [pl.BlockSpec((tm, tn), lambda i, j, k: (i, j)) for _ in out_dtypes]
    out_shape = [jax.ShapeDtypeStruct((m, n), dt) for dt in out_dtypes]
    kern = functools.partial(_mm_kernel, na=len(a_ins), nrk=len(rowk_ins), ne=len(e_ins),
                             nrn=len(rown_ins) + len(extra_ins), nout=len(out_dtypes), nk=nk,
                             prologue=prologue, epilogue=epilogue)
    outs = pl.pallas_call(
        kern,
        grid=(m // tm, n // tn, nk),
        in_specs=in_specs,
        out_specs=out_specs,
        out_shape=out_shape,
        scratch_shapes=[pltpu.VMEM((tm, tn), F32)] if nk > 1 else [],
        compiler_params=_cparams(("parallel", "parallel", "arbitrary")),
        name=name,
    )(*a_ins, *rowk_ins, w, *e_ins, *rown_ins, *[x for x, _, _ in extra_ins])
    return outs


def _ln_epilogue(alpha):
    def epilogue(acc, xres, g, b):
        y = alpha * xres + acc
        mu = jnp.mean(y, axis=-1, keepdims=True)
        yc = y - mu
        var = jnp.mean(yc * yc, axis=-1, keepdims=True)
        return (yc * lax.rsqrt(var + LN_EPS) * g + b,)
    return epilogue


def _halo_specs(tt, tc, t_total):
    r = tt // HALO
    last = t_total // HALO - 1
    main = pl.BlockSpec((1, tt, tc), lambda b, i, c: (b, i, c))
    prev = pl.BlockSpec((1, HALO, tc), lambda b, i, c: (b, jnp.maximum(i * r - 1, 0), c))
    nxt = pl.BlockSpec((1, HALO, tc), lambda b, i, c: (b, jnp.minimum((i + 1) * r, last), c))
    return main, prev, nxt


def _fill_halo(scr, x_ref, xp_ref, xn_ref, tt, nt):
    i = pl.program_id(1)
    scr[0:HALO, :] = jnp.where(i > 0, xp_ref[0], 0.0)
    scr[HALO:HALO + tt, :] = x_ref[0]
    scr[HALO + tt:2 * HALO + tt, :] = jnp.where(i < nt - 1, xn_ref[0], 0.0)


def _gdn_conv_kernel(x_ref, xp_ref, xn_ref, w_ref, o_ref, scr, *, tt, tc, nt, nq_tiles, nqk_tiles, q_scale):
    _fill_halo(scr, x_ref, xp_ref, xn_ref, tt, nt)
    pad = (GDN_CONV - 1) // 2
    acc = None
    for j in range(GDN_CONV):
        term = w_ref[j:j + 1, :] * scr[pl.ds(HALO - pad + j, tt), :]
        acc = term if acc is None else acc + term
    y = acc * jax.nn.sigmoid(acc)
    c = pl.program_id(2)
    is_qk = c < nqk_tiles
    scale = jnp.where(c < nq_tiles, q_scale, 1.0).astype(F32)
    for hh in range(tc // LANES):
        blk = y[:, hh * LANES:(hh + 1) * LANES]
        ss = jnp.sum(blk * blk, axis=-1, keepdims=True)
        nrm = blk * (lax.rsqrt(ss + 1e-6) * scale)
        o_ref[0, :, hh * LANES:(hh + 1) * LANES] = jnp.where(is_qk, nrm, blk)


def _gdn_conv(qkv, conv_w, kd, vd, dk, tt=512, tc=2048):
    b, t, _ = qkv.shape
    ch = 2 * kd + vd
    tt = min(tt, t)
    assert t % tt == 0 and ch % tc == 0 and kd % tc == 0 and dk == LANES
    main, prev, nxt = _halo_specs(tt, tc, t)
    kern = functools.partial(_gdn_conv_kernel, tt=tt, tc=tc, nt=t // tt, nq_tiles=kd // tc,
                             nqk_tiles=2 * kd // tc, q_scale=dk ** -0.5)
    return pl.pallas_call(
        kern,
        grid=(b, t // tt, ch // tc),
        in_specs=[main, prev, nxt, pl.BlockSpec((GDN_CONV, tc), lambda b_, i, c: (0, c))],
        out_specs=pl.BlockSpec((1, tt, tc), lambda b_, i, c: (b_, i, c)),
        out_shape=jax.ShapeDtypeStruct((b, t, ch), F32),
        scratch_shapes=[pltpu.VMEM((tt + 2 * HALO, tc), F32)],
        compiler_params=_cparams(("parallel", "parallel", "parallel")),
        name="gdn_conv",
    )(qkv, qkv, qkv, conv_w)


def _chunk_tri_masks(rows, chunk):
    ri = lax.broadcasted_iota(jnp.int32, (rows, rows), 0)
    ci = lax.broadcasted_iota(jnp.int32, (rows, rows), 1)
    same = (ri // chunk) == (ci // chunk)
    return same & (ci <= ri), same & (ci >= ri)


def _softplus(x):
    return jnp.maximum(x, 0.0) + jnp.log1p(jnp.exp(-jnp.abs(x)))


def _zero_one(mask):
    return jnp.where(mask, 1.0, 0.0).astype(BF16)


def _masked_sums(zero_ones, x):
    hi = x.astype(BF16)
    rest = x - hi.astype(F32)
    mid = rest.astype(BF16)
    terms = (hi, mid, (rest - mid.astype(F32)).astype(BF16))
    outs = []
    for m in zero_ones:
        parts = [jnp.dot(m, t, preferred_element_type=F32) for t in terms]
        outs.append(parts[0] + parts[1] + parts[2])
    return outs


def _gdn_gates_kernel(g_ref, alog_ref, dtb_ref, o_ref, ot_ref, *, hv):
    x = g_ref[0]
    g = -jnp.exp(alog_ref[...]) * _softplus(x + dtb_ref[...])
    gc_f, gc_b = _masked_sums([_zero_one(m) for m in _chunk_tri_masks(TIME_TILE, GDN_CHUNK)], g)
    lane = lax.broadcasted_iota(jnp.int32, x.shape, 1)
    out = jnp.where(lane < hv, gc_f, jnp.where(lane < 2 * hv, gc_b, jax.nn.sigmoid(x)))
    o_ref[0] = out
    ot_ref[0] = out.T


def _gdn_gates(gates, a_log, dt_bias, hv):
    b, t, ch = gates.shape
    assert ch == LANES == 4 * hv and t % TIME_TILE == 0
    pad = jnp.zeros((1, 2 * hv), F32)
    alog = jnp.concatenate([a_log.reshape(1, 2 * hv).astype(F32), pad], axis=1)
    dtb = jnp.concatenate([dt_bias.reshape(1, 2 * hv).astype(F32), pad], axis=1)
    row = pl.BlockSpec((1, LANES), lambda b_, i: (0, 0))
    return pl.pallas_call(
        functools.partial(_gdn_gates_kernel, hv=hv),
        grid=(b, t // TIME_TILE),
        in_specs=[pl.BlockSpec((1, TIME_TILE, LANES), lambda b_, i: (b_, i, 0)), row, row],
        out_specs=[pl.BlockSpec((1, TIME_TILE, LANES), lambda b_, i: (b_, i, 0)),
                   pl.BlockSpec((1, LANES, TIME_TILE), lambda b_, i: (b_, 0, i))],
        out_shape=[jax.ShapeDtypeStruct((b, t, LANES), F32), jax.ShapeDtypeStruct((b, LANES, t), F32)],
        compiler_params=_cparams(("parallel", "parallel")),
        name="gdn_gates",
    )(gates, alog, dtb)


def _mm(a, b):
    return jnp.dot(a.astype(BF16), b.astype(BF16), preferred_element_type=F32)


def _mm_nt(a, b):
    return lax.dot_general(a.astype(BF16), b.astype(BF16), (((1,), (1,)), ((), ())), preferred_element_type=F32)


def _mm_tn(a, b):
    return lax.dot_general(a.astype(BF16), b.astype(BF16), (((0,), (0,)), ((), ())), preferred_element_type=F32)


def _half_rows(x, s, half):
    return jnp.concatenate([x[(2 * b + half) * s:(2 * b + half + 1) * s] for b in range(x.shape[0] // (2 * s))],
                           axis=0)


def _with_half_rows(x, new, s, half):
    parts = []
    for b in range(x.shape[0] // (2 * s)):
        keep = x[(2 * b + 1 - half) * s:(2 * b + 2 - half) * s]
        repl = new[b * s:(b + 1) * s]
        parts += [keep, repl] if half else [repl, keep]
    return jnp.concatenate(parts, axis=0)


def _tri_inverses(ms, uppers, size, top):
    ri = lax.broadcasted_iota(jnp.int32, (size, size), 0)
    ci = lax.broadcasted_iota(jnp.int32, (size, size), 1)
    eye = (ri == ci).astype(F32)
    zeros = jnp.zeros((size, size), F32)
    base_bits = int(math.log2(INV_BASE))
    base = (ri >> base_bits) == (ci >> base_bits)
    xps = [jnp.where(base, -m, 0.0) for m in ms]
    ps = [eye + x for x in xps]
    xps = [_mm(xp, xp) for xp in xps]
    for it in range(base_bits - 1):
        if it < base_bits - 2:
            both = [_mm(jnp.concatenate([p, xp], axis=0), xp) for p, xp in zip(ps, xps)]
            ps = [p + b[:size] for p, b in zip(ps, both)]
            xps = [b[size:] for b in both]
        else:
            ps = [p + _mm(p, xp) for p, xp in zip(ps, xps)]
    halves = [0 if up else 1 for up in uppers]
    bits = base_bits
    while (1 << bits) < top:
        s = 1 << bits
        pair = ((ri >> (bits + 1)) == (ci >> (bits + 1))) & ((ri >> bits) != (ci >> bits))
        qs = [_mm(_half_rows(jnp.where(pair, m, 0.0), s, h), p) for m, p, h in zip(ms, ps, halves)]
        rs = [_mm(_half_rows(p, s, h), _with_half_rows(zeros, q, s, h)) for p, q, h in zip(ps, qs, halves)]
        ps = [_with_half_rows(p, _half_rows(p, s, h) - r, s, h) for p, r, h in zip(ps, rs, halves)]
        bits += 1
    return ps


def _pick_lane(tile, lane_idx):
    lane = lax.broadcasted_iota(jnp.int32, tile.shape, 1)
    return jnp.sum(jnp.where(lane == lane_idx, tile, 0.0), axis=1, keepdims=True)


def _gdn_chunk_kernel(qf_ref, kf_ref, vf_ref, gf_ref, gtf_ref, qb_ref, kb_ref, vb_ref, gb_ref, gtb_ref,
                      of_ref, ob_ref, sf_ref, sb_ref, *, rep, hv, npairs):
    n = pl.program_id(2)
    hg = pl.program_id(1)

    @pl.when(n == 0)
    def _():
        sf_ref[...] = jnp.zeros_like(sf_ref)
        sb_ref[...] = jnp.zeros_like(sb_ref)

    c = GDN_CHUNK
    ri = lax.broadcasted_iota(jnp.int32, (c, c), 0)
    ci = lax.broadcasted_iota(jnp.int32, (c, c), 1)
    dirs = ((False, qf_ref, kf_ref, vf_ref, gf_ref, gtf_ref, of_ref, sf_ref),
            (True, qb_ref, kb_ref, vb_ref, gb_ref, gtb_ref, ob_ref, sb_ref))

    pairs = []
    for rev, q_ref, k_ref, *_ in dirs:
        for i in range(npairs):
            pairs.append((q_ref[0, :, i * LANES:(i + 1) * LANES], k_ref[0, :, i * LANES:(i + 1) * LANES]))
    qks = [_mm_nt(q, k) for q, k in pairs]
    kks = [_mm_nt(k, k) for _, k in pairs]

    probs = []
    for d, (rev, q_ref, k_ref, v_ref, g_ref, gt_ref, o_ref, s_ref) in enumerate(dirs):
        incl = (ci >= ri) if rev else (ci <= ri)
        strict = (ci > ri) if rev else (ci < ri)
        g_tile = g_ref[0]
        end = 0 if rev else c - 1
        for i in range(npairs):
            q, k = pairs[d * npairs + i]
            for r in range(rep):
                slot = i * rep + r
                h = (hg * npairs + i) * rep + r
                gcol = _pick_lane(g_tile, d * hv + h)
                bcol = _pick_lane(g_tile, 2 * hv + d * hv + h)
                grow = gt_ref[0, pl.ds(d * hv + h, 1), :]
                decay = jnp.where(incl, jnp.exp(jnp.where(incl, gcol - grow, 0.0)), 0.0)
                g_last = gcol[end:end + 1]
                egc = jnp.exp(gcol)
                v = v_ref[0, :, slot * LANES:(slot + 1) * LANES]
                probs.append(dict(
                    q=q, k=k, egc=egc, g_last=g_last, slot=slot, o_ref=o_ref, s_ref=s_ref, rev=rev,
                    a_kk=jnp.where(strict, kks[d * npairs + i] * bcol * decay, 0.0),
                    a_qk=qks[d * npairs + i] * decay,
                    rhs=jnp.concatenate([v * bcol, k * (bcol * egc)], axis=1),
                    k_dec=k * jnp.exp(g_last - gcol)))

    t_invs = _tri_inverses([p["a_kk"] for p in probs], [p["rev"] for p in probs], c, c)
    uws = [_mm(t, p["rhs"]) for t, p in zip(t_invs, probs)]
    aunms = [_mm(jnp.concatenate([p["a_qk"], p["k_dec"].T], axis=0), uw) for p, uw in zip(probs, uws)]
    aus = [x[:c] for x in aunms]
    nms = [x[c:] for x in aunms]

    states = [p["s_ref"][p["slot"]] for p in probs]
    lhss = [jnp.concatenate([p["q"] * p["egc"] - au[:, LANES:], nm[:, LANES:]], axis=0)
            for p, au, nm in zip(probs, aus, nms)]
    ress = [_mm(lhs, s) for lhs, s in zip(lhss, states)]
    for p, s, au, nm, res in zip(probs, states, aus, nms, ress):
        slot = p["slot"]
        p["o_ref"][0, :, slot * LANES:(slot + 1) * LANES] = (res[:c] + au[:, :LANES]).astype(p["o_ref"].dtype)
        p["s_ref"][slot] = s * jnp.exp(p["g_last"]) - res[c:] + nm[:, :LANES]


def _gdn_chunk(qkv, gcb, gcbt, hk, hv, dk, dv):
    b, t, _ = qkv.shape
    rep = hv // hk
    npairs = math.gcd(GDN_PAIRS, hk)
    assert dk == LANES and dv == LANES and t % TIME_TILE == 0 and GDN_CHUNK == TIME_TILE
    assert (2 * hk) % (npairs * rep) == 0
    nt = t // TIME_TILE
    qw, vw = npairs * LANES, npairs * rep * LANES
    kblk0 = hk // npairs
    vblk0 = 2 * hk // (npairs * rep)

    def specs(tile):
        return [pl.BlockSpec((1, TIME_TILE, qw), lambda b_, h, n: (b_, tile(n), h)),
                pl.BlockSpec((1, TIME_TILE, qw), lambda b_, h, n: (b_, tile(n), kblk0 + h)),
                pl.BlockSpec((1, TIME_TILE, vw), lambda b_, h, n: (b_, tile(n), vblk0 + h)),
                pl.BlockSpec((1, TIME_TILE, LANES), lambda b_, h, n: (b_, tile(n), 0)),
                pl.BlockSpec((1, LANES, TIME_TILE), lambda b_, h, n: (b_, 0, tile(n)))]

    fwd = lambda n: n
    bwd = lambda n: nt - 1 - n
    out_f = pl.BlockSpec((1, TIME_TILE, vw), lambda b_, h, n: (b_, n, h))
    out_b = pl.BlockSpec((1, TIME_TILE, vw), lambda b_, h, n: (b_, nt - 1 - n, h))
    shp = jax.ShapeDtypeStruct((b, t, hv * dv), BF16)
    return pl.pallas_call(
        functools.partial(_gdn_chunk_kernel, rep=rep, hv=hv, npairs=npairs),
        grid=(b, hk // npairs, nt),
        in_specs=specs(fwd) + specs(bwd),
        out_specs=[out_f, out_b],
        out_shape=[shp, shp],
        scratch_shapes=[pltpu.VMEM((npairs * rep, dk, dv), F32), pltpu.VMEM((npairs * rep, dk, dv), F32)],
        compiler_params=_cparams(("parallel", "parallel", "arbitrary")),
        name="gdn_chunk",
    )(qkv, qkv, qkv, gcb, gcbt, qkv, qkv, qkv, gcb, gcbt)


def _gdn_out_prologue(o_f, o_b, z, nw):
    parts = []
    for hh in range(o_f.shape[1] // LANES):
        sl = slice(hh * LANES, (hh + 1) * LANES)
        o = o_f[:, sl].astype(F32) + o_b[:, sl].astype(F32)
        o = o * lax.rsqrt(jnp.mean(o * o, axis=-1, keepdims=True) + 1e-6) * nw[:, sl]
        zz = z[:, sl]
        parts.append((o * (zz * jax.nn.sigmoid(zz))).astype(BF16))
    return jnp.concatenate(parts, axis=1) if len(parts) > 1 else parts[0]


def _gdn_mixer_ln(x, w, alpha, ln_g, ln_b):
    b, t, dm = x.shape
    m = b * t
    x2 = x.reshape(m, dm)
    kd, vd, hk, hv, dk, dv = w["kd"], w["vd"], w["hk"], w["hv"], w["dk"], w["dv"]
    qkvz, = _fused_matmul([x2], w["w_qkvz"], tm=1024, tn=1024, tk=dm, name="gdn_in")
    gates, = _fused_matmul([x2], w["w_gates"], tm=1024, tn=LANES, tk=dm, name="gdn_in_gates")
    qkv = _gdn_conv(qkvz.reshape(b, t, -1), w["conv"], kd, vd, dk)
    gcb, gcbt = _gdn_gates(gates.reshape(b, t, -1), w["a_log"], w["dt_bias"], hv)
    o_f, o_b = _gdn_chunk(qkv, gcb, gcbt, hk, hv, dk, dv)
    out, = _fused_matmul(
        [o_f.reshape(m, vd), o_b.reshape(m, vd), qkvz], w["w_out"], a_col0=[0, 0, 2 * kd + vd],
        rowk_ins=[w["norm_row"]], e_ins=[x2], rown_ins=[ln_g, ln_b],
        prologue=_gdn_out_prologue, epilogue=_ln_epilogue(alpha), tm=512, tn=dm, tk=1024, name="gdn_out")
    return out


def _shift_kernel(x_ref, xp_ref, xn_ref, o_ref, scr, *, tt, nt):
    _fill_halo(scr, x_ref, xp_ref, xn_ref, tt, nt)
    o_ref[0] = 0.5 * (scr[pl.ds(HALO - 1, tt), :] + scr[pl.ds(HALO + 1, tt), :]) - x_ref[0]


def _token_shift(x, tt=512, tc=1024):
    b, t, ch = x.shape
    tt, tc = min(tt, t), min(tc, ch)
    assert t % tt == 0 and ch % tc == 0
    main, prev, nxt = _halo_specs(tt, tc, t)
    return pl.pallas_call(
        functools.partial(_shift_kernel, tt=tt, nt=t // tt),
        grid=(b, t // tt, ch // tc),
        in_specs=[main, prev, nxt],
        out_specs=pl.BlockSpec((1, tt, tc), lambda b_, i, c: (b_, i, c)),
        out_shape=jax.ShapeDtypeStruct((b, t, ch), F32),
        scratch_shapes=[pltpu.VMEM((tt + 2 * HALO, tc), F32)],
        compiler_params=_cparams(("parallel", "parallel", "parallel")),
        name="rwkv_shift",
    )(x, x, x)


def _mix_prologue(x, xx, mix):
    return (x + xx * mix).astype(BF16)


def _same_head_ones():
    rl = lax.broadcasted_iota(jnp.int32, (LANES, LANES), 0)
    cl = lax.broadcasted_iota(jnp.int32, (LANES, LANES), 1)
    return jnp.where((rl // RWKV_N) == (cl // RWKV_N), 1.0, 0.0).astype(BF16)


def _head_sum(x, ones):
    hi = x.astype(BF16)
    lo = (x - hi.astype(F32)).astype(BF16)
    return (jnp.dot(hi, ones, preferred_element_type=F32) + jnp.dot(lo, ones, preferred_element_type=F32))


def _rwkv_chunk_kernel(rf_ref, kf_ref, vf_ref, lwf_ref, laf_ref, rb_ref, kb_ref, vb_ref, lwb_ref, lab_ref,
                       w2f_ref, a2f_ref, w0f_ref, a0f_ref, w2b_ref, a2b_ref, w0b_ref, a0b_ref,
                       kk_ref, ka_ref, rk_ref, yf_ref, bf_ref, yb_ref, bb_ref, gf_ref, gb_ref, *, nblk):
    n = pl.program_id(2)

    @pl.when(n == 0)
    def _():
        gf_ref[...] = jnp.zeros_like(gf_ref)
        gb_ref[...] = jnp.zeros_like(gb_ref)

    c, nn, tt = RWKV_CHUNK, RWKV_N, TIME_TILE
    nch, nh = tt // c, LANES // nn
    row = lax.broadcasted_iota(jnp.int32, (tt, LANES), 0)
    ri = lax.broadcasted_iota(jnp.int32, (tt, tt), 0)
    ci = lax.broadcasted_iota(jnp.int32, (tt, tt), 1)
    same = (ri // c) == (ci // c)
    ones = _same_head_ones()
    dirs = ((False, rf_ref, kf_ref, vf_ref, lwf_ref, laf_ref, w2f_ref, a2f_ref, w0f_ref, a0f_ref,
             yf_ref, bf_ref, gf_ref),
            (True, rb_ref, kb_ref, vb_ref, lwb_ref, lab_ref, w2b_ref, a2b_ref, w0b_ref, a0b_ref,
             yb_ref, bb_ref, gb_ref))

    probs, blocks = [], []
    for rev, r_ref, k_ref, v_ref, lw_ref, la_ref, w2_ref, a2_ref, w0_ref, a0_ref, y_ref, bonus_ref, g_ref in dirs:
        incl = same & ((ci >= ri) if rev else (ci <= ri))
        strict = same & ((ci > ri) if rev else (ci < ri))
        incl_ones = _zero_one(incl)
        ends = [cc * c if rev else (cc + 1) * c - 1 for cc in range(nch)]
        lw_all = -jnp.exp(-_softplus(-(w0_ref[...] + jnp.dot(lw_ref[0], w2_ref[...], preferred_element_type=F32)))
                          - 0.5)
        a_all = jax.nn.sigmoid(a0_ref[...] + jnp.dot(la_ref[0], a2_ref[...], preferred_element_type=F32))
        for blk in range(nblk):
            ls = slice(blk * LANES, (blk + 1) * LANES)
            r, k, v, lw, a = r_ref[0, :, ls], k_ref[0, :, ls], v_ref[0, :, ls], lw_all[:, ls], a_all[:, ls]
            kkk = k * kk_ref[:, ls]
            kk = kkk * lax.rsqrt(_head_sum(kkk * kkk, ones) + 1e-6)
            kd = k * (1.0 + (a - 1.0) * ka_ref[:, ls])
            av = -kk
            bv = kk * a
            bonus_ref[0, :, ls] = (_head_sum(r * kd * rk_ref[:, ls], ones) * v).astype(bonus_ref.dtype)
            gc, = _masked_sums([incl_ones], lw)
            gends = [gc[e:e + 1] for e in ends]
            gend_tile = gends[0]
            for cc in range(1, nch):
                gend_tile = jnp.where(row >= cc * c, gends[cc], gend_tile)
            ee = jnp.exp(gend_tile - gc)
            ei = jnp.exp(-gc)
            at, kt, bt, rt = av * jnp.exp(gc - lw), kd * ei, bv * ei, r * jnp.exp(gc)
            kh, bh = kd * ee, bv * ee
            blocks.append(dict(rev=rev, y_ref=y_ref, g_ref=g_ref, blk=blk, rt=rt, v=v, kh=kh, bh=bh,
                               egend=[jnp.exp(g) for g in gends], first=len(probs)))
            for j in range(nh):
                hs = slice(j * nn, (j + 1) * nn)
                probs.append(dict(rev=rev, incl=incl, strict=strict, at=at[:, hs], kt=kt[:, hs], bt=bt[:, hs],
                                  rt=rt[:, hs], v=v[:, hs]))

    bigs = [_mm_nt(jnp.concatenate([p["at"], p["rt"]], axis=0), jnp.concatenate([p["bt"], p["kt"]], axis=0))
            for p in probs]
    for p, big in zip(probs, bigs):
        p["a_ab"] = jnp.where(p["strict"], big[:tt, :tt], 0.0)
        p["a_ak"] = jnp.where(p["strict"], big[:tt, tt:], 0.0)
        p["m_rb"] = jnp.where(p["incl"], big[tt:, :tt], 0.0)
        p["m_rk"] = jnp.where(p["incl"], big[tt:, tt:], 0.0)
    avyvs = [_mm(jnp.concatenate([p["a_ak"], p["m_rk"]], axis=0), p["v"]) for p in probs]
    t_invs = _tri_inverses([-p["a_ab"] for p in probs], [p["rev"] for p in probs], tt, c)
    tauvs = [_mm(t, jnp.concatenate([p["at"], av[:tt]], axis=1)) for t, p, av in zip(t_invs, probs, avyvs)]
    mtmus = [_mm(p["m_rb"], tauv) for p, tauv in zip(probs, tauvs)]

    for blk in blocks:
        hd = range(blk["first"], blk["first"] + nh)
        blk["ta"] = jnp.concatenate([tauvs[i][:, :nn] for i in hd], axis=1)
        blk["uv"] = jnp.concatenate([tauvs[i][:, nn:] for i in hd], axis=1)
        blk["rq"] = blk["rt"] + jnp.concatenate([mtmus[i][:, :nn] for i in hd], axis=1)
        blk["yc"] = jnp.concatenate([avyvs[i][tt:] + mtmus[i][:, nn:] for i in hd], axis=1)

    def rows(cc):
        return slice(cc * c, (cc + 1) * c)

    rl = lax.broadcasted_iota(jnp.int32, (LANES, LANES), 0)
    cl = lax.broadcasted_iota(jnp.int32, (LANES, LANES), 1)
    same_head = (rl // nn) == (cl // nn)
    mps = [[_mm_tn(blk["ta"][rows(cc)], blk["bh"][rows(cc)]) for cc in range(nch)] for blk in blocks]
    n2s = [[_mm_tn(jnp.concatenate([blk["v"][rows(cc)], blk["uv"][rows(cc)]], axis=0),
                   jnp.concatenate([blk["kh"][rows(cc)], blk["bh"][rows(cc)]], axis=0)) for cc in range(nch)]
           for blk in blocks]
    states = [blk["g_ref"][blk["blk"]] for blk in blocks]
    ys = [[None] * nch for _ in blocks]
    for step in range(nch):
        ccs = [nch - 1 - step if blk["rev"] else step for blk in blocks]
        yparts = [_mm_nt(blk["rq"][rows(cc)], g) for blk, g, cc in zip(blocks, states, ccs)]
        gms = [_mm(g, jnp.where(same_head, mp[cc], 0.0)) for g, mp, cc in zip(states, mps, ccs)]
        for i, (blk, cc) in enumerate(zip(blocks, ccs)):
            ys[i][cc] = yparts[i] + blk["yc"][rows(cc)]
            states[i] = states[i] * blk["egend"][cc] + gms[i] + jnp.where(same_head, n2s[i][cc], 0.0)
    for blk, g, y in zip(blocks, states, ys):
        blk["g_ref"][blk["blk"]] = g
        blk["y_ref"][0, :, blk["blk"] * LANES:(blk["blk"] + 1) * LANES] = (
            jnp.concatenate(y, axis=0).astype(blk["y_ref"].dtype))


def _rwkv_chunk(rkv, lora, w, dm):
    b, t, _ = rkv.shape
    assert t % TIME_TILE == 0 and dm % LANES == 0 and LANES % RWKV_N == 0 and TIME_TILE % RWKV_CHUNK == 0
    nblk = math.gcd(RWKV_BLOCKS, dm // LANES)
    nt = t // TIME_TILE
    width = nblk * LANES
    nw = dm // width
    seg_blocks = w["lora_seg"] // LANES
    fwd = lambda n: n
    bwd = lambda n: nt - 1 - n

    def tiles(order, d):
        rkv_specs = [pl.BlockSpec((1, TIME_TILE, width), lambda b_, h, n, s=s: (b_, order(n), s * nw + h))
                     for s in range(3)]
        lora_specs = [pl.BlockSpec((1, TIME_TILE, LANES), lambda b_, h, n, s=s: (b_, order(n), s * seg_blocks + d))
                      for s in range(2)]
        return rkv_specs + lora_specs

    def out_tile(order):
        return pl.BlockSpec((1, TIME_TILE, width), lambda b_, h, n: (b_, order(n), h))

    wblk = pl.BlockSpec((LANES, width), lambda b_, h, n: (0, h))
    row = pl.BlockSpec((1, width), lambda b_, h, n: (0, h))
    shp = jax.ShapeDtypeStruct((b, t, dm), BF16)
    dir_weights = [z for d in range(2) for z in (w["w2"][d], w["a2"][d], w["w0"][d:d + 1], w["a0"][d:d + 1])]
    return pl.pallas_call(
        functools.partial(_rwkv_chunk_kernel, nblk=nblk),
        grid=(b, nw, nt),
        in_specs=tiles(fwd, 0) + tiles(bwd, 1) + [wblk, wblk, row, row] * 2 + [row] * 3,
        out_specs=[out_tile(fwd), out_tile(fwd), out_tile(bwd), out_tile(bwd)],
        out_shape=[shp] * 4,
        scratch_shapes=[pltpu.VMEM((nblk, LANES, LANES), F32), pltpu.VMEM((nblk, LANES, LANES), F32)],
        compiler_params=_cparams(("parallel", "parallel", "arbitrary")),
        name="rwkv_chunk",
    )(rkv, rkv, rkv, lora, lora, rkv, rkv, rkv, lora, lora, *dir_weights, w["k_k"], w["k_a"], w["r_k"])


def _rwkv_out_prologue(y_f, y_b, bn_f, bn_b, g, ln_w, ln_b):
    parts = []
    ones = _same_head_ones()
    for hh in range(y_f.shape[1] // LANES):
        sl = slice(hh * LANES, (hh + 1) * LANES)
        wkv = y_f[:, sl].astype(F32) + y_b[:, sl].astype(F32)
        mu = _head_sum(wkv, ones) * (1.0 / RWKV_N)
        wc = wkv - mu
        var = _head_sum(wc * wc, ones) * (1.0 / RWKV_N)
        o = (wc * lax.rsqrt(var + RWKV_GN_EPS) * ln_w[:, sl] + ln_b[:, sl]
             + (bn_f[:, sl].astype(F32) + bn_b[:, sl].astype(F32)))
        parts.append((o * g[:, sl]).astype(BF16))
    return jnp.concatenate(parts, axis=1) if len(parts) > 1 else parts[0]


def _rwkv_mixer_ln(x, w, alpha, ln_g, ln_b):
    b, t, dm = x.shape
    m = b * t
    x2 = x.reshape(m, dm)
    xx = _token_shift(x).reshape(m, dm)
    seg = w["lora_seg"]
    tn = min(1024, dm)
    rkv, = _fused_matmul([x2, xx], w["w_rkv"], rowk_ins=[w["mix_rkv"]], rowk_seg_tiles=dm // tn,
                         prologue=_mix_prologue, tm=1024, tn=tn, tk=dm, name="rwkv_rkv")

    def lora_epilogue(acc):
        j = pl.program_id(1)
        return (jnp.where(j == 0, jnp.tanh(acc), jnp.where(j == 2, jax.nn.sigmoid(acc), acc)),)

    lora, = _fused_matmul([x2, xx], w["w_lora"], rowk_ins=[w["mix_lora"]], rowk_seg_tiles=1,
                          prologue=_mix_prologue, epilogue=lora_epilogue, out_dtypes=(BF16,),
                          tm=1024, tn=seg, tk=dm, name="rwkv_lora")
    g, = _fused_matmul([lora], w["g2"], a_col0=[2 * seg], tm=1024, tn=1024, tk=seg, name="rwkv_g2")

    y_f, bn_f, y_b, bn_b = _rwkv_chunk(rkv.reshape(b, t, 3 * dm), lora.reshape(b, t, 3 * seg), w, dm)
    out, = _fused_matmul(
        [z.reshape(m, dm) for z in (y_f, y_b, bn_f, bn_b)] + [g], w["w_o"],
        rowk_ins=[w["ln_w"], w["ln_b"]], e_ins=[x2], rown_ins=[ln_g, ln_b],
        prologue=_rwkv_out_prologue, epilogue=_ln_epilogue(alpha), tm=512, tn=dm, tk=512, name="rwkv_out")
    return out


def _relu2_epilogue(acc):
    return (jnp.square(jnp.maximum(acc, 0.0)),)


def _mlp_ln(x2, w_up, w_down, alpha, ln_g, ln_b):
    dm = x2.shape[1]
    h, = _fused_matmul([x2], w_up, epilogue=_relu2_epilogue, out_dtypes=(BF16,), tm=1024, tn=1024, tk=dm,
                       name="mlp_up")
    out, = _fused_matmul([h], w_down, e_ins=[x2], rown_ins=[ln_g, ln_b], epilogue=_ln_epilogue(alpha),
                         tm=512, tn=dm, tk=2048, name="mlp_down")
    return out


def _ple_epilogue(acc, x, p, w_proj):
    return (x + jax.nn.sigmoid(acc) * jnp.dot(p.astype(BF16), w_proj, preferred_element_type=F32),)


def _ple(x2, p2, w_proj, w_gate):
    dm = x2.shape[1]
    pdim = p2.shape[1]
    extra = [(p2, lambda tm, tn: (tm, pdim), lambda i, j, k: (i, 0)),
             (w_proj, lambda tm, tn: (pdim, tn), lambda i, j, k: (0, j))]
    out, = _fused_matmul([x2], w_gate, e_ins=[x2], extra_ins=extra, epilogue=_ple_epilogue,
                         tm=1024, tn=1024, tk=dm, name="ple_gate")
    return out


def _pad_rows(w, rows):
    return jnp.pad(w, ((0, rows - w.shape[0]), (0, 0)))


def _pad_cols(w, cols):
    return jnp.pad(w, ((0, 0), (0, cols - w.shape[1])))


def _gdn_weights(w_in, conv, a_log, dt_bias, norm, w_out):
    hv, dv = a_log.shape[-1], norm.shape[-1]
    vd = hv * dv
    kd = (conv.shape[-1] - vd) // 2
    dk = dv
    assert w_in.shape[1] == 2 * kd + 2 * vd + 4 * hv and w_out.shape[0] == vd
    nz = 2 * kd + 2 * vd
    return dict(kd=kd, vd=vd, hk=kd // dk, hv=hv, dk=dk, dv=dv,
                w_qkvz=w_in[:, :nz].astype(BF16), w_gates=w_in[:, nz:].astype(BF16), conv=conv.astype(F32),
                a_log=a_log, dt_bias=dt_bias, norm_row=jnp.tile(norm.astype(F32), hv).reshape(1, vd),
                w_out=w_out.astype(BF16))


def _rwkv_weights(mix, w_rkv, w0, w1, w2, a0, a1, a2, g1, g2, k_k, k_a, r_k, ln_w, ln_b, w_o):
    dm = w_o.shape[0]
    assert w1.shape[-1] <= LANES and a1.shape[-1] <= LANES and r_k.size == dm and r_k.shape[-1] == RWKV_N
    seg = max(2 * LANES, -(-g1.shape[1] // LANES) * LANES)
    cat = lambda ws: _pad_cols(jnp.concatenate([_pad_cols(ws[d], LANES) for d in range(2)], axis=1), seg)
    row = lambda z: z.reshape(1, dm).astype(F32)
    mix = mix.astype(F32)
    return dict(lora_seg=seg,
                mix_rkv=jnp.stack([mix[0], mix[2], mix[3]]).reshape(3, 1, dm),
                mix_lora=jnp.stack([mix[1], mix[4], mix[5]]).reshape(3, 1, dm),
                w_rkv=jnp.concatenate([w_rkv[0], w_rkv[1], w_rkv[2]], axis=1).astype(BF16),
                w_lora=jnp.concatenate([cat(w1), cat(a1), _pad_cols(g1, seg)], axis=1).astype(BF16),
                w2=[_pad_rows(w2[d], LANES).astype(BF16) for d in range(2)],
                a2=[_pad_rows(a2[d], LANES).astype(BF16) for d in range(2)],
                g2=_pad_rows(g2, seg).astype(BF16), w0=w0.astype(F32), a0=a0.astype(F32),
                k_k=row(k_k), k_a=row(k_a), r_k=row(r_k), ln_w=row(ln_w), ln_b=row(ln_b), w_o=w_o.astype(BF16))


def _trunk(x, p, layers, alpha):
    b, t, dm = x.shape
    m = b * t
    for i, lyr in enumerate(layers):
        mixer = _gdn_mixer_ln if lyr["kind"] == "gdn" else _rwkv_mixer_ln
        x2 = mixer(x, lyr["mixer"], alpha, lyr["ln_g"][0:1], lyr["ln_b"][0:1])
        x2 = _mlp_ln(x2, lyr["w_up"], lyr["w_down"], alpha, lyr["ln_g"][1:2], lyr["ln_b"][1:2])
        x2 = _ple(x2, p[i].reshape(m, -1), lyr["w_proj"], lyr["w_gate"])
        x = x2.reshape(b, t, dm)
    return x


def kernel(x_prompt, x_sample, p_prompt, p_sample, gdn_w_in, gdn_conv, gdn_a_log, gdn_dt_bias, gdn_norm, gdn_w_out, rwkv_mix, rwkv_w_rkv, rwkv_w0, rwkv_w1, rwkv_w2, rwkv_a0, rwkv_a1, rwkv_a2, rwkv_g1, rwkv_g2, rwkv_k_k, rwkv_k_a, rwkv_r_k, rwkv_ln_w, rwkv_ln_b, rwkv_w_o, ln_g, ln_b, mlp_w_up, mlp_w_down, ple_w_proj, ple_w_gate):
    depth = ln_g.shape[0]
    alpha = (2 * depth) ** 0.25
    layers = []
    for i in range(depth):
        j = i // 2
        if i % 2 == 0:
            kind = "gdn"
            mixer = _gdn_weights(gdn_w_in[j], gdn_conv[j], gdn_a_log[j], gdn_dt_bias[j], gdn_norm[j], gdn_w_out[j])
        else:
            kind = "rwkv"
            mixer = _rwkv_weights(rwkv_mix[j], rwkv_w_rkv[j], rwkv_w0[j], rwkv_w1[j], rwkv_w2[j], rwkv_a0[j],
                                  rwkv_a1[j], rwkv_a2[j], rwkv_g1[j], rwkv_g2[j], rwkv_k_k[j], rwkv_k_a[j],
                                  rwkv_r_k[j], rwkv_ln_w[j], rwkv_ln_b[j], rwkv_w_o[j])
        layers.append(dict(kind=kind, mixer=mixer, ln_g=ln_g[i].astype(F32), ln_b=ln_b[i].astype(F32),
                           w_up=mlp_w_up[i].astype(BF16), w_down=mlp_w_down[i].astype(BF16),
                           w_proj=ple_w_proj[i].astype(BF16), w_gate=ple_w_gate[i].astype(BF16)))
    y_prompt = _trunk(x_prompt, p_prompt, layers, alpha)
    y_sample = _trunk(x_sample, p_sample, layers, alpha)
    return (y_prompt, y_sample)
```

```python
import functools
import math

import jax
import jax.numpy as jnp
from jax import lax
from jax.experimental import pallas as pl
from jax.experimental.pallas import tpu as pltpu

F32 = jnp.float32
BF16 = jnp.bfloat16

LANES = 128
HALO = 8
TIME_TILE = 128
GDN_CHUNK = TIME_TILE
RWKV_CHUNK = 64
GDN_PAIRS = 8
RWKV_BLOCKS = 8
INV_BASE = 8
VMEM_LIMIT = 56 * 1024 * 1024

LN_EPS = 1e-5
RWKV_GN_EPS = 64e-5
RWKV_N = 64
GDN_CONV = 5


def _cparams(sem):
    return pltpu.CompilerParams(dimension_semantics=sem, vmem_limit_bytes=VMEM_LIMIT)


def _mm_kernel(*refs, na, nrk, ne, nrn, nout, nk, prologue, epilogue):
    a_refs = refs[:na]
    rk_refs = refs[na:na + nrk]
    w_ref = refs[na + nrk]
    e_refs = refs[na + nrk + 1:na + nrk + 1 + ne]
    rn_refs = refs[na + nrk + 1 + ne:na + nrk + 1 + ne + nrn]
    out_refs = refs[na + nrk + 1 + ne + nrn:na + nrk + 1 + ne + nrn + nout]

    a = prologue(*[r[...] for r in a_refs], *[r[...] for r in rk_refs])
    part = jnp.dot(a, w_ref[...], preferred_element_type=F32)

    def finish(acc):
        outs = epilogue(acc, *[r[...] for r in e_refs], *[r[...] for r in rn_refs])
        for o_ref, o in zip(out_refs, outs):
            o_ref[...] = o.astype(o_ref.dtype)

    if nk == 1:
        finish(part)
        return

    acc_ref = refs[-1]
    k = pl.program_id(2)

    @pl.when(k == 0)
    def _():
        acc_ref[...] = part

    @pl.when(k > 0)
    def _():
        acc_ref[...] += part

    @pl.when(k == nk - 1)
    def _():
        finish(acc_ref[...])


def _cast_bf16(a):
    return a.astype(BF16)


def _identity_epilogue(acc):
    return (acc,)


def _fused_matmul(a_ins, w, *, a_col0=None, rowk_ins=(), rowk_seg_tiles=None, e_ins=(), rown_ins=(),
                  extra_ins=(), prologue=_cast_bf16, epilogue=_identity_epilogue, out_dtypes=(F32,),
                  tm=512, tn=512, tk=512, name="mm"):
    m = a_ins[0].shape[0]
    kdim, n = w.shape
    tm, tn, tk = min(tm, m), min(tn, n), min(tk, kdim)
    assert m % tm == 0 and n % tn == 0 and kdim % tk == 0, (m, n, kdim, tm, tn, tk)
    nk = kdim // tk
    a_col0 = [0] * len(a_ins) if a_col0 is None else a_col0
    assert all(c0 % tk == 0 for c0 in a_col0)

    def a_spec(c0):
        return pl.BlockSpec((tm, tk), lambda i, j, k: (i, c0 // tk + k))

    if rowk_seg_tiles is None:
        rowk_spec = pl.BlockSpec((1, tk), lambda i, j, k: (0, k))
    else:
        rowk_spec = pl.BlockSpec((None, 1, tk), lambda i, j, k: (j // rowk_seg_tiles, 0, k))
    in_specs = ([a_spec(c0) for c0 in a_col0]
                + [rowk_spec for _ in rowk_ins]
                + [pl.BlockSpec((tk, tn), lambda i, j, k: (k, j))]
                + [pl.BlockSpec((tm, tn), lambda i, j, k: (i, j)) for _ in e_ins]
                + [pl.BlockSpec((1, tn), lambda i, j, k: (0, j)) for _ in rown_ins]
                + [pl.BlockSpec(shape(tm, tn), imap) for _, shape, imap in extra_ins])
    out_specs = [pl.BlockSpec((tm, tn), lambda i, j, k: (i, j)) for _ in out_dtypes]
    out_shape = [jax.ShapeDtypeStruct((m, n), dt) for dt in out_dtypes]
    kern = functools.partial(_mm_kernel, na=len(a_ins), nrk=len(rowk_ins), ne=len(e_ins),
                             nrn=len(rown_ins) + len(extra_ins), nout=len(out_dtypes), nk=nk,
                             prologue=prologue, epilogue=epilogue)
    outs = pl.pallas_call(
        kern,
        grid=(m // tm, n // tn, nk),
        in_specs=in_specs,
        out_specs=out_specs,
        out_shape=out_shape,
        scratch_shapes=[pltpu.VMEM((tm, tn), F32)] if nk > 1 else [],
        compiler_params=_cparams(("parallel", "parallel", "arbitrary")),
        name=name,
    )(*a_ins, *rowk_ins, w, *e_ins, *rown_ins, *[x for x, _, _ in extra_ins])
    return outs


def _ln_epilogue(alpha):
    def epilogue(acc, xres, g, b):
        y = alpha * xres + acc
        mu = jnp.mean(y, axis=-1, keepdims=True)
        yc = y - mu
        var = jnp.mean(yc * yc, axis=-1, keepdims=True)
        return (yc * lax.rsqrt(var + LN_EPS) * g + b,)
    return epilogue


def _halo_specs(tt, tc, t_total):
    r = tt // HALO
    last = t_total // HALO - 1
    main = pl.BlockSpec((1, tt, tc), lambda b, i, c: (b, i, c))
    prev = pl.BlockSpec((1, HALO, tc), lambda b, i, c: (b, jnp.maximum(i * r - 1, 0), c))
    nxt = pl.BlockSpec((1, HALO, tc), lambda b, i, c: (b, jnp.minimum((i + 1) * r, last), c))
    return main, prev, nxt


def _fill_halo(scr, x_ref, xp_ref, xn_ref, tt, nt):
    i = pl.program_id(1)
    scr[0:HALO, :] = jnp.where(i > 0, xp_ref[0], 0.0)
    scr[HALO:HALO + tt, :] = x_ref[0]
    scr[HALO + tt:2 * HALO + tt, :] = jnp.where(i < nt - 1, xn_ref[0], 0.0)


def _gdn_conv_kernel(x_ref, xp_ref, xn_ref, w_ref, o_ref, scr, *, tt, tc, nt, nq_tiles, nqk_tiles, q_scale):
    _fill_halo(scr, x_ref, xp_ref, xn_ref, tt, nt)
    pad = (GDN_CONV - 1) // 2
    acc = None
    for j in range(GDN_CONV):
        term = w_ref[j:j + 1, :] * scr[pl.ds(HALO - pad + j, tt), :]
        acc = term if acc is None else acc + term
    y = acc * jax.nn.sigmoid(acc)
    c = pl.program_id(2)
    is_qk = c < nqk_tiles
    scale = jnp.where(c < nq_tiles, q_scale, 1.0).astype(F32)
    for hh in range(tc // LANES):
        blk = y[:, hh * LANES:(hh + 1) * LANES]
        ss = jnp.sum(blk * blk, axis=-1, keepdims=True)
        nrm = blk * (lax.rsqrt(ss + 1e-6) * scale)
        o_ref[0, :, hh * LANES:(hh + 1) * LANES] = jnp.where(is_qk, nrm, blk)


def _gdn_conv(qkv, conv_w, kd, vd, dk, tt=512, tc=2048):
    b, t, _ = qkv.shape
    ch = 2 * kd + vd
    tt = min(tt, t)
    assert t % tt == 0 and ch % tc == 0 and kd % tc == 0 and dk == LANES
    main, prev, nxt = _halo_specs(tt, tc, t)
    kern = functools.partial(_gdn_conv_kernel, tt=tt, tc=tc, nt=t // tt, nq_tiles=kd // tc,
                             nqk_tiles=2 * kd // tc, q_scale=dk ** -0.5)
    return pl.pallas_call(
        kern,
        grid=(b, t // tt, ch // tc),
        in_specs=[main, prev, nxt, pl.BlockSpec((GDN_CONV, tc), lambda b_, i, c: (0, c))],
        out_specs=pl.BlockSpec((1, tt, tc), lambda b_, i, c: (b_, i, c)),
        out_shape=jax.ShapeDtypeStruct((b, t, ch), F32),
        scratch_shapes=[pltpu.VMEM((tt + 2 * HALO, tc), F32)],
        compiler_params=_cparams(("parallel", "parallel", "parallel")),
        name="gdn_conv",
    )(qkv, qkv, qkv, conv_w)


def _chunk_tri_masks(rows, chunk):
    ri = lax.broadcasted_iota(jnp.int32, (rows, rows), 0)
    ci = lax.broadcasted_iota(jnp.int32, (rows, rows), 1)
    same = (ri // chunk) == (ci // chunk)
    return same & (ci <= ri), same & (ci >= ri)


def _softplus(x):
    return jnp.maximum(x, 0.0) + jnp.log1p(jnp.exp(-jnp.abs(x)))


def _zero_one(mask):
    return jnp.where(mask, 1.0, 0.0).astype(BF16)


def _masked_sums(zero_ones, x):
    hi = x.astype(BF16)
    rest = x - hi.astype(F32)
    mid = rest.astype(BF16)
    terms = (hi, mid, (rest - mid.astype(F32)).astype(BF16))
    outs = []
    for m in zero_ones:
        parts = [jnp.dot(m, t, preferred_element_type=F32) for t in terms]
        outs.append(parts[0] + parts[1] + parts[2])
    return outs


def _gdn_gates_kernel(g_ref, alog_ref, dtb_ref, o_ref, ot_ref, *, hv):
    x = g_ref[0]
    g = -jnp.exp(alog_ref[...]) * _softplus(x + dtb_ref[...])
    gc_f, gc_b = _masked_sums([_zero_one(m) for m in _chunk_tri_masks(TIME_TILE, GDN_CHUNK)], g)
    lane = lax.broadcasted_iota(jnp.int32, x.shape, 1)
    out = jnp.where(lane < hv, gc_f, jnp.where(lane < 2 * hv, gc_b, jax.nn.sigmoid(x)))
    o_ref[0] = out
    ot_ref[0] = out.T


def _gdn_gates(gates, a_log, dt_bias, hv):
    b, t, ch = gates.shape
    assert ch == LANES == 4 * hv and t % TIME_TILE == 0
    pad = jnp.zeros((1, 2 * hv), F32)
    alog = jnp.concatenate([a_log.reshape(1, 2 * hv).astype(F32), pad], axis=1)
    dtb = jnp.concatenate([dt_bias.reshape(1, 2 * hv).astype(F32), pad], axis=1)
    row = pl.BlockSpec((1, LANES), lambda b_, i: (0, 0))
    return pl.pallas_call(
        functools.partial(_gdn_gates_kernel, hv=hv),
        grid=(b, t // TIME_TILE),
        in_specs=[pl.BlockSpec((1, TIME_TILE, LANES), lambda b_, i: (b_, i, 0)), row, row],
        out_specs=[pl.BlockSpec((1, TIME_TILE, LANES), lambda b_, i: (b_, i, 0)),
                   pl.BlockSpec((1, LANES, TIME_TILE), lambda b_, i: (b_, 0, i))],
        out_shape=[jax.ShapeDtypeStruct((b, t, LANES), F32), jax.ShapeDtypeStruct((b, LANES, t), F32)],
        compiler_params=_cparams(("parallel", "parallel")),
        name="gdn_gates",
    )(gates, alog, dtb)


def _mm(a, b):
    return jnp.dot(a.astype(BF16), b.astype(BF16), preferred_element_type=F32)


def _mm_nt(a, b):
    return lax.dot_general(a.astype(BF16), b.astype(BF16), (((1,), (1,)), ((), ())), preferred_element_type=F32)


def _mm_tn(a, b):
    return lax.dot_general(a.astype(BF16), b.astype(BF16), (((0,), (0,)), ((), ())), preferred_element_type=F32)


def _half_rows(x, s, half):
    return jnp.concatenate([x[(2 * b + half) * s:(2 * b + half + 1) * s] for b in range(x.shape[0] // (2 * s))],
                           axis=0)


def _with_half_rows(x, new, s, half):
    parts = []
    for b in range(x.shape[0] // (2 * s)):
        keep = x[(2 * b + 1 - half) * s:(2 * b + 2 - half) * s]
        repl = new[b * s:(b + 1) * s]
        parts += [keep, repl] if half else [repl, keep]
    return jnp.concatenate(parts, axis=0)


def _tri_inverses(ms, uppers, size, top):
    ri = lax.broadcasted_iota(jnp.int32, (size, size), 0)
    ci = lax.broadcasted_iota(jnp.int32, (size, size), 1)
    eye = (ri == ci).astype(F32)
    zeros = jnp.zeros((size, size), F32)
    base_bits = int(math.log2(INV_BASE))
    base = (ri >> base_bits) == (ci >> base_bits)
    xps = [jnp.where(base, -m, 0.0) for m in ms]
    ps = [eye + x for x in xps]
    xps = [_mm(xp, xp) for xp in xps]
    for it in range(base_bits - 1):
        if it < base_bits - 2:
            both = [_mm(jnp.concatenate([p, xp], axis=0), xp) for p, xp in zip(ps, xps)]
            ps = [p + b[:size] for p, b in zip(ps, both)]
            xps = [b[size:] for b in both]
        else:
            ps = [p + _mm(p, xp) for p, xp in zip(ps, xps)]
    halves = [0 if up else 1 for up in uppers]
    bits = base_bits
    while (1 << bits) < top:
        s = 1 << bits
        pair = ((ri >> (bits + 1)) == (ci >> (bits + 1))) & ((ri >> bits) != (ci >> bits))
        qs = [_mm(_half_rows(jnp.where(pair, m, 0.0), s, h), p) for m, p, h in zip(ms, ps, halves)]
        rs = [_mm(_half_rows(p, s, h), _with_half_rows(zeros, q, s, h)) for p, q, h in zip(ps, qs, halves)]
        ps = [_with_half_rows(p, _half_rows(p, s, h) - r, s, h) for p, r, h in zip(ps, rs, halves)]
        bits += 1
    return ps


def _pick_lane(tile, lane_idx):
    lane = lax.broadcasted_iota(jnp.int32, tile.shape, 1)
    return jnp.sum(jnp.where(lane == lane_idx, tile, 0.0), axis=1, keepdims=True)


def _gdn_chunk_kernel(qf_ref, kf_ref, vf_ref, gf_ref, gtf_ref, qb_ref, kb_ref, vb_ref, gb_ref, gtb_ref,
                      of_ref, ob_ref, sf_ref, sb_ref, *, rep, hv, npairs):
    n = pl.program_id(2)
    hg = pl.program_id(1)

    @pl.when(n == 0)
    def _():
        sf_ref[...] = jnp.zeros_like(sf_ref)
        sb_ref[...] = jnp.zeros_like(sb_ref)

    c = GDN_CHUNK
    ri = lax.broadcasted_iota(jnp.int32, (c, c), 0)
    ci = lax.broadcasted_iota(jnp.int32, (c, c), 1)
    dirs = ((False, qf_ref, kf_ref, vf_ref, gf_ref, gtf_ref, of_ref, sf_ref),
            (True, qb_ref, kb_ref, vb_ref, gb_ref, gtb_ref, ob_ref, sb_ref))

    pairs = []
    for rev, q_ref, k_ref, *_ in dirs:
        for i in range(npairs):
            pairs.append((q_ref[0, :, i * LANES:(i + 1) * LANES], k_ref[0, :, i * LANES:(i + 1) * LANES]))
    qks = [_mm_nt(q, k) for q, k in pairs]
    kks = [_mm_nt(k, k) for _, k in pairs]

    probs = []
    for d, (rev, q_ref, k_ref, v_ref, g_ref, gt_ref, o_ref, s_ref) in enumerate(dirs):
        incl = (ci >= ri) if rev else (ci <= ri)
        strict = (ci > ri) if rev else (ci < ri)
        g_tile = g_ref[0]
        end = 0 if rev else c - 1
        for i in range(npairs):
            q, k = pairs[d * npairs + i]
            for r in range(rep):
                slot = i * rep + r
                h = (hg * npairs + i) * rep + r
                gcol = _pick_lane(g_tile, d * hv + h)
                bcol = _pick_lane(g_tile, 2 * hv + d * hv + h)
                grow = gt_ref[0, pl.ds(d * hv + h, 1), :]
                decay = jnp.where(incl, jnp.exp(jnp.where(incl, gcol - grow, 0.0)), 0.0)
                g_last = gcol[end:end + 1]
                egc = jnp.exp(gcol)
                v = v_ref[0, :, slot * LANES:(slot + 1) * LANES]
                probs.append(dict(
                    q=q, k=k, egc=egc, g_last=g_last, slot=slot, o_ref=o_ref, s_ref=s_ref, rev=rev,
                    a_kk=jnp.where(strict, kks[d * npairs + i] * bcol * decay, 0.0),
                    a_qk=qks[d * npairs + i] * decay,
                    rhs=jnp.concatenate([v * bcol, k * (bcol * egc)], axis=1),
                    k_dec=k * jnp.exp(g_last - gcol)))

    t_invs = _tri_inverses([p["a_kk"] for p in probs], [p["rev"] for p in probs], c, c)
    uws = [_mm(t, p["rhs"]) for t, p in zip(t_invs, probs)]
    aunms = [_mm(jnp.concatenate([p["a_qk"], p["k_dec"].T], axis=0), uw) for p, uw in zip(probs, uws)]
    aus = [x[:c] for x in aunms]
    nms = [x[c:] for x in aunms]

    states = [p["s_ref"][p["slot"]] for p in probs]
    lhss = [jnp.concatenate([p["q"] * p["egc"] - au[:, LANES:], nm[:, LANES:]], axis=0)
            for p, au, nm in zip(probs, aus, nms)]
    ress = [_mm(lhs, s) for lhs, s in zip(lhss, states)]
    for p, s, au, nm, res in zip(probs, states, aus, nms, ress):
        slot = p["slot"]
        p["o_ref"][0, :, slot * LANES:(slot + 1) * LANES] = (res[:c] + au[:, :LANES]).astype(p["o_ref"].dtype)
        p["s_ref"][slot] = s * jnp.exp(p["g_last"]) - res[c:] + nm[:, :LANES]


def _gdn_chunk(qkv, gcb, gcbt, hk, hv, dk, dv):
    b, t, _ = qkv.shape
    rep = hv // hk
    npairs = math.gcd(GDN_PAIRS, hk)
    assert dk == LANES and dv == LANES and t % TIME_TILE == 0 and GDN_CHUNK == TIME_TILE
    assert (2 * hk) % (npairs * rep) == 0
    nt = t // TIME_TILE
    qw, vw = npairs * LANES, npairs * rep * LANES
    kblk0 = hk // npairs
    vblk0 = 2 * hk // (npairs * rep)

    def specs(tile):
        return [pl.BlockSpec((1, TIME_TILE, qw), lambda b_, h, n: (b_, tile(n), h)),
                pl.BlockSpec((1, TIME_TILE, qw), lambda b_, h, n: (b_, tile(n), kblk0 + h)),
                pl.BlockSpec((1, TIME_TILE, vw), lambda b_, h, n: (b_, tile(n), vblk0 + h)),
                pl.BlockSpec((1, TIME_TILE, LANES), lambda b_, h, n: (b_, tile(n), 0)),
                pl.BlockSpec((1, LANES, TIME_TILE), lambda b_, h, n: (b_, 0, tile(n)))]

    fwd = lambda n: n
    bwd = lambda n: nt - 1 - n
    out_f = pl.BlockSpec((1, TIME_TILE, vw), lambda b_, h, n: (b_, n, h))
    out_b = pl.BlockSpec((1, TIME_TILE, vw), lambda b_, h, n: (b_, nt - 1 - n, h))
    shp = jax.ShapeDtypeStruct((b, t, hv * dv), BF16)
    return pl.pallas_call(
        functools.partial(_gdn_chunk_kernel, rep=rep, hv=hv, npairs=npairs),
        grid=(b, hk // npairs, nt),
        in_specs=specs(fwd) + specs(bwd),
        out_specs=[out_f, out_b],
        out_shape=[shp, shp],
        scratch_shapes=[pltpu.VMEM((npairs * rep, dk, dv), F32), pltpu.VMEM((npairs * rep, dk, dv), F32)],
        compiler_params=_cparams(("parallel", "parallel", "arbitrary")),
        name="gdn_chunk",
    )(qkv, qkv, qkv, gcb, gcbt, qkv, qkv, qkv, gcb, gcbt)


def _gdn_out_prologue(o_f, o_b, z, nw):
    parts = []
    for hh in range(o_f.shape[1] // LANES):
        sl = slice(hh * LANES, (hh + 1) * LANES)
        o = o_f[:, sl].astype(F32) + o_b[:, sl].astype(F32)
        o = o * lax.rsqrt(jnp.mean(o * o, axis=-1, keepdims=True) + 1e-6) * nw[:, sl]
        zz = z[:, sl]
        parts.append((o * (zz * jax.nn.sigmoid(zz))).astype(BF16))
    return jnp.concatenate(parts, axis=1) if len(parts) > 1 else parts[0]


def _gdn_mixer_ln(x, w, alpha, ln_g, ln_b):
    b, t, dm = x.shape
    m = b * t
    x2 = x.reshape(m, dm)
    kd, vd, hk, hv, dk, dv = w["kd"], w["vd"], w["hk"], w["hv"], w["dk"], w["dv"]
    qkvz, = _fused_matmul([x2], w["w_qkvz"], tm=1024, tn=1024, tk=dm, name="gdn_in")
    gates, = _fused_matmul([x2], w["w_gates"], tm=1024, tn=LANES, tk=dm, name="gdn_in_gates")
    qkv = _gdn_conv(qkvz.reshape(b, t, -1), w["conv"], kd, vd, dk)
    gcb, gcbt = _gdn_gates(gates.reshape(b, t, -1), w["a_log"], w["dt_bias"], hv)
    o_f, o_b = _gdn_chunk(qkv, gcb, gcbt, hk, hv, dk, dv)
    out, = _fused_matmul(
        [o_f.reshape(m, vd), o_b.reshape(m, vd), qkvz], w["w_out"], a_col0=[0, 0, 2 * kd + vd],
        rowk_ins=[w["norm_row"]], e_ins=[x2], rown_ins=[ln_g, ln_b],
        prologue=_gdn_out_prologue, epilogue=_ln_epilogue(alpha), tm=512, tn=dm, tk=1024, name="gdn_out")
    return out


def _shift_kernel(x_ref, xp_ref, xn_ref, o_ref, scr, *, tt, nt):
    _fill_halo(scr, x_ref, xp_ref, xn_ref, tt, nt)
    o_ref[0] = 0.5 * (scr[pl.ds(HALO - 1, tt), :] + scr[pl.ds(HALO + 1, tt), :]) - x_ref[0]


def _token_shift(x, tt=512, tc=1024):
    b, t, ch = x.shape
    tt, tc = min(tt, t), min(tc, ch)
    assert t % tt == 0 and ch % tc == 0
    main, prev, nxt = _halo_specs(tt, tc, t)
    return pl.pallas_call(
        functools.partial(_shift_kernel, tt=tt, nt=t // tt),
        grid=(b, t // tt, ch // tc),
        in_specs=[main, prev, nxt],
        out_specs=pl.BlockSpec((1, tt, tc), lambda b_, i, c: (b_, i, c)),
        out_shape=jax.ShapeDtypeStruct((b, t, ch), F32),
        scratch_shapes=[pltpu.VMEM((tt + 2 * HALO, tc), F32)],
        compiler_params=_cparams(("parallel", "parallel", "parallel")),
        name="rwkv_shift",
    )(x, x, x)


def _mix_prologue(x, xx, mix):
    return (x + xx * mix).astype(BF16)


def _same_head_ones():
    rl = lax.broadcasted_iota(jnp.int32, (LANES, LANES), 0)
    cl = lax.broadcasted_iota(jnp.int32, (LANES, LANES), 1)
    return jnp.where((rl // RWKV_N) == (cl // RWKV_N), 1.0, 0.0).astype(BF16)


def _head_sum(x, ones):
    hi = x.astype(BF16)
    lo = (x - hi.astype(F32)).astype(BF16)
    return (jnp.dot(hi, ones, preferred_element_type=F32) + jnp.dot(lo, ones, preferred_element_type=F32))


def _rwkv_chunk_kernel(rf_ref, kf_ref, vf_ref, lwf_ref, laf_ref, rb_ref, kb_ref, vb_ref, lwb_ref, lab_ref,
                       w2f_ref, a2f_ref, w0f_ref, a0f_ref, w2b_ref, a2b_ref, w0b_ref, a0b_ref,
                       kk_ref, ka_ref, rk_ref, yf_ref, bf_ref, yb_ref, bb_ref, gf_ref, gb_ref, *, nblk):
    n = pl.program_id(2)

    @pl.when(n == 0)
    def _():
        gf_ref[...] = jnp.zeros_like(gf_ref)
        gb_ref[...] = jnp.zeros_like(gb_ref)

    c, nn, tt = RWKV_CHUNK, RWKV_N, TIME_TILE
    nch, nh = tt // c, LANES // nn
    row = lax.broadcasted_iota(jnp.int32, (tt, LANES), 0)
    ri = lax.broadcasted_iota(jnp.int32, (tt, tt), 0)
    ci = lax.broadcasted_iota(jnp.int32, (tt, tt), 1)
    same = (ri // c) == (ci // c)
    ones = _same_head_ones()
    dirs = ((False, rf_ref, kf_ref, vf_ref, lwf_ref, laf_ref, w2f_ref, a2f_ref, w0f_ref, a0f_ref,
             yf_ref, bf_ref, gf_ref),
            (True, rb_ref, kb_ref, vb_ref, lwb_ref, lab_ref, w2b_ref, a2b_ref, w0b_ref, a0b_ref,
             yb_ref, bb_ref, gb_ref))

    probs, blocks = [], []
    for rev, r_ref, k_ref, v_ref, lw_ref, la_ref, w2_ref, a2_ref, w0_ref, a0_ref, y_ref, bonus_ref, g_ref in dirs:
        incl = same & ((ci >= ri) if rev else (ci <= ri))
        strict = same & ((ci > ri) if rev else (ci < ri))
        incl_ones = _zero_one(incl)
        ends = [cc * c if rev else (cc + 1) * c - 1 for cc in range(nch)]
        lw_all = -jnp.exp(-_softplus(-(w0_ref[...] + jnp.dot(lw_ref[0], w2_ref[...], preferred_element_type=F32)))
                          - 0.5)
        a_all = jax.nn.sigmoid(a0_ref[...] + jnp.dot(la_ref[0], a2_ref[...], preferred_element_type=F32))
        for blk in range(nblk):
            ls = slice(blk * LANES, (blk + 1) * LANES)
            r, k, v, lw, a = r_ref[0, :, ls], k_ref[0, :, ls], v_ref[0, :, ls], lw_all[:, ls], a_all[:, ls]
            kkk = k * kk_ref[:, ls]
            kk = kkk * lax.rsqrt(_head_sum(kkk * kkk, ones) + 1e-6)
            kd = k * (1.0 + (a - 1.0) * ka_ref[:, ls])
            av = -kk
            bv = kk * a
            bonus_ref[0, :, ls] = (_head_sum(r * kd * rk_ref[:, ls], ones) * v).astype(bonus_ref.dtype)
            gc, = _masked_sums([incl_ones], lw)
            gends = [gc[e:e + 1] for e in ends]
            gend_tile = gends[0]
            for cc in range(1, nch):
                gend_tile = jnp.where(row >= cc * c, gends[cc], gend_tile)
            ee = jnp.exp(gend_tile - gc)
            ei = jnp.exp(-gc)
            at, kt, bt, rt = av * jnp.exp(gc - lw), kd * ei, bv * ei, r * jnp.exp(gc)
            kh, bh = kd * ee, bv * ee
            blocks.append(dict(rev=rev, y_ref=y_ref, g_ref=g_ref, blk=blk, rt=rt, v=v, kh=kh, bh=bh,
                               egend=[jnp.exp(g) for g in gends], first=len(probs)))
            for j in range(nh):
                hs = slice(j * nn, (j + 1) * nn)
                probs.append(dict(rev=rev, incl=incl, strict=strict, at=at[:, hs], kt=kt[:, hs], bt=bt[:, hs],
                                  rt=rt[:, hs], v=v[:, hs]))

    bigs = [_mm_nt(jnp.concatenate([p["at"], p["rt"]], axis=0), jnp.concatenate([p["bt"], p["kt"]], axis=0))
            for p in probs]
    for p, big in zip(probs, bigs):
        p["a_ab"] = jnp.where(p["strict"], big[:tt, :tt], 0.0)
        p["a_ak"] = jnp.where(p["strict"], big[:tt, tt:], 0.0)
        p["m_rb"] = jnp.where(p["incl"], big[tt:, :tt], 0.0)
        p["m_rk"] = jnp.where(p["incl"], big[tt:, tt:], 0.0)
    avyvs = [_mm(jnp.concatenate([p["a_ak"], p["m_rk"]], axis=0), p["v"]) for p in probs]
    t_invs = _tri_inverses([-p["a_ab"] for p in probs], [p["rev"] for p in probs], tt, c)
    tauvs = [_mm(t, jnp.concatenate([p["at"], av[:tt]], axis=1)) for t, p, av in zip(t_invs, probs, avyvs)]
    mtmus = [_mm(p["m_rb"], tauv) for p, tauv in zip(probs, tauvs)]

    for blk in blocks:
        hd = range(blk["first"], blk["first"] + nh)
        blk["ta"] = jnp.concatenate([tauvs[i][:, :nn] for i in hd], axis=1)
        blk["uv"] = jnp.concatenate([tauvs[i][:, nn:] for i in hd], axis=1)
        blk["rq"] = blk["rt"] + jnp.concatenate([mtmus[i][:, :nn] for i in hd], axis=1)
        blk["yc"] = jnp.concatenate([avyvs[i][tt:] + mtmus[i][:, nn:] for i in hd], axis=1)

    def rows(cc):
        return slice(cc * c, (cc + 1) * c)

    rl = lax.broadcasted_iota(jnp.int32, (LANES, LANES), 0)
    cl = lax.broadcasted_iota(jnp.int32, (LANES, LANES), 1)
    same_head = (rl // nn) == (cl // nn)
    mps = [[_mm_tn(blk["ta"][rows(cc)], blk["bh"][rows(cc)]) for cc in range(nch)] for blk in blocks]
    n2s = [[_mm_tn(jnp.concatenate([blk["v"][rows(cc)], blk["uv"][rows(cc)]], axis=0),
                   jnp.concatenate([blk["kh"][rows(cc)], blk["bh"][rows(cc)]], axis=0)) for cc in range(nch)]
           for blk in blocks]
    states = [blk["g_ref"][blk["blk"]] for blk in blocks]
    ys = [[None] * nch for _ in blocks]
    for step in range(nch):
        ccs = [nch - 1 - step if blk["rev"] else step for blk in blocks]
        yparts = [_mm_nt(blk["rq"][rows(cc)], g) for blk, g, cc in zip(blocks, states, ccs)]
        gms = [_mm(g, jnp.where(same_head, mp[cc], 0.0)) for g, mp, cc in zip(states, mps, ccs)]
        for i, (blk, cc) in enumerate(zip(blocks, ccs)):
            ys[i][cc] = yparts[i] + blk["yc"][rows(cc)]
            states[i] = states[i] * blk["egend"][cc] + gms[i] + jnp.where(same_head, n2s[i][cc], 0.0)
    for blk, g, y in zip(blocks, states, ys):
        blk["g_ref"][blk["blk"]] = g
        blk["y_ref"][0, :, blk["blk"] * LANES:(blk["blk"] + 1) * LANES] = (
            jnp.concatenate(y, axis=0).astype(blk["y_ref"].dtype))


def _rwkv_chunk(rkv, lora, w, dm):
    b, t, _ = rkv.shape
    assert t % TIME_TILE == 0 and dm % LANES == 0 and LANES % RWKV_N == 0 and TIME_TILE % RWKV_CHUNK == 0
    nblk = math.gcd(RWKV_BLOCKS, dm // LANES)
    nt = t // TIME_TILE
    width = nblk * LANES
    nw = dm // width
    seg_blocks = w["lora_seg"] // LANES
    fwd = lambda n: n
    bwd = lambda n: nt - 1 - n

    def tiles(order, d):
        rkv_specs = [pl.BlockSpec((1, TIME_TILE, width), lambda b_, h, n, s=s: (b_, order(n), s * nw + h))
                     for s in range(3)]
        lora_specs = [pl.BlockSpec((1, TIME_TILE, LANES), lambda b_, h, n, s=s: (b_, order(n), s * seg_blocks + d))
                      for s in range(2)]
        return rkv_specs + lora_specs

    def out_tile(order):
        return pl.BlockSpec((1, TIME_TILE, width), lambda b_, h, n: (b_, order(n), h))

    wblk = pl.BlockSpec((LANES, width), lambda b_, h, n: (0, h))
    row = pl.BlockSpec((1, width), lambda b_, h, n: (0, h))
    shp = jax.ShapeDtypeStruct((b, t, dm), BF16)
    dir_weights = [z for d in range(2) for z in (w["w2"][d], w["a2"][d], w["w0"][d:d + 1], w["a0"][d:d + 1])]
    return pl.pallas_call(
        functools.partial(_rwkv_chunk_kernel, nblk=nblk),
        grid=(b, nw, nt),
        in_specs=tiles(fwd, 0) + tiles(bwd, 1) + [wblk, wblk, row, row] * 2 + [row] * 3,
        out_specs=[out_tile(fwd), out_tile(fwd), out_tile(bwd), out_tile(bwd)],
        out_shape=[shp] * 4,
        scratch_shapes=[pltpu.VMEM((nblk, LANES, LANES), F32), pltpu.VMEM((nblk, LANES, LANES), F32)],
        compiler_params=_cparams(("parallel", "parallel", "arbitrary")),
        name="rwkv_chunk",
    )(rkv, rkv, rkv, lora, lora, rkv, rkv, rkv, lora, lora, *dir_weights, w["k_k"], w["k_a"], w["r_k"])


def _rwkv_out_prologue(y_f, y_b, bn_f, bn_b, g, ln_w, ln_b):
    parts = []
    ones = _same_head_ones()
    for hh in range(y_f.shape[1] // LANES):
        sl = slice(hh * LANES, (hh + 1) * LANES)
        wkv = y_f[:, sl].astype(F32) + y_b[:, sl].astype(F32)
        mu = _head_sum(wkv, ones) * (1.0 / RWKV_N)
        wc = wkv - mu
        var = _head_sum(wc * wc, ones) * (1.0 / RWKV_N)
        o = (wc * lax.rsqrt(var + RWKV_GN_EPS) * ln_w[:, sl] + ln_b[:, sl]
             + (bn_f[:, sl].astype(F32) + bn_b[:, sl].astype(F32)))
        parts.append((o * g[:, sl]).astype(BF16))
    return jnp.concatenate(parts, axis=1) if len(parts) > 1 else parts[0]


def _rwkv_mixer_ln(x, w, alpha, ln_g, ln_b):
    b, t, dm = x.shape
    m = b * t
    x2 = x.reshape(m, dm)
    xx = _token_shift(x).reshape(m, dm)
    seg = w["lora_seg"]
    tn = min(1024, dm)
    rkv, = _fused_matmul([x2, xx], w["w_rkv"], rowk_ins=[w["mix_rkv"]], rowk_seg_tiles=dm // tn,
                         prologue=_mix_prologue, tm=1024, tn=tn, tk=dm, name="rwkv_rkv")

    def lora_epilogue(acc):
        j = pl.program_id(1)
        return (jnp.where(j == 0, jnp.tanh(acc), jnp.where(j == 2, jax.nn.sigmoid(acc), acc)),)

    lora, = _fused_matmul([x2, xx], w["w_lora"], rowk_ins=[w["mix_lora"]], rowk_seg_tiles=1,
                          prologue=_mix_prologue, epilogue=lora_epilogue, out_dtypes=(BF16,),
                          tm=1024, tn=seg, tk=dm, name="rwkv_lora")
    g, = _fused_matmul([lora], w["g2"], a_col0=[2 * seg], tm=1024, tn=1024, tk=seg, name="rwkv_g2")

    y_f, bn_f, y_b, bn_b = _rwkv_chunk(rkv.reshape(b, t, 3 * dm), lora.reshape(b, t, 3 * seg), w, dm)
    out, = _fused_matmul(
        [z.reshape(m, dm) for z in (y_f, y_b, bn_f, bn_b)] + [g], w["w_o"],
        rowk_ins=[w["ln_w"], w["ln_b"]], e_ins=[x2], rown_ins=[ln_g, ln_b],
        prologue=_rwkv_out_prologue, epilogue=_ln_epilogue(alpha), tm=512, tn=dm, tk=1024, name="rwkv_out")
    return out


def _relu2_epilogue(acc):
    return (jnp.square(jnp.maximum(acc, 0.0)),)


def _mlp_ln(x2, w_up, w_down, alpha, ln_g, ln_b):
    dm = x2.shape[1]
    h, = _fused_matmul([x2], w_up, epilogue=_relu2_epilogue, out_dtypes=(BF16,), tm=1024, tn=1024, tk=dm,
                       name="mlp_up")
    out, = _fused_matmul([h], w_down, e_ins=[x2], rown_ins=[ln_g, ln_b], epilogue=_ln_epilogue(alpha),
                         tm=512, tn=dm, tk=2048, name="mlp_down")
    return out


def _ple_epilogue(acc, x, p, w_proj):
    return (x + jax.nn.sigmoid(acc) * jnp.dot(p.astype(BF16), w_proj, preferred_element_type=F32),)


def _ple(x2, p2, w_proj, w_gate):
    dm = x2.shape[1]
    pdim = p2.shape[1]
    extra = [(p2, lambda tm, tn: (tm, pdim), lambda i, j, k: (i, 0)),
             (w_proj, lambda tm, tn: (pdim, tn), lambda i, j, k: (0, j))]
    out, = _fused_matmul([x2], w_gate, e_ins=[x2], extra_ins=extra, epilogue=_ple_epilogue,
                         tm=1024, tn=1024, tk=dm, name="ple_gate")
    return out


def _pad_rows(w, rows):
    return jnp.pad(w, ((0, rows - w.shape[0]), (0, 0)))


def _pad_cols(w, cols):
    return jnp.pad(w, ((0, 0), (0, cols - w.shape[1])))


def _gdn_weights(w_in, conv, a_log, dt_bias, norm, w_out):
    hv, dv = a_log.shape[-1], norm.shape[-1]
    vd = hv * dv
    kd = (conv.shape[-1] - vd) // 2
    dk = dv
    assert w_in.shape[1] == 2 * kd + 2 * vd + 4 * hv and w_out.shape[0] == vd
    nz = 2 * kd + 2 * vd
    return dict(kd=kd, vd=vd, hk=kd // dk, hv=hv, dk=dk, dv=dv,
                w_qkvz=w_in[:, :nz].astype(BF16), w_gates=w_in[:, nz:].astype(BF16), conv=conv.astype(F32),
                a_log=a_log, dt_bias=dt_bias, norm_row=jnp.tile(norm.astype(F32), hv).reshape(1, vd),
                w_out=w_out.astype(BF16))


def _rwkv_weights(mix, w_rkv, w0, w1, w2, a0, a1, a2, g1, g2, k_k, k_a, r_k, ln_w, ln_b, w_o):
    dm = w_o.shape[0]
    assert w1.shape[-1] <= LANES and a1.shape[-1] <= LANES and r_k.size == dm and r_k.shape[-1] == RWKV_N
    seg = max(2 * LANES, -(-g1.shape[1] // LANES) * LANES)
    cat = lambda ws: _pad_cols(jnp.concatenate([_pad_cols(ws[d], LANES) for d in range(2)], axis=1), seg)
    row = lambda z: z.reshape(1, dm).astype(F32)
    mix = mix.astype(F32)
    return dict(lora_seg=seg,
                mix_rkv=jnp.stack([mix[0], mix[2], mix[3]]).reshape(3, 1, dm),
                mix_lora=jnp.stack([mix[1], mix[4], mix[5]]).reshape(3, 1, dm),
                w_rkv=jnp.concatenate([w_rkv[0], w_rkv[1], w_rkv[2]], axis=1).astype(BF16),
                w_lora=jnp.concatenate([cat(w1), cat(a1), _pad_cols(g1, seg)], axis=1).astype(BF16),
                w2=[_pad_rows(w2[d], LANES).astype(BF16) for d in range(2)],
                a2=[_pad_rows(a2[d], LANES).astype(BF16) for d in range(2)],
                g2=_pad_rows(g2, seg).astype(BF16), w0=w0.astype(F32), a0=a0.astype(F32),
                k_k=row(k_k), k_a=row(k_a), r_k=row(r_k), ln_w=row(ln_w), ln_b=row(ln_b), w_o=w_o.astype(BF16))


def _trunk(x, p, layers, alpha):
    b, t, dm = x.shape
    m = b * t
    for i, lyr in enumerate(layers):
        mixer = _gdn_mixer_ln if lyr["kind"] == "gdn" else _rwkv_mixer_ln
        x2 = mixer(x, lyr["mixer"], alpha, lyr["ln_g"][0:1], lyr["ln_b"][0:1])
        x2 = _mlp_ln(x2, lyr["w_up"], lyr["w_down"], alpha, lyr["ln_g"][1:2], lyr["ln_b"][1:2])
        x2 = _ple(x2, p[i].reshape(m, -1), lyr["w_proj"], lyr["w_gate"])
        x = x2.reshape(b, t, dm)
    return x


def kernel(x_prompt, x_sample, p_prompt, p_sample, gdn_w_in, gdn_conv, gdn_a_log, gdn_dt_bias, gdn_norm, gdn_w_out, rwkv_mix, rwkv_w_rkv, rwkv_w0, rwkv_w1, rwkv_w2, rwkv_a0, rwkv_a1, rwkv_a2, rwkv_g1, rwkv_g2, rwkv_k_k, rwkv_k_a, rwkv_r_k, rwkv_ln_w, rwkv_ln_b, rwkv_w_o, ln_g, ln_b, mlp_w_up, mlp_w_down, ple_w_proj, ple_w_gate):
    depth = ln_g.shape[0]
    alpha = (2 * depth) ** 0.25
    layers = []
    for i in range(depth):
        j = i // 2
        if i % 2 == 0:
            kind = "gdn"
            mixer = _gdn_weights(gdn_w_in[j], gdn_conv[j], gdn_a_log[j], gdn_dt_bias[j], gdn_norm[j], gdn_w_out[j])
        else:
            kind = "rwkv"
            mixer = _rwkv_weights(rwkv_mix[j], rwkv_w_rkv[j], rwkv_w0[j], rwkv_w1[j], rwkv_w2[j], rwkv_a0[j],
                                  rwkv_a1[j], rwkv_a2[j], rwkv_g1[j], rwkv_g2[j], rwkv_k_k[j], rwkv_k_a[j],
                                  rwkv_r_k[j], rwkv_ln_w[j], rwkv_ln_b[j], rwkv_w_o[j])
        layers.append(dict(kind=kind, mixer=mixer, ln_g=ln_g[i].astype(F32), ln_b=ln_b[i].astype(F32),
                           w_up=mlp_w_up[i].astype(BF16), w_down=mlp_w_down[i].astype(BF16),
                           w_proj=ple_w_proj[i].astype(BF16), w_gate=ple_w_gate[i].astype(BF16)))
    y_prompt = _trunk(x_prompt, p_prompt, layers, alpha)
    y_sample = _trunk(x_sample, p_sample, layers, alpha)
    return (y_prompt, y_sample)
```

```python
import functools
import math

import jax
import jax.numpy as jnp
from jax import lax
from jax.experimental import pallas as pl
from jax.experimental.pallas import tpu as pltpu

F32 = jnp.float32
BF16 = jnp.bfloat16

LANES = 128
HALO = 8
TIME_TILE = 128
GDN_CHUNK = TIME_TILE
RWKV_CHUNK = 64
GDN_PAIRS = 8
RWKV_BLOCKS = 8
INV_BASE = 8
VMEM_LIMIT = 56 * 1024 * 1024

LN_EPS = 1e-5
RWKV_GN_EPS = 64e-5
RWKV_N = 64
GDN_CONV = 5


def _cparams(sem):
    return pltpu.CompilerParams(dimension_semantics=sem, vmem_limit_bytes=VMEM_LIMIT)


def _mm_kernel(*refs, na, nrk, ne, nrn, nout, nk, prologue, epilogue):
    a_refs = refs[:na]
    rk_refs = refs[na:na + nrk]
    w_ref = refs[na + nrk]
    e_refs = refs[na + nrk + 1:na + nrk + 1 + ne]
    rn_refs = refs[na + nrk + 1 + ne:na + nrk + 1 + ne + nrn]
    out_refs = refs[na + nrk + 1 + ne + nrn:na + nrk + 1 + ne + nrn + nout]

    a = prologue(*[r[...] for r in a_refs], *[r[...] for r in rk_refs])
    part = jnp.dot(a, w_ref[...], preferred_element_type=F32)

    def finish(acc):
        outs = epilogue(acc, *[r[...] for r in e_refs], *[r[...] for r in rn_refs])
        for o_ref, o in zip(out_refs, outs):
            o_ref[...] = o.astype(o_ref.dtype)

    if nk == 1:
        finish(part)
        return

    acc_ref = refs[-1]
    k = pl.program_id(2)

    @pl.when(k == 0)
    def _():
        acc_ref[...] = part

    @pl.when(k > 0)
    def _():
        acc_ref[...] += part

    @pl.when(k == nk - 1)
    def _():
        finish(acc_ref[...])


def _cast_bf16(a):
    return a.astype(BF16)


def _identity_epilogue(acc):
    return (acc,)


def _fused_matmul(a_ins, w, *, a_col0=None, rowk_ins=(), rowk_seg_tiles=None, e_ins=(), rown_ins=(),
                  extra_ins=(), prologue=_cast_bf16, epilogue=_identity_epilogue, out_dtypes=(F32,),
                  tm=512, tn=512, tk=512, name="mm"):
    m = a_ins[0].shape[0]
    kdim, n = w.shape
    tm, tn, tk = min(tm, m), min(tn, n), min(tk, kdim)
    assert m % tm == 0 and n % tn == 0 and kdim % tk == 0, (m, n, kdim, tm, tn, tk)
    nk = kdim // tk
    a_col0 = [0] * len(a_ins) if a_col0 is None else a_col0
    assert all(c0 % tk == 0 for c0 in a_col0)

    def a_spec(c0):
        return pl.BlockSpec((tm, tk), lambda i, j, k: (i, c0 // tk + k))

    if rowk_seg_tiles is None:
        rowk_spec = pl.BlockSpec((1, tk), lambda i, j, k: (0, k))
    else:
        rowk_spec = pl.BlockSpec((None, 1, tk), lambda i, j, k: (j // rowk_seg_tiles, 0, k))
    in_specs = ([a_spec(c0) for c0 in a_col0]
                + [rowk_spec for _ in rowk_ins]
                + [pl.BlockSpec((tk, tn), lambda i, j, k: (k, j))]
                + [pl.BlockSpec((tm, tn), lambda i, j, k: (i, j)) for _ in e_ins]
                + [pl.BlockSpec((1, tn), lambda i, j, k: (0, j)) for _ in rown_ins]
                + [pl.BlockSpec(shape(tm, tn), imap) for _, shape, imap in extra_ins])
    out_specs = [pl.BlockSpec((tm, tn), lambda i, j, k: (i, j)) for _ in out_dtypes]
    out_shape = [jax.ShapeDtypeStruct((m, n), dt) for dt in out_dtypes]
    kern = functools.partial(_mm_kernel, na=len(a_ins), nrk=len(rowk_ins), ne=len(e_ins),
                             nrn=len(rown_ins) + len(extra_ins), nout=len(out_dtypes), nk=nk,
                             prologue=prologue, epilogue=epilogue)
    outs = pl.pallas_call(
        kern,
        grid=(m // tm, n // tn, nk),
        in_specs=in_specs,
        out_specs=out_specs,
        out_shape=out_shape,
        scratch_shapes=[pltpu.VMEM((tm, tn), F32)] if nk > 1 else [],
        compiler_params=_cparams(("parallel", "parallel", "arbitrary")),
        name=name,
    )(*a_ins, *rowk_ins, w, *e_ins, *rown_ins, *[x for x, _, _ in extra_ins])
    return outs


def _ln_epilogue(alpha):
    def epilogue(acc, xres, g, b):
        y = alpha * xres + acc
        mu = jnp.mean(y, axis=-1, keepdims=True)
        yc = y - mu
        var = jnp.mean(yc * yc, axis=-1, keepdims=True)
        return (yc * lax.rsqrt(var + LN_EPS) * g + b,)
    return epilogue


def _halo_specs(tt, tc, t_total):
    r = tt // HALO
    last = t_total // HALO - 1
    main = pl.BlockSpec((1, tt, tc), lambda b, i, c: (b, i, c))
    prev = pl.BlockSpec((1, HALO, tc), lambda b, i, c: (b, jnp.maximum(i * r - 1, 0), c))
    nxt = pl.BlockSpec((1, HALO, tc), lambda b, i, c: (b, jnp.minimum((i + 1) * r, last), c))
    return main, prev, nxt


def _fill_halo(scr, x_ref, xp_ref, xn_ref, tt, nt):
    i = pl.program_id(1)
    scr[0:HALO, :] = jnp.where(i > 0, xp_ref[0], 0.0)
    scr[HALO:HALO + tt, :] = x_ref[0]
    scr[HALO + tt:2 * HALO + tt, :] = jnp.where(i < nt - 1, xn_ref[0], 0.0)


def _gdn_conv_kernel(x_ref, xp_ref, xn_ref, w_ref, o_ref, scr, *, tt, tc, nt, nq_tiles, nqk_tiles, q_scale):
    _fill_halo(scr, x_ref, xp_ref, xn_ref, tt, nt)
    pad = (GDN_CONV - 1) // 2
    acc = None
    for j in range(GDN_CONV):
        term = w_ref[j:j + 1, :] * scr[pl.ds(HALO - pad + j, tt), :]
        acc = term if acc is None else acc + term
    y = acc * jax.nn.sigmoid(acc)
    c = pl.program_id(2)
    is_qk = c < nqk_tiles
    scale = jnp.where(c < nq_tiles, q_scale, 1.0).astype(F32)
    for hh in range(tc // LANES):
        blk = y[:, hh * LANES:(hh + 1) * LANES]
        ss = jnp.sum(blk * blk, axis=-1, keepdims=True)
        nrm = blk * (lax.rsqrt(ss + 1e-6) * scale)
        o_ref[0, :, hh * LANES:(hh + 1) * LANES] = jnp.where(is_qk, nrm, blk)


def _gdn_conv(qkv, conv_w, kd, vd, dk, tt=512, tc=2048):
    b, t, _ = qkv.shape
    ch = 2 * kd + vd
    tt = min(tt, t)
    assert t % tt == 0 and ch % tc == 0 and kd % tc == 0 and dk == LANES
    main, prev, nxt = _halo_specs(tt, tc, t)
    kern = functools.partial(_gdn_conv_kernel, tt=tt, tc=tc, nt=t // tt, nq_tiles=kd // tc,
                             nqk_tiles=2 * kd // tc, q_scale=dk ** -0.5)
    return pl.pallas_call(
        kern,
        grid=(b, t // tt, ch // tc),
        in_specs=[main, prev, nxt, pl.BlockSpec((GDN_CONV, tc), lambda b_, i, c: (0, c))],
        out_specs=pl.BlockSpec((1, tt, tc), lambda b_, i, c: (b_, i, c)),
        out_shape=jax.ShapeDtypeStruct((b, t, ch), F32),
        scratch_shapes=[pltpu.VMEM((tt + 2 * HALO, tc), F32)],
        compiler_params=_cparams(("parallel", "parallel", "parallel")),
        name="gdn_conv",
    )(qkv, qkv, qkv, conv_w)


def _chunk_tri_masks(rows, chunk):
    ri = lax.broadcasted_iota(jnp.int32, (rows, rows), 0)
    ci = lax.broadcasted_iota(jnp.int32, (rows, rows), 1)
    same = (ri // chunk) == (ci // chunk)
    return same & (ci <= ri), same & (ci >= ri)


def _softplus(x):
    return jnp.maximum(x, 0.0) + jnp.log1p(jnp.exp(-jnp.abs(x)))


def _zero_one(mask):
    return jnp.where(mask, 1.0, 0.0).astype(BF16)


def _masked_sums(zero_ones, x):
    hi = x.astype(BF16)
    rest = x - hi.astype(F32)
    mid = rest.astype(BF16)
    terms = (hi, mid, (rest - mid.astype(F32)).astype(BF16))
    outs = []
    for m in zero_ones:
        parts = [jnp.dot(m, t, preferred_element_type=F32) for t in terms]
        outs.append(parts[0] + parts[1] + parts[2])
    return outs


def _gdn_gates_kernel(g_ref, alog_ref, dtb_ref, o_ref, ot_ref, *, hv):
    x = g_ref[0]
    g = -jnp.exp(alog_ref[...]) * _softplus(x + dtb_ref[...])
    gc_f, gc_b = _masked_sums([_zero_one(m) for m in _chunk_tri_masks(TIME_TILE, GDN_CHUNK)], g)
    lane = lax.broadcasted_iota(jnp.int32, x.shape, 1)
    out = jnp.where(lane < hv, gc_f, jnp.where(lane < 2 * hv, gc_b, jax.nn.sigmoid(x)))
    o_ref[0] = out
    ot_ref[0] = out.T


def _gdn_gates(gates, a_log, dt_bias, hv):
    b, t, ch = gates.shape
    assert ch == LANES == 4 * hv and t % TIME_TILE == 0
    pad = jnp.zeros((1, 2 * hv), F32)
    alog = jnp.concatenate([a_log.reshape(1, 2 * hv).astype(F32), pad], axis=1)
    dtb = jnp.concatenate([dt_bias.reshape(1, 2 * hv).astype(F32), pad], axis=1)
    row = pl.BlockSpec((1, LANES), lambda b_, i: (0, 0))
    return pl.pallas_call(
        functools.partial(_gdn_gates_kernel, hv=hv),
        grid=(b, t // TIME_TILE),
        in_specs=[pl.BlockSpec((1, TIME_TILE, LANES), lambda b_, i: (b_, i, 0)), row, row],
        out_specs=[pl.BlockSpec((1, TIME_TILE, LANES), lambda b_, i: (b_, i, 0)),
                   pl.BlockSpec((1, LANES, TIME_TILE), lambda b_, i: (b_, 0, i))],
        out_shape=[jax.ShapeDtypeStruct((b, t, LANES), F32), jax.ShapeDtypeStruct((b, LANES, t), F32)],
        compiler_params=_cparams(("parallel", "parallel")),
        name="gdn_gates",
    )(gates, alog, dtb)


def _mm(a, b):
    return jnp.dot(a.astype(BF16), b.astype(BF16), preferred_element_type=F32)


def _mm_nt(a, b):
    return lax.dot_general(a.astype(BF16), b.astype(BF16), (((1,), (1,)), ((), ())), preferred_element_type=F32)


def _mm_tn(a, b):
    return lax.dot_general(a.astype(BF16), b.astype(BF16), (((0,), (0,)), ((), ())), preferred_element_type=F32)


def _half_rows(x, s, half):
    return jnp.concatenate([x[(2 * b + half) * s:(2 * b + half + 1) * s] for b in range(x.shape[0] // (2 * s))],
                           axis=0)


def _with_half_rows(x, new, s, half):
    parts = []
    for b in range(x.shape[0] // (2 * s)):
        keep = x[(2 * b + 1 - half) * s:(2 * b + 2 - half) * s]
        repl = new[b * s:(b + 1) * s]
        parts += [keep, repl] if half else [repl, keep]
    return jnp.concatenate(parts, axis=0)


def _tri_inverses(ms, uppers, size, top):
    ri = lax.broadcasted_iota(jnp.int32, (size, size), 0)
    ci = lax.broadcasted_iota(jnp.int32, (size, size), 1)
    eye = (ri == ci).astype(F32)
    zeros = jnp.zeros((size, size), F32)
    base_bits = int(math.log2(INV_BASE))
    base = (ri >> base_bits) == (ci >> base_bits)
    xps = [jnp.where(base, -m, 0.0) for m in ms]
    ps = [eye + x for x in xps]
    xps = [_mm(xp, xp) for xp in xps]
    for it in range(base_bits - 1):
        if it < base_bits - 2:
            both = [_mm(jnp.concatenate([p, xp], axis=0), xp) for p, xp in zip(ps, xps)]
            ps = [p + b[:size] for p, b in zip(ps, both)]
            xps = [b[size:] for b in both]
        else:
            ps = [p + _mm(p, xp) for p, xp in zip(ps, xps)]
    halves = [0 if up else 1 for up in uppers]
    bits = base_bits
    while (1 << bits) < top:
        s = 1 << bits
        pair = ((ri >> (bits + 1)) == (ci >> (bits + 1))) & ((ri >> bits) != (ci >> bits))
        qs = [_mm(_half_rows(jnp.where(pair, m, 0.0), s, h), p) for m, p, h in zip(ms, ps, halves)]
        rs = [_mm(_half_rows(p, s, h), _with_half_rows(zeros, q, s, h)) for p, q, h in zip(ps, qs, halves)]
        ps = [_with_half_rows(p, _half_rows(p, s, h) - r, s, h) for p, r, h in zip(ps, rs, halves)]
        bits += 1
    return ps


def _pick_lane(tile, lane_idx):
    lane = lax.broadcasted_iota(jnp.int32, tile.shape, 1)
    return jnp.sum(jnp.where(lane == lane_idx, tile, 0.0), axis=1, keepdims=True)


def _gdn_chunk_kernel(qf_ref, kf_ref, vf_ref, gf_ref, gtf_ref, qb_ref, kb_ref, vb_ref, gb_ref, gtb_ref,
                      of_ref, ob_ref, sf_ref, sb_ref, *, rep, hv, npairs):
    n = pl.program_id(2)
    hg = pl.program_id(1)

    @pl.when(n == 0)
    def _():
        sf_ref[...] = jnp.zeros_like(sf_ref)
        sb_ref[...] = jnp.zeros_like(sb_ref)

    c = GDN_CHUNK
    ri = lax.broadcasted_iota(jnp.int32, (c, c), 0)
    ci = lax.broadcasted_iota(jnp.int32, (c, c), 1)
    dirs = ((False, qf_ref, kf_ref, vf_ref, gf_ref, gtf_ref, of_ref, sf_ref),
            (True, qb_ref, kb_ref, vb_ref, gb_ref, gtb_ref, ob_ref, sb_ref))

    pairs = []
    for rev, q_ref, k_ref, *_ in dirs:
        for i in range(npairs):
            pairs.append((q_ref[0, :, i * LANES:(i + 1) * LANES], k_ref[0, :, i * LANES:(i + 1) * LANES]))
    qks = [_mm_nt(q, k) for q, k in pairs]
    kks = [_mm_nt(k, k) for _, k in pairs]

    probs = []
    for d, (rev, q_ref, k_ref, v_ref, g_ref, gt_ref, o_ref, s_ref) in enumerate(dirs):
        incl = (ci >= ri) if rev else (ci <= ri)
        strict = (ci > ri) if rev else (ci < ri)
        g_tile = g_ref[0]
        end = 0 if rev else c - 1
        for i in range(npairs):
            q, k = pairs[d * npairs + i]
            for r in range(rep):
                slot = i * rep + r
                h = (hg * npairs + i) * rep + r
                gcol = _pick_lane(g_tile, d * hv + h)
                bcol = _pick_lane(g_tile, 2 * hv + d * hv + h)
                grow = gt_ref[0, pl.ds(d * hv + h, 1), :]
                decay = jnp.where(incl, jnp.exp(jnp.where(incl, gcol - grow, 0.0)), 0.0)
                g_last = gcol[end:end + 1]
                egc = jnp.exp(gcol)
                v = v_ref[0, :, slot * LANES:(slot + 1) * LANES]
                probs.append(dict(
                    q=q, k=k, egc=egc, g_last=g_last, slot=slot, o_ref=o_ref, s_ref=s_ref, rev=rev,
                    a_kk=jnp.where(strict, kks[d * npairs + i] * bcol * decay, 0.0),
                    a_qk=qks[d * npairs + i] * decay,
                    rhs=jnp.concatenate([v * bcol, k * (bcol * egc)], axis=1),
                    k_dec=k * jnp.exp(g_last - gcol)))

    t_invs = _tri_inverses([p["a_kk"] for p in probs], [p["rev"] for p in probs], c, c)
    uws = [_mm(t, p["rhs"]) for t, p in zip(t_invs, probs)]
    aunms = [_mm(jnp.concatenate([p["a_qk"], p["k_dec"].T], axis=0), uw) for p, uw in zip(probs, uws)]
    aus = [x[:c] for x in aunms]
    nms = [x[c:] for x in aunms]

    states = [p["s_ref"][p["slot"]] for p in probs]
    lhss = [jnp.concatenate([p["q"] * p["egc"] - au[:, LANES:], nm[:, LANES:]], axis=0)
            for p, au, nm in zip(probs, aus, nms)]
    ress = [_mm(lhs, s) for lhs, s in zip(lhss, states)]
    for p, s, au, nm, res in zip(probs, states, aus, nms, ress):
        slot = p["slot"]
        p["o_ref"][0, :, slot * LANES:(slot + 1) * LANES] = (res[:c] + au[:, :LANES]).astype(p["o_ref"].dtype)
        p["s_ref"][slot] = s * jnp.exp(p["g_last"]) - res[c:] + nm[:, :LANES]


def _gdn_chunk(qkv, gcb, gcbt, hk, hv, dk, dv):
    b, t, _ = qkv.shape
    rep = hv // hk
    npairs = math.gcd(GDN_PAIRS, hk)
    assert dk == LANES and dv == LANES and t % TIME_TILE == 0 and GDN_CHUNK == TIME_TILE
    assert (2 * hk) % (npairs * rep) == 0
    nt = t // TIME_TILE
    qw, vw = npairs * LANES, npairs * rep * LANES
    kblk0 = hk // npairs
    vblk0 = 2 * hk // (npairs * rep)

    def specs(tile):
        return [pl.BlockSpec((1, TIME_TILE, qw), lambda b_, h, n: (b_, tile(n), h)),
                pl.BlockSpec((1, TIME_TILE, qw), lambda b_, h, n: (b_, tile(n), kblk0 + h)),
                pl.BlockSpec((1, TIME_TILE, vw), lambda b_, h, n: (b_, tile(n), vblk0 + h)),
                pl.BlockSpec((1, TIME_TILE, LANES), lambda b_, h, n: (b_, tile(n), 0)),
                pl.BlockSpec((1, LANES, TIME_TILE), lambda b_, h, n: (b_, 0, tile(n)))]

    fwd = lambda n: n
    bwd = lambda n: nt - 1 - n
    out_f = pl.BlockSpec((1, TIME_TILE, vw), lambda b_, h, n: (b_, n, h))
    out_b = pl.BlockSpec((1, TIME_TILE, vw), lambda b_, h, n: (b_, nt - 1 - n, h))
    shp = jax.ShapeDtypeStruct((b, t, hv * dv), BF16)
    return pl.pallas_call(
        functools.partial(_gdn_chunk_kernel, rep=rep, hv=hv, npairs=npairs),
        grid=(b, hk // npairs, nt),
        in_specs=specs(fwd) + specs(bwd),
        out_specs=[out_f, out_b],
        out_shape=[shp, shp],
        scratch_shapes=[pltpu.VMEM((npairs * rep, dk, dv), F32), pltpu.VMEM((npairs * rep, dk, dv), F32)],
        compiler_params=_cparams(("parallel", "parallel", "arbitrary")),
        name="gdn_chunk",
    )(qkv, qkv, qkv, gcb, gcbt, qkv, qkv, qkv, gcb, gcbt)


def _gdn_out_prologue(o_f, o_b, z, nw):
    parts = []
    for hh in range(o_f.shape[1] // LANES):
        sl = slice(hh * LANES, (hh + 1) * LANES)
        o = o_f[:, sl].astype(F32) + o_b[:, sl].astype(F32)
        o = o * lax.rsqrt(jnp.mean(o * o, axis=-1, keepdims=True) + 1e-6) * nw[:, sl]
        zz = z[:, sl]
        parts.append((o * (zz * jax.nn.sigmoid(zz))).astype(BF16))
    return jnp.concatenate(parts, axis=1) if len(parts) > 1 else parts[0]


def _gdn_mixer_ln(x, w, alpha, ln_g, ln_b):
    b, t, dm = x.shape
    m = b * t
    x2 = x.reshape(m, dm)
    kd, vd, hk, hv, dk, dv = w["kd"], w["vd"], w["hk"], w["hv"], w["dk"], w["dv"]
    qkvz, = _fused_matmul([x2], w["w_qkvz"], tm=1024, tn=1024, tk=dm, name="gdn_in")
    gates, = _fused_matmul([x2], w["w_gates"], tm=1024, tn=LANES, tk=dm, name="gdn_in_gates")
    qkv = _gdn_conv(qkvz.reshape(b, t, -1), w["conv"], kd, vd, dk)
    gcb, gcbt = _gdn_gates(gates.reshape(b, t, -1), w["a_log"], w["dt_bias"], hv)
    o_f, o_b = _gdn_chunk(qkv, gcb, gcbt, hk, hv, dk, dv)
    out, = _fused_matmul(
        [o_f.reshape(m, vd), o_b.reshape(m, vd), qkvz], w["w_out"], a_col0=[0, 0, 2 * kd + vd],
        rowk_ins=[w["norm_row"]], e_ins=[x2], rown_ins=[ln_g, ln_b],
        prologue=_gdn_out_prologue, epilogue=_ln_epilogue(alpha), tm=512, tn=dm, tk=1024, name="gdn_out")
    return out


def _shift_kernel(x_ref, xp_ref, xn_ref, o_ref, scr, *, tt, nt):
    _fill_halo(scr, x_ref, xp_ref, xn_ref, tt, nt)
    o_ref[0] = 0.5 * (scr[pl.ds(HALO - 1, tt), :] + scr[pl.ds(HALO + 1, tt), :]) - x_ref[0]


def _token_shift(x, tt=512, tc=1024):
    b, t, ch = x.shape
    tt, tc = min(tt, t), min(tc, ch)
    assert t % tt == 0 and ch % tc == 0
    main, prev, nxt = _halo_specs(tt, tc, t)
    return pl.pallas_call(
        functools.partial(_shift_kernel, tt=tt, nt=t // tt),
        grid=(b, t // tt, ch // tc),
        in_specs=[main, prev, nxt],
        out_specs=pl.BlockSpec((1, tt, tc), lambda b_, i, c: (b_, i, c)),
        out_shape=jax.ShapeDtypeStruct((b, t, ch), F32),
        scratch_shapes=[pltpu.VMEM((tt + 2 * HALO, tc), F32)],
        compiler_params=_cparams(("parallel", "parallel", "parallel")),
        name="rwkv_shift",
    )(x, x, x)


def _mix_prologue(x, xx, mix):
    return (x + xx * mix).astype(BF16)


def _same_head_ones():
    rl = lax.broadcasted_iota(jnp.int32, (LANES, LANES), 0)
    cl = lax.broadcasted_iota(jnp.int32, (LANES, LANES), 1)
    return jnp.where((rl // RWKV_N) == (cl // RWKV_N), 1.0, 0.0).astype(BF16)


def _head_sum(x, ones):
    hi = x.astype(BF16)
    lo = (x - hi.astype(F32)).astype(BF16)
    return (jnp.dot(hi, ones, preferred_element_type=F32) + jnp.dot(lo, ones, preferred_element_type=F32))


def _rwkv_chunk_kernel(rf_ref, kf_ref, vf_ref, lwf_ref, laf_ref, rb_ref, kb_ref, vb_ref, lwb_ref, lab_ref,
                       w2f_ref, a2f_ref, w0f_ref, a0f_ref, w2b_ref, a2b_ref, w0b_ref, a0b_ref,
                       kk_ref, ka_ref, rk_ref, yf_ref, bf_ref, yb_ref, bb_ref, gf_ref, gb_ref, *, nblk):
    n = pl.program_id(2)

    @pl.when(n == 0)
    def _():
        gf_ref[...] = jnp.zeros_like(gf_ref)
        gb_ref[...] = jnp.zeros_like(gb_ref)

    c, nn, tt = RWKV_CHUNK, RWKV_N, TIME_TILE
    nch, nh = tt // c, LANES // nn
    row = lax.broadcasted_iota(jnp.int32, (tt, LANES), 0)
    ri = lax.broadcasted_iota(jnp.int32, (tt, tt), 0)
    ci = lax.broadcasted_iota(jnp.int32, (tt, tt), 1)
    same = (ri // c) == (ci // c)
    ones = _same_head_ones()
    dirs = ((False, rf_ref, kf_ref, vf_ref, lwf_ref, laf_ref, w2f_ref, a2f_ref, w0f_ref, a0f_ref,
             yf_ref, bf_ref, gf_ref),
            (True, rb_ref, kb_ref, vb_ref, lwb_ref, lab_ref, w2b_ref, a2b_ref, w0b_ref, a0b_ref,
             yb_ref, bb_ref, gb_ref))

    probs, blocks = [], []
    for rev, r_ref, k_ref, v_ref, lw_ref, la_ref, w2_ref, a2_ref, w0_ref, a0_ref, y_ref, bonus_ref, g_ref in dirs:
        incl = same & ((ci >= ri) if rev else (ci <= ri))
        strict = same & ((ci > ri) if rev else (ci < ri))
        incl_ones = _zero_one(incl)
        ends = [cc * c if rev else (cc + 1) * c - 1 for cc in range(nch)]
        lw_all = -jnp.exp(-_softplus(-(w0_ref[...] + jnp.dot(lw_ref[0], w2_ref[...], preferred_element_type=F32)))
                          - 0.5)
        a_all = jax.nn.sigmoid(a0_ref[...] + jnp.dot(la_ref[0], a2_ref[...], preferred_element_type=F32))
        for blk in range(nblk):
            ls = slice(blk * LANES, (blk + 1) * LANES)
            r, k, v, lw, a = r_ref[0, :, ls], k_ref[0, :, ls], v_ref[0, :, ls], lw_all[:, ls], a_all[:, ls]
            kkk = k * kk_ref[:, ls]
            kk = kkk * lax.rsqrt(_head_sum(kkk * kkk, ones) + 1e-6)
            kd = k * (1.0 + (a - 1.0) * ka_ref[:, ls])
            av = -kk
            bv = kk * a
            bonus_ref[0, :, ls] = (_head_sum(r * kd * rk_ref[:, ls], ones) * v).astype(bonus_ref.dtype)
            gc, = _masked_sums([incl_ones], lw)
            gends = [gc[e:e + 1] for e in ends]
            gend_tile = gends[0]
            for cc in range(1, nch):
                gend_tile = jnp.where(row >= cc * c, gends[cc], gend_tile)
            ee = jnp.exp(gend_tile - gc)
            ei = jnp.exp(-gc)
            at, kt, bt, rt = av * jnp.exp(gc - lw), kd * ei, bv * ei, r * jnp.exp(gc)
            kh, bh = kd * ee, bv * ee
            blocks.append(dict(rev=rev, y_ref=y_ref, g_ref=g_ref, blk=blk, rt=rt, v=v, kh=kh, bh=bh,
                               egend=[jnp.exp(g) for g in gends], first=len(probs)))
            for j in range(nh):
                hs = slice(j * nn, (j + 1) * nn)
                probs.append(dict(rev=rev, incl=incl, strict=strict, at=at[:, hs], kt=kt[:, hs], bt=bt[:, hs],
                                  rt=rt[:, hs], v=v[:, hs]))

    bigs = [_mm_nt(jnp.concatenate([p["at"], p["rt"]], axis=0), jnp.concatenate([p["bt"], p["kt"]], axis=0))
            for p in probs]
    for p, big in zip(probs, bigs):
        p["a_ab"] = jnp.where(p["strict"], big[:tt, :tt], 0.0)
        p["a_ak"] = jnp.where(p["strict"], big[:tt, tt:], 0.0)
        p["m_rb"] = jnp.where(p["incl"], big[tt:, :tt], 0.0)
        p["m_rk"] = jnp.where(p["incl"], big[tt:, tt:], 0.0)
    avyvs = [_mm(jnp.concatenate([p["a_ak"], p["m_rk"]], axis=0), p["v"]) for p in probs]
    t_invs = _tri_inverses([-p["a_ab"] for p in probs], [p["rev"] for p in probs], tt, c)
    tauvs = [_mm(t, jnp.concatenate([p["at"], av[:tt]], axis=1)) for t, p, av in zip(t_invs, probs, avyvs)]
    mtmus = [_mm(p["m_rb"], tauv) for p, tauv in zip(probs, tauvs)]

    for blk in blocks:
        hd = range(blk["first"], blk["first"] + nh)
        blk["ta"] = jnp.concatenate([tauvs[i][:, :nn] for i in hd], axis=1)
        blk["uv"] = jnp.concatenate([tauvs[i][:, nn:] for i in hd], axis=1)
        blk["rq"] = blk["rt"] + jnp.concatenate([mtmus[i][:, :nn] for i in hd], axis=1)
        blk["yc"] = jnp.concatenate([avyvs[i][tt:] + mtmus[i][:, nn:] for i in hd], axis=1)

    def rows(cc):
        return slice(cc * c, (cc + 1) * c)

    rl = lax.broadcasted_iota(jnp.int32, (LANES, LANES), 0)
    cl = lax.broadcasted_iota(jnp.int32, (LANES, LANES), 1)
    same_head = (rl // nn) == (cl // nn)
    mps = [[_mm_tn(blk["ta"][rows(cc)], blk["bh"][rows(cc)]) for cc in range(nch)] for blk in blocks]
    n2s = [[_mm_tn(jnp.concatenate([blk["v"][rows(cc)], blk["uv"][rows(cc)]], axis=0),
                   jnp.concatenate([blk["kh"][rows(cc)], blk["bh"][rows(cc)]], axis=0)) for cc in range(nch)]
           for blk in blocks]
    states = [blk["g_ref"][blk["blk"]] for blk in blocks]
    ys = [[None] * nch for _ in blocks]
    for step in range(nch):
        ccs = [nch - 1 - step if blk["rev"] else step for blk in blocks]
        yparts = [_mm_nt(blk["rq"][rows(cc)], g) for blk, g, cc in zip(blocks, states, ccs)]
        gms = [_mm(g, jnp.where(same_head, mp[cc], 0.0)) for g, mp, cc in zip(states, mps, ccs)]
        for i, (blk, cc) in enumerate(zip(blocks, ccs)):
            ys[i][cc] = yparts[i] + blk["yc"][rows(cc)]
            states[i] = states[i] * blk["egend"][cc] + gms[i] + jnp.where(same_head, n2s[i][cc], 0.0)
    for blk, g, y in zip(blocks, states, ys):
        blk["g_ref"][blk["blk"]] = g
        blk["y_ref"][0, :, blk["blk"] * LANES:(blk["blk"] + 1) * LANES] = (
            jnp.concatenate(y, axis=0).astype(blk["y_ref"].dtype))


def _rwkv_chunk(rkv, lora, w, dm):
    b, t, _ = rkv.shape
    assert t % TIME_TILE == 0 and dm % LANES == 0 and LANES % RWKV_N == 0 and TIME_TILE % RWKV_CHUNK == 0
    nblk = math.gcd(RWKV_BLOCKS, dm // LANES)
    nt = t // TIME_TILE
    width = nblk * LANES
    nw = dm // width
    seg_blocks = w["lora_seg"] // LANES
    fwd = lambda n: n
    bwd = lambda n: nt - 1 - n

    def tiles(order, d):
        rkv_specs = [pl.BlockSpec((1, TIME_TILE, width), lambda b_, h, n, s=s: (b_, order(n), s * nw + h))
                     for s in range(3)]
        lora_specs = [pl.BlockSpec((1, TIME_TILE, LANES), lambda b_, h, n, s=s: (b_, order(n), s * seg_blocks + d))
                      for s in range(2)]
        return rkv_specs + lora_specs

    def out_tile(order):
        return pl.BlockSpec((1, TIME_TILE, width), lambda b_, h, n: (b_, order(n), h))

    wblk = pl.BlockSpec((LANES, width), lambda b_, h, n: (0, h))
    row = pl.BlockSpec((1, width), lambda b_, h, n: (0, h))
    shp = jax.ShapeDtypeStruct((b, t, dm), BF16)
    dir_weights = [z for d in range(2) for z in (w["w2"][d], w["a2"][d], w["w0"][d:d + 1], w["a0"][d:d + 1])]
    return pl.pallas_call(
        functools.partial(_rwkv_chunk_kernel, nblk=nblk),
        grid=(b, nw, nt),
        in_specs=tiles(fwd, 0) + tiles(bwd, 1) + [wblk, wblk, row, row] * 2 + [row] * 3,
        out_specs=[out_tile(fwd), out_tile(fwd), out_tile(bwd), out_tile(bwd)],
        out_shape=[shp] * 4,
        scratch_shapes=[pltpu.VMEM((nblk, LANES, LANES), F32), pltpu.VMEM((nblk, LANES, LANES), F32)],
        compiler_params=_cparams(("parallel", "parallel", "arbitrary")),
        name="rwkv_chunk",
    )(rkv, rkv, rkv, lora, lora, rkv, rkv, rkv, lora, lora, *dir_weights, w["k_k"], w["k_a"], w["r_k"])


def _rwkv_out_prologue(y_f, y_b, bn_f, bn_b, g, ln_w, ln_b):
    parts = []
    ones = _same_head_ones()
    for hh in range(y_f.shape[1] // LANES):
        sl = slice(hh * LANES, (hh + 1) * LANES)
        wkv = y_f[:, sl].astype(F32) + y_b[:, sl].astype(F32)
        mu = _head_sum(wkv, ones) * (1.0 / RWKV_N)
        wc = wkv - mu
        var = _head_sum(wc * wc, ones) * (1.0 / RWKV_N)
        o = (wc * lax.rsqrt(var + RWKV_GN_EPS) * ln_w[:, sl] + ln_b[:, sl]
             + (bn_f[:, sl].astype(F32) + bn_b[:, sl].astype(F32)))
        parts.append((o * g[:, sl]).astype(BF16))
    return jnp.concatenate(parts, axis=1) if len(parts) > 1 else parts[0]


def _rwkv_mixer_ln(x, w, alpha, ln_g, ln_b):
    b, t, dm = x.shape
    m = b * t
    x2 = x.reshape(m, dm)
    xx = _token_shift(x).reshape(m, dm)
    seg = w["lora_seg"]
    tn = min(1024, dm)
    rkv, = _fused_matmul([x2, xx], w["w_rkv"], rowk_ins=[w["mix_rkv"]], rowk_seg_tiles=dm // tn,
                         prologue=_mix_prologue, tm=1024, tn=tn, tk=dm, name="rwkv_rkv")

    def lora_epilogue(acc):
        j = pl.program_id(1)
        return (jnp.where(j == 0, jnp.tanh(acc), jnp.where(j == 2, jax.nn.sigmoid(acc), acc)),)

    lora, = _fused_matmul([x2, xx], w["w_lora"], rowk_ins=[w["mix_lora"]], rowk_seg_tiles=1,
                          prologue=_mix_prologue, epilogue=lora_epilogue, out_dtypes=(BF16,),
                          tm=1024, tn=seg, tk=dm, name="rwkv_lora")
    g, = _fused_matmul([lora], w["g2"], a_col0=[2 * seg], tm=1024, tn=1024, tk=seg, name="rwkv_g2")

    y_f, bn_f, y_b, bn_b = _rwkv_chunk(rkv.reshape(b, t, 3 * dm), lora.reshape(b, t, 3 * seg), w, dm)
    out, = _fused_matmul(
        [z.reshape(m, dm) for z in (y_f, y_b, bn_f, bn_b)] + [g], w["w_o"],
        rowk_ins=[w["ln_w"], w["ln_b"]], e_ins=[x2], rown_ins=[ln_g, ln_b],
        prologue=_rwkv_out_prologue, epilogue=_ln_epilogue(alpha), tm=512, tn=dm, tk=1024, name="rwkv_out")
    return out


def _relu2_epilogue(acc):
    return (jnp.square(jnp.maximum(acc, 0.0)),)


def _mlp_ln(x2, w_up, w_down, alpha, ln_g, ln_b):
    dm = x2.shape[1]
    h, = _fused_matmul([x2], w_up, epilogue=_relu2_epilogue, out_dtypes=(BF16,), tm=1024, tn=1024, tk=dm,
                       name="mlp_up")
    out, = _fused_matmul([h], w_down, e_ins=[x2], rown_ins=[ln_g, ln_b], epilogue=_ln_epilogue(alpha),
                         tm=512, tn=dm, tk=2048, name="mlp_down")
    return out


def _ple_epilogue(acc, x, p, w_proj):
    return (x + jax.nn.sigmoid(acc) * jnp.dot(p.astype(BF16), w_proj, preferred_element_type=F32),)


def _ple(x2, p_all, layer, w_proj, w_gate, tm=1024):
    m, dm = x2.shape
    pdim = p_all.shape[1]
    tm = min(tm, m)
    row0 = layer * m // tm
    extra = [(p_all, lambda tm_, tn: (tm_, pdim), lambda i, j, k: (row0 + i, 0)),
             (w_proj, lambda tm_, tn: (pdim, tn), lambda i, j, k: (0, j))]
    out, = _fused_matmul([x2], w_gate, e_ins=[x2], extra_ins=extra, epilogue=_ple_epilogue,
                         tm=tm, tn=1024, tk=dm, name="ple_gate")
    return out


def _pad_rows(w, rows):
    return jnp.pad(w, ((0, rows - w.shape[0]), (0, 0)))


def _pad_cols(w, cols):
    return jnp.pad(w, ((0, 0), (0, cols - w.shape[1])))


def _gdn_weights(w_in, conv, a_log, dt_bias, norm, w_out):
    hv, dv = a_log.shape[-1], norm.shape[-1]
    vd = hv * dv
    kd = (conv.shape[-1] - vd) // 2
    dk = dv
    assert w_in.shape[1] == 2 * kd + 2 * vd + 4 * hv and w_out.shape[0] == vd
    nz = 2 * kd + 2 * vd
    return dict(kd=kd, vd=vd, hk=kd // dk, hv=hv, dk=dk, dv=dv,
                w_qkvz=w_in[:, :nz].astype(BF16), w_gates=w_in[:, nz:].astype(BF16), conv=conv.astype(F32),
                a_log=a_log, dt_bias=dt_bias, norm_row=jnp.tile(norm.astype(F32), hv).reshape(1, vd),
                w_out=w_out.astype(BF16))


def _rwkv_weights(mix, w_rkv, w0, w1, w2, a0, a1, a2, g1, g2, k_k, k_a, r_k, ln_w, ln_b, w_o):
    dm = w_o.shape[0]
    assert w1.shape[-1] <= LANES and a1.shape[-1] <= LANES and r_k.size == dm and r_k.shape[-1] == RWKV_N
    seg = max(2 * LANES, -(-g1.shape[1] // LANES) * LANES)
    cat = lambda ws: _pad_cols(jnp.concatenate([_pad_cols(ws[d], LANES) for d in range(2)], axis=1), seg)
    row = lambda z: z.reshape(1, dm).astype(F32)
    mix = mix.astype(F32)
    return dict(lora_seg=seg,
                mix_rkv=jnp.stack([mix[0], mix[2], mix[3]]).reshape(3, 1, dm),
                mix_lora=jnp.stack([mix[1], mix[4], mix[5]]).reshape(3, 1, dm),
                w_rkv=jnp.concatenate([w_rkv[0], w_rkv[1], w_rkv[2]], axis=1).astype(BF16),
                w_lora=jnp.concatenate([cat(w1), cat(a1), _pad_cols(g1, seg)], axis=1).astype(BF16),
                w2=[_pad_rows(w2[d], LANES).astype(BF16) for d in range(2)],
                a2=[_pad_rows(a2[d], LANES).astype(BF16) for d in range(2)],
                g2=_pad_rows(g2, seg).astype(BF16), w0=w0.astype(F32), a0=a0.astype(F32),
                k_k=row(k_k), k_a=row(k_a), r_k=row(r_k), ln_w=row(ln_w), ln_b=row(ln_b), w_o=w_o.astype(BF16))


def _trunk(x, p, layers, alpha):
    b, t, dm = x.shape
    p_all = p.reshape(-1, p.shape[-1])
    for i, lyr in enumerate(layers):
        mixer = _gdn_mixer_ln if lyr["kind"] == "gdn" else _rwkv_mixer_ln
        x2 = mixer(x, lyr["mixer"], alpha, lyr["ln_g"][0:1], lyr["ln_b"][0:1])
        x2 = _mlp_ln(x2, lyr["w_up"], lyr["w_down"], alpha, lyr["ln_g"][1:2], lyr["ln_b"][1:2])
        x2 = _ple(x2, p_all, i, lyr["w_proj"], lyr["w_gate"])
        x = x2.reshape(b, t, dm)
    return x


def kernel(x_prompt, x_sample, p_prompt, p_sample, gdn_w_in, gdn_conv, gdn_a_log, gdn_dt_bias, gdn_norm, gdn_w_out, rwkv_mix, rwkv_w_rkv, rwkv_w0, rwkv_w1, rwkv_w2, rwkv_a0, rwkv_a1, rwkv_a2, rwkv_g1, rwkv_g2, rwkv_k_k, rwkv_k_a, rwkv_r_k, rwkv_ln_w, rwkv_ln_b, rwkv_w_o, ln_g, ln_b, mlp_w_up, mlp_w_down, ple_w_proj, ple_w_gate):
    depth = ln_g.shape[0]
    alpha = (2 * depth) ** 0.25
    layers = []
    for i in range(depth):
        j = i // 2
        if i % 2 == 0:
            kind = "gdn"
            mixer = _gdn_weights(gdn_w_in[j], gdn_conv[j], gdn_a_log[j], gdn_dt_bias[j], gdn_norm[j], gdn_w_out[j])
        else:
            kind = "rwkv"
            mixer = _rwkv_weights(rwkv_mix[j], rwkv_w_rkv[j], rwkv_w0[j], rwkv_w1[j], rwkv_w2[j], rwkv_a0[j],
                                  rwkv_a1[j], rwkv_a2[j], rwkv_g1[j], rwkv_g2[j], rwkv_k_k[j], rwkv_k_a[j],
                                  rwkv_r_k[j], rwkv_ln_w[j], rwkv_ln_b[j], rwkv_w_o[j])
        layers.append(dict(kind=kind, mixer=mixer, ln_g=ln_g[i].astype(F32), ln_b=ln_b[i].astype(F32),
                           w_up=mlp_w_up[i].astype(BF16), w_down=mlp_w_down[i].astype(BF16),
                           w_proj=ple_w_proj[i].astype(BF16), w_gate=ple_w_gate[i].astype(BF16)))
    y_prompt = _trunk(x_prompt, p_prompt, layers, alpha)
    y_sample = _trunk(x_sample, p_sample, layers, alpha)
    return (y_prompt, y_sample)
```

```python
import functools
import math

import jax
import jax.numpy as jnp
from jax import lax
from jax.experimental import pallas as pl
from jax.experimental.pallas import tpu as pltpu

F32 = jnp.float32
BF16 = jnp.bfloat16

LANES = 128
HALO = 8
TIME_TILE = 128
GDN_CHUNK = TIME_TILE
RWKV_CHUNK = 64
GDN_PAIRS = 8
RWKV_BLOCKS = 8
INV_BASE = 8
VMEM_LIMIT = 56 * 1024 * 1024

LN_EPS = 1e-5
RWKV_GN_EPS = 64e-5
RWKV_N = 64
GDN_CONV = 5


def _cparams(sem):
    return pltpu.CompilerParams(dimension_semantics=sem, vmem_limit_bytes=VMEM_LIMIT)


def _mm_kernel(*refs, na, nrk, ne, nrn, nout, nk, prologue, epilogue):
    a_refs = refs[:na]
    rk_refs = refs[na:na + nrk]
    w_ref = refs[na + nrk]
    e_refs = refs[na + nrk + 1:na + nrk + 1 + ne]
    rn_refs = refs[na + nrk + 1 + ne:na + nrk + 1 + ne + nrn]
    out_refs = refs[na + nrk + 1 + ne + nrn:na + nrk + 1 + ne + nrn + nout]

    a = prologue(*[r[...] for r in a_refs], *[r[...] for r in rk_refs])
    part = jnp.dot(a, w_ref[...], preferred_element_type=F32)

    def finish(acc):
        outs = epilogue(acc, *[r[...] for r in e_refs], *[r[...] for r in rn_refs])
        for o_ref, o in zip(out_refs, outs):
            o_ref[...] = o.astype(o_ref.dtype)

    if nk == 1:
        finish(part)
        return

    acc_ref = refs[-1]
    k = pl.program_id(2)

    @pl.when(k == 0)
    def _():
        acc_ref[...] = part

    @pl.when(k > 0)
    def _():
        acc_ref[...] += part

    @pl.when(k == nk - 1)
    def _():
        finish(acc_ref[...])


def _cast_bf16(a):
    return a.astype(BF16)


def _identity_epilogue(acc):
    return (acc,)


def _fused_matmul(a_ins, w, *, a_col0=None, rowk_ins=(), rowk_seg_tiles=None, e_ins=(), rown_ins=(),
                  extra_ins=(), prologue=_cast_bf16, epilogue=_identity_epilogue, out_dtypes=(F32,),
                  tm=512, tn=512, tk=512, kdim=None, w_row0=0, name="mm"):
    m = a_ins[0].shape[0]
    n = w.shape[1]
    kdim = w.shape[0] if kdim is None else kdim
    tm, tn, tk = min(tm, m), min(tn, n), min(tk, kdim)
    assert m % tm == 0 and n % tn == 0 and kdim % tk == 0 and w_row0 % tk == 0, (m, n, kdim, tm, tn, tk, w_row0)
    nk = kdim // tk
    w_blk0 = w_row0 // tk
    a_col0 = [0] * len(a_ins) if a_col0 is None else a_col0
    assert all(c0 % tk == 0 for c0 in a_col0)

    def a_spec(c0):
        return pl.BlockSpec((tm, tk), lambda i, j, k: (i, c0 // tk + k))

    if rowk_seg_tiles is None:
        rowk_spec = pl.BlockSpec((1, tk), lambda i, j, k: (0, k))
    else:
        rowk_spec = pl.BlockSpec((None, 1, tk), lambda i, j, k: (j // rowk_seg_tiles, 0, k))
    in_specs = ([a_spec(c0) for c0 in a_col0]
                + [rowk_spec for _ in rowk_ins]
                + [pl.BlockSpec((tk, tn), lambda i, j, k: (w_blk0 + k, j))]
                + [pl.BlockSpec((tm, tn), lambda i, j, k: (i, j)) for _ in e_ins]
                + [pl.BlockSpec((1, tn), lambda i, j, k: (0, j)) for _ in rown_ins]
                + [pl.BlockSpec(shape(tm, tn), imap) for _, shape, imap in extra_ins])
    out_specs = [pl.BlockSpec((tm, tn), lambda i, j, k: (i, j)) for _ in out_dtypes]
    out_shape = [jax.ShapeDtypeStruct((m, n), dt) for dt in out_dtypes]
    kern = functools.partial(_mm_kernel, na=len(a_ins), nrk=len(rowk_ins), ne=len(e_ins),
                             nrn=len(rown_ins) + len(extra_ins), nout=len(out_dtypes), nk=nk,
                             prologue=prologue, epilogue=epilogue)
    outs = pl.pallas_call(
        kern,
        grid=(m // tm, n // tn, nk),
        in_specs=in_specs,
        out_specs=out_specs,
        out_shape=out_shape,
        scratch_shapes=[pltpu.VMEM((tm, tn), F32)] if nk > 1 else [],
        compiler_params=_cparams(("parallel", "parallel", "arbitrary")),
        name=name,
    )(*a_ins, *rowk_ins, w, *e_ins, *rown_ins, *[x for x, _, _ in extra_ins])
    return outs


def _ln_epilogue(alpha):
    def epilogue(acc, xres, g, b):
        y = alpha * xres + acc
        mu = jnp.mean(y, axis=-1, keepdims=True)
        yc = y - mu
        var = jnp.mean(yc * yc, axis=-1, keepdims=True)
        return (yc * lax.rsqrt(var + LN_EPS) * g + b,)
    return epilogue


def _halo_specs(tt, tc, t_total):
    r = tt // HALO
    last = t_total // HALO - 1
    main = pl.BlockSpec((1, tt, tc), lambda b, i, c: (b, i, c))
    prev = pl.BlockSpec((1, HALO, tc), lambda b, i, c: (b, jnp.maximum(i * r - 1, 0), c))
    nxt = pl.BlockSpec((1, HALO, tc), lambda b, i, c: (b, jnp.minimum((i + 1) * r, last), c))
    return main, prev, nxt


def _fill_halo(scr, x_ref, xp_ref, xn_ref, tt, nt):
    i = pl.program_id(1)
    scr[0:HALO, :] = jnp.where(i > 0, xp_ref[0], 0.0)
    scr[HALO:HALO + tt, :] = x_ref[0]
    scr[HALO + tt:2 * HALO + tt, :] = jnp.where(i < nt - 1, xn_ref[0], 0.0)


def _gdn_conv_kernel(x_ref, xp_ref, xn_ref, w_ref, o_ref, scr, *, tt, tc, nt, nq_tiles, nqk_tiles, q_scale):
    _fill_halo(scr, x_ref, xp_ref, xn_ref, tt, nt)
    pad = (GDN_CONV - 1) // 2
    acc = None
    for j in range(GDN_CONV):
        term = w_ref[j:j + 1, :] * scr[pl.ds(HALO - pad + j, tt), :]
        acc = term if acc is None else acc + term
    y = acc * jax.nn.sigmoid(acc)
    c = pl.program_id(2)
    is_qk = c < nqk_tiles
    scale = jnp.where(c < nq_tiles, q_scale, 1.0).astype(F32)
    for hh in range(tc // LANES):
        blk = y[:, hh * LANES:(hh + 1) * LANES]
        ss = jnp.sum(blk * blk, axis=-1, keepdims=True)
        nrm = blk * (lax.rsqrt(ss + 1e-6) * scale)
        o_ref[0, :, hh * LANES:(hh + 1) * LANES] = jnp.where(is_qk, nrm, blk)


def _gdn_conv(qkv, conv_w, kd, vd, dk, tt=512, tc=2048):
    b, t, _ = qkv.shape
    ch = 2 * kd + vd
    tt = min(tt, t)
    assert t % tt == 0 and ch % tc == 0 and kd % tc == 0 and dk == LANES
    main, prev, nxt = _halo_specs(tt, tc, t)
    kern = functools.partial(_gdn_conv_kernel, tt=tt, tc=tc, nt=t // tt, nq_tiles=kd // tc,
                             nqk_tiles=2 * kd // tc, q_scale=dk ** -0.5)
    return pl.pallas_call(
        kern,
        grid=(b, t // tt, ch // tc),
        in_specs=[main, prev, nxt, pl.BlockSpec((GDN_CONV, tc), lambda b_, i, c: (0, c))],
        out_specs=pl.BlockSpec((1, tt, tc), lambda b_, i, c: (b_, i, c)),
        out_shape=jax.ShapeDtypeStruct((b, t, ch), F32),
        scratch_shapes=[pltpu.VMEM((tt + 2 * HALO, tc), F32)],
        compiler_params=_cparams(("parallel", "parallel", "parallel")),
        name="gdn_conv",
    )(qkv, qkv, qkv, conv_w)


def _chunk_tri_masks(rows, chunk):
    ri = lax.broadcasted_iota(jnp.int32, (rows, rows), 0)
    ci = lax.broadcasted_iota(jnp.int32, (rows, rows), 1)
    same = (ri // chunk) == (ci // chunk)
    return same & (ci <= ri), same & (ci >= ri)


def _softplus(x):
    return jnp.maximum(x, 0.0) + jnp.log1p(jnp.exp(-jnp.abs(x)))


def _zero_one(mask):
    return jnp.where(mask, 1.0, 0.0).astype(BF16)


def _masked_sums(zero_ones, x):
    hi = x.astype(BF16)
    rest = x - hi.astype(F32)
    mid = rest.astype(BF16)
    terms = (hi, mid, (rest - mid.astype(F32)).astype(BF16))
    outs = []
    for m in zero_ones:
        parts = [jnp.dot(m, t, preferred_element_type=F32) for t in terms]
        outs.append(parts[0] + parts[1] + parts[2])
    return outs


def _gdn_gates_kernel(g_ref, alog_ref, dtb_ref, o_ref, ot_ref, *, hv):
    x = g_ref[0]
    g = -jnp.exp(alog_ref[...]) * _softplus(x + dtb_ref[...])
    gc_f, gc_b = _masked_sums([_zero_one(m) for m in _chunk_tri_masks(TIME_TILE, GDN_CHUNK)], g)
    lane = lax.broadcasted_iota(jnp.int32, x.shape, 1)
    out = jnp.where(lane < hv, gc_f, jnp.where(lane < 2 * hv, gc_b, jax.nn.sigmoid(x)))
    o_ref[0] = out
    ot_ref[0] = out.T


def _gdn_gates(gates, a_log, dt_bias, hv):
    b, t, ch = gates.shape
    assert ch == LANES == 4 * hv and t % TIME_TILE == 0
    pad = jnp.zeros((1, 2 * hv), F32)
    alog = jnp.concatenate([a_log.reshape(1, 2 * hv).astype(F32), pad], axis=1)
    dtb = jnp.concatenate([dt_bias.reshape(1, 2 * hv).astype(F32), pad], axis=1)
    row = pl.BlockSpec((1, LANES), lambda b_, i: (0, 0))
    return pl.pallas_call(
        functools.partial(_gdn_gates_kernel, hv=hv),
        grid=(b, t // TIME_TILE),
        in_specs=[pl.BlockSpec((1, TIME_TILE, LANES), lambda b_, i: (b_, i, 0)), row, row],
        out_specs=[pl.BlockSpec((1, TIME_TILE, LANES), lambda b_, i: (b_, i, 0)),
                   pl.BlockSpec((1, LANES, TIME_TILE), lambda b_, i: (b_, 0, i))],
        out_shape=[jax.ShapeDtypeStruct((b, t, LANES), F32), jax.ShapeDtypeStruct((b, LANES, t), F32)],
        compiler_params=_cparams(("parallel", "parallel")),
        name="gdn_gates",
    )(gates, alog, dtb)


def _mm(a, b):
    return jnp.dot(a.astype(BF16), b.astype(BF16), preferred_element_type=F32)


def _mm_nt(a, b):
    return lax.dot_general(a.astype(BF16), b.astype(BF16), (((1,), (1,)), ((), ())), preferred_element_type=F32)


def _mm_tn(a, b):
    return lax.dot_general(a.astype(BF16), b.astype(BF16), (((0,), (0,)), ((), ())), preferred_element_type=F32)


def _half_rows(x, s, half):
    return jnp.concatenate([x[(2 * b + half) * s:(2 * b + half + 1) * s] for b in range(x.shape[0] // (2 * s))],
                           axis=0)


def _with_half_rows(x, new, s, half):
    parts = []
    for b in range(x.shape[0] // (2 * s)):
        keep = x[(2 * b + 1 - half) * s:(2 * b + 2 - half) * s]
        repl = new[b * s:(b + 1) * s]
        parts += [keep, repl] if half else [repl, keep]
    return jnp.concatenate(parts, axis=0)


def _tri_inverses(ms, uppers, size, top):
    ri = lax.broadcasted_iota(jnp.int32, (size, size), 0)
    ci = lax.broadcasted_iota(jnp.int32, (size, size), 1)
    eye = (ri == ci).astype(F32)
    zeros = jnp.zeros((size, size), F32)
    base_bits = int(math.log2(INV_BASE))
    base = (ri >> base_bits) == (ci >> base_bits)
    xps = [jnp.where(base, -m, 0.0) for m in ms]
    ps = [eye + x for x in xps]
    xps = [_mm(xp, xp) for xp in xps]
    for it in range(base_bits - 1):
        if it < base_bits - 2:
            both = [_mm(jnp.concatenate([p, xp], axis=0), xp) for p, xp in zip(ps, xps)]
            ps = [p + b[:size] for p, b in zip(ps, both)]
            xps = [b[size:] for b in both]
        else:
            ps = [p + _mm(p, xp) for p, xp in zip(ps, xps)]
    halves = [0 if up else 1 for up in uppers]
    bits = base_bits
    while (1 << bits) < top:
        s = 1 << bits
        pair = ((ri >> (bits + 1)) == (ci >> (bits + 1))) & ((ri >> bits) != (ci >> bits))
        qs = [_mm(_half_rows(jnp.where(pair, m, 0.0), s, h), p) for m, p, h in zip(ms, ps, halves)]
        rs = [_mm(_half_rows(p, s, h), _with_half_rows(zeros, q, s, h)) for p, q, h in zip(ps, qs, halves)]
        ps = [_with_half_rows(p, _half_rows(p, s, h) - r, s, h) for p, r, h in zip(ps, rs, halves)]
        bits += 1
    return ps


def _pick_lane(tile, lane_idx):
    lane = lax.broadcasted_iota(jnp.int32, tile.shape, 1)
    return jnp.sum(jnp.where(lane == lane_idx, tile, 0.0), axis=1, keepdims=True)


def _gdn_chunk_kernel(qf_ref, kf_ref, vf_ref, gf_ref, gtf_ref, qb_ref, kb_ref, vb_ref, gb_ref, gtb_ref,
                      of_ref, ob_ref, sf_ref, sb_ref, *, rep, hv, npairs):
    n = pl.program_id(2)
    hg = pl.program_id(1)

    @pl.when(n == 0)
    def _():
        sf_ref[...] = jnp.zeros_like(sf_ref)
        sb_ref[...] = jnp.zeros_like(sb_ref)

    c = GDN_CHUNK
    ri = lax.broadcasted_iota(jnp.int32, (c, c), 0)
    ci = lax.broadcasted_iota(jnp.int32, (c, c), 1)
    dirs = ((False, qf_ref, kf_ref, vf_ref, gf_ref, gtf_ref, of_ref, sf_ref),
            (True, qb_ref, kb_ref, vb_ref, gb_ref, gtb_ref, ob_ref, sb_ref))

    pairs = []
    for rev, q_ref, k_ref, *_ in dirs:
        for i in range(npairs):
            pairs.append((q_ref[0, :, i * LANES:(i + 1) * LANES], k_ref[0, :, i * LANES:(i + 1) * LANES]))
    qks = [_mm_nt(q, k) for q, k in pairs]
    kks = [_mm_nt(k, k) for _, k in pairs]

    probs = []
    for d, (rev, q_ref, k_ref, v_ref, g_ref, gt_ref, o_ref, s_ref) in enumerate(dirs):
        incl = (ci >= ri) if rev else (ci <= ri)
        strict = (ci > ri) if rev else (ci < ri)
        g_tile = g_ref[0]
        end = 0 if rev else c - 1
        for i in range(npairs):
            q, k = pairs[d * npairs + i]
            for r in range(rep):
                slot = i * rep + r
                h = (hg * npairs + i) * rep + r
                gcol = _pick_lane(g_tile, d * hv + h)
                bcol = _pick_lane(g_tile, 2 * hv + d * hv + h)
                grow = gt_ref[0, pl.ds(d * hv + h, 1), :]
                decay = jnp.where(incl, jnp.exp(jnp.where(incl, gcol - grow, 0.0)), 0.0)
                g_last = gcol[end:end + 1]
                egc = jnp.exp(gcol)
                v = v_ref[0, :, slot * LANES:(slot + 1) * LANES]
                probs.append(dict(
                    q=q, k=k, egc=egc, g_last=g_last, slot=slot, o_ref=o_ref, s_ref=s_ref, rev=rev,
                    a_kk=jnp.where(strict, kks[d * npairs + i] * bcol * decay, 0.0),
                    a_qk=qks[d * npairs + i] * decay,
                    rhs=jnp.concatenate([v * bcol, k * (bcol * egc)], axis=1),
                    k_dec=k * jnp.exp(g_last - gcol)))

    t_invs = _tri_inverses([p["a_kk"] for p in probs], [p["rev"] for p in probs], c, c)
    uws = [_mm(t, p["rhs"]) for t, p in zip(t_invs, probs)]
    aunms = [_mm(jnp.concatenate([p["a_qk"], p["k_dec"].T], axis=0), uw) for p, uw in zip(probs, uws)]
    aus = [x[:c] for x in aunms]
    nms = [x[c:] for x in aunms]

    states = [p["s_ref"][p["slot"]] for p in probs]
    lhss = [jnp.concatenate([p["q"] * p["egc"] - au[:, LANES:], nm[:, LANES:]], axis=0)
            for p, au, nm in zip(probs, aus, nms)]
    ress = [_mm(lhs, s) for lhs, s in zip(lhss, states)]
    for p, s, au, nm, res in zip(probs, states, aus, nms, ress):
        slot = p["slot"]
        p["o_ref"][0, :, slot * LANES:(slot + 1) * LANES] = (res[:c] + au[:, :LANES]).astype(p["o_ref"].dtype)
        p["s_ref"][slot] = s * jnp.exp(p["g_last"]) - res[c:] + nm[:, :LANES]


def _gdn_chunk(qkv, gcb, gcbt, hk, hv, dk, dv):
    b, t, _ = qkv.shape
    rep = hv // hk
    npairs = math.gcd(GDN_PAIRS, hk)
    assert dk == LANES and dv == LANES and t % TIME_TILE == 0 and GDN_CHUNK == TIME_TILE
    assert (2 * hk) % (npairs * rep) == 0
    nt = t // TIME_TILE
    qw, vw = npairs * LANES, npairs * rep * LANES
    kblk0 = hk // npairs
    vblk0 = 2 * hk // (npairs * rep)

    def specs(tile):
        return [pl.BlockSpec((1, TIME_TILE, qw), lambda b_, h, n: (b_, tile(n), h)),
                pl.BlockSpec((1, TIME_TILE, qw), lambda b_, h, n: (b_, tile(n), kblk0 + h)),
                pl.BlockSpec((1, TIME_TILE, vw), lambda b_, h, n: (b_, tile(n), vblk0 + h)),
                pl.BlockSpec((1, TIME_TILE, LANES), lambda b_, h, n: (b_, tile(n), 0)),
                pl.BlockSpec((1, LANES, TIME_TILE), lambda b_, h, n: (b_, 0, tile(n)))]

    fwd = lambda n: n
    bwd = lambda n: nt - 1 - n
    out_f = pl.BlockSpec((1, TIME_TILE, vw), lambda b_, h, n: (b_, n, h))
    out_b = pl.BlockSpec((1, TIME_TILE, vw), lambda b_, h, n: (b_, nt - 1 - n, h))
    shp = jax.ShapeDtypeStruct((b, t, hv * dv), BF16)
    return pl.pallas_call(
        functools.partial(_gdn_chunk_kernel, rep=rep, hv=hv, npairs=npairs),
        grid=(b, hk // npairs, nt),
        in_specs=specs(fwd) + specs(bwd),
        out_specs=[out_f, out_b],
        out_shape=[shp, shp],
        scratch_shapes=[pltpu.VMEM((npairs * rep, dk, dv), F32), pltpu.VMEM((npairs * rep, dk, dv), F32)],
        compiler_params=_cparams(("parallel", "parallel", "arbitrary")),
        name="gdn_chunk",
    )(qkv, qkv, qkv, gcb, gcbt, qkv, qkv, qkv, gcb, gcbt)


def _gdn_out_prologue(o_f, o_b, z, nw):
    parts = []
    for hh in range(o_f.shape[1] // LANES):
        sl = slice(hh * LANES, (hh + 1) * LANES)
        o = o_f[:, sl].astype(F32) + o_b[:, sl].astype(F32)
        o = o * lax.rsqrt(jnp.mean(o * o, axis=-1, keepdims=True) + 1e-6) * nw[:, sl]
        zz = z[:, sl]
        parts.append((o * (zz * jax.nn.sigmoid(zz))).astype(BF16))
    return jnp.concatenate(parts, axis=1) if len(parts) > 1 else parts[0]


def _gdn_mixer_ln(x, w, alpha, ln_g, ln_b):
    b, t, dm = x.shape
    m = b * t
    x2 = x.reshape(m, dm)
    kd, vd, hk, hv, dk, dv = w["kd"], w["vd"], w["hk"], w["hv"], w["dk"], w["dv"]
    qkvz, = _fused_matmul([x2], w["w_qkvz"], tm=1024, tn=1024, tk=dm, name="gdn_in")
    gates, = _fused_matmul([x2], w["w_gates"], tm=1024, tn=LANES, tk=dm, name="gdn_in_gates")
    qkv = _gdn_conv(qkvz.reshape(b, t, -1), w["conv"], kd, vd, dk)
    gcb, gcbt = _gdn_gates(gates.reshape(b, t, -1), w["a_log"], w["dt_bias"], hv)
    o_f, o_b = _gdn_chunk(qkv, gcb, gcbt, hk, hv, dk, dv)
    out, = _fused_matmul(
        [o_f.reshape(m, vd), o_b.reshape(m, vd), qkvz], w["w_out"], a_col0=[0, 0, 2 * kd + vd],
        rowk_ins=[w["norm_row"]], e_ins=[x2], rown_ins=[ln_g, ln_b],
        prologue=_gdn_out_prologue, epilogue=_ln_epilogue(alpha), tm=512, tn=dm, tk=1024, name="gdn_out")
    return out


def _shift_kernel(x_ref, xp_ref, xn_ref, o_ref, scr, *, tt, nt):
    _fill_halo(scr, x_ref, xp_ref, xn_ref, tt, nt)
    o_ref[0] = 0.5 * (scr[pl.ds(HALO - 1, tt), :] + scr[pl.ds(HALO + 1, tt), :]) - x_ref[0]


def _token_shift(x, tt=512, tc=1024):
    b, t, ch = x.shape
    tt, tc = min(tt, t), min(tc, ch)
    assert t % tt == 0 and ch % tc == 0
    main, prev, nxt = _halo_specs(tt, tc, t)
    return pl.pallas_call(
        functools.partial(_shift_kernel, tt=tt, nt=t // tt),
        grid=(b, t // tt, ch // tc),
        in_specs=[main, prev, nxt],
        out_specs=pl.BlockSpec((1, tt, tc), lambda b_, i, c: (b_, i, c)),
        out_shape=jax.ShapeDtypeStruct((b, t, ch), F32),
        scratch_shapes=[pltpu.VMEM((tt + 2 * HALO, tc), F32)],
        compiler_params=_cparams(("parallel", "parallel", "parallel")),
        name="rwkv_shift",
    )(x, x, x)


def _mix_prologue(x, xx, mix):
    return (x + xx * mix).astype(BF16)


def _same_head_ones():
    rl = lax.broadcasted_iota(jnp.int32, (LANES, LANES), 0)
    cl = lax.broadcasted_iota(jnp.int32, (LANES, LANES), 1)
    return jnp.where((rl // RWKV_N) == (cl // RWKV_N), 1.0, 0.0).astype(BF16)


def _head_sum(x, ones):
    hi = x.astype(BF16)
    lo = (x - hi.astype(F32)).astype(BF16)
    return (jnp.dot(hi, ones, preferred_element_type=F32) + jnp.dot(lo, ones, preferred_element_type=F32))


def _rwkv_chunk_kernel(rf_ref, kf_ref, vf_ref, lwf_ref, laf_ref, rb_ref, kb_ref, vb_ref, lwb_ref, lab_ref,
                       w2f_ref, a2f_ref, w0f_ref, a0f_ref, w2b_ref, a2b_ref, w0b_ref, a0b_ref,
                       kk_ref, ka_ref, rk_ref, yf_ref, bf_ref, yb_ref, bb_ref, gf_ref, gb_ref, *, nblk):
    n = pl.program_id(2)

    @pl.when(n == 0)
    def _():
        gf_ref[...] = jnp.zeros_like(gf_ref)
        gb_ref[...] = jnp.zeros_like(gb_ref)

    c, nn, tt = RWKV_CHUNK, RWKV_N, TIME_TILE
    nch, nh = tt // c, LANES // nn
    row = lax.broadcasted_iota(jnp.int32, (tt, LANES), 0)
    ri = lax.broadcasted_iota(jnp.int32, (tt, tt), 0)
    ci = lax.broadcasted_iota(jnp.int32, (tt, tt), 1)
    same = (ri // c) == (ci // c)
    ones = _same_head_ones()
    dirs = ((False, rf_ref, kf_ref, vf_ref, lwf_ref, laf_ref, w2f_ref, a2f_ref, w0f_ref, a0f_ref,
             yf_ref, bf_ref, gf_ref),
            (True, rb_ref, kb_ref, vb_ref, lwb_ref, lab_ref, w2b_ref, a2b_ref, w0b_ref, a0b_ref,
             yb_ref, bb_ref, gb_ref))

    probs, blocks = [], []
    for rev, r_ref, k_ref, v_ref, lw_ref, la_ref, w2_ref, a2_ref, w0_ref, a0_ref, y_ref, bonus_ref, g_ref in dirs:
        incl = same & ((ci >= ri) if rev else (ci <= ri))
        strict = same & ((ci > ri) if rev else (ci < ri))
        incl_ones = _zero_one(incl)
        ends = [cc * c if rev else (cc + 1) * c - 1 for cc in range(nch)]
        lw_all = -jnp.exp(-_softplus(-(w0_ref[...] + jnp.dot(lw_ref[0], w2_ref[...], preferred_element_type=F32)))
                          - 0.5)
        a_all = jax.nn.sigmoid(a0_ref[...] + jnp.dot(la_ref[0], a2_ref[...], preferred_element_type=F32))
        for blk in range(nblk):
            ls = slice(blk * LANES, (blk + 1) * LANES)
            r, k, v, lw, a = r_ref[0, :, ls], k_ref[0, :, ls], v_ref[0, :, ls], lw_all[:, ls], a_all[:, ls]
            kkk = k * kk_ref[:, ls]
            kk = kkk * lax.rsqrt(_head_sum(kkk * kkk, ones) + 1e-6)
            kd = k * (1.0 + (a - 1.0) * ka_ref[:, ls])
            av = -kk
            bv = kk * a
            bonus_ref[0, :, ls] = (_head_sum(r * kd * rk_ref[:, ls], ones) * v).astype(bonus_ref.dtype)
            gc, = _masked_sums([incl_ones], lw)
            gends = [gc[e:e + 1] for e in ends]
            gend_tile = gends[0]
            for cc in range(1, nch):
                gend_tile = jnp.where(row >= cc * c, gends[cc], gend_tile)
            ee = jnp.exp(gend_tile - gc)
            ei = jnp.exp(-gc)
            at, kt, bt, rt = av * jnp.exp(gc - lw), kd * ei, bv * ei, r * jnp.exp(gc)
            kh, bh = kd * ee, bv * ee
            blocks.append(dict(rev=rev, y_ref=y_ref, g_ref=g_ref, blk=blk, rt=rt, v=v, kh=kh, bh=bh,
                               egend=[jnp.exp(g) for g in gends], first=len(probs)))
            for j in range(nh):
                hs = slice(j * nn, (j + 1) * nn)
                probs.append(dict(rev=rev, incl=incl, strict=strict, at=at[:, hs], kt=kt[:, hs], bt=bt[:, hs],
                                  rt=rt[:, hs], v=v[:, hs]))

    bigs = [_mm_nt(jnp.concatenate([p["at"], p["rt"]], axis=0), jnp.concatenate([p["bt"], p["kt"]], axis=0))
            for p in probs]
    for p, big in zip(probs, bigs):
        p["a_ab"] = jnp.where(p["strict"], big[:tt, :tt], 0.0)
        p["a_ak"] = jnp.where(p["strict"], big[:tt, tt:], 0.0)
        p["m_rb"] = jnp.where(p["incl"], big[tt:, :tt], 0.0)
        p["m_rk"] = jnp.where(p["incl"], big[tt:, tt:], 0.0)
    avyvs = [_mm(jnp.concatenate([p["a_ak"], p["m_rk"]], axis=0), p["v"]) for p in probs]
    t_invs = _tri_inverses([-p["a_ab"] for p in probs], [p["rev"] for p in probs], tt, c)
    tauvs = [_mm(t, jnp.concatenate([p["at"], av[:tt]], axis=1)) for t, p, av in zip(t_invs, probs, avyvs)]
    mtmus = [_mm(p["m_rb"], tauv) for p, tauv in zip(probs, tauvs)]

    for blk in blocks:
        hd = range(blk["first"], blk["first"] + nh)
        blk["ta"] = jnp.concatenate([tauvs[i][:, :nn] for i in hd], axis=1)
        blk["uv"] = jnp.concatenate([tauvs[i][:, nn:] for i in hd], axis=1)
        blk["rq"] = blk["rt"] + jnp.concatenate([mtmus[i][:, :nn] for i in hd], axis=1)
        blk["yc"] = jnp.concatenate([avyvs[i][tt:] + mtmus[i][:, nn:] for i in hd], axis=1)

    def rows(cc):
        return slice(cc * c, (cc + 1) * c)

    rl = lax.broadcasted_iota(jnp.int32, (LANES, LANES), 0)
    cl = lax.broadcasted_iota(jnp.int32, (LANES, LANES), 1)
    same_head = (rl // nn) == (cl // nn)
    mps = [[_mm_tn(blk["ta"][rows(cc)], blk["bh"][rows(cc)]) for cc in range(nch)] for blk in blocks]
    n2s = [[_mm_tn(jnp.concatenate([blk["v"][rows(cc)], blk["uv"][rows(cc)]], axis=0),
                   jnp.concatenate([blk["kh"][rows(cc)], blk["bh"][rows(cc)]], axis=0)) for cc in range(nch)]
           for blk in blocks]
    states = [blk["g_ref"][blk["blk"]] for blk in blocks]
    ys = [[None] * nch for _ in blocks]
    for step in range(nch):
        ccs = [nch - 1 - step if blk["rev"] else step for blk in blocks]
        yparts = [_mm_nt(blk["rq"][rows(cc)], g) for blk, g, cc in zip(blocks, states, ccs)]
        gms = [_mm(g, jnp.where(same_head, mp[cc], 0.0)) for g, mp, cc in zip(states, mps, ccs)]
        for i, (blk, cc) in enumerate(zip(blocks, ccs)):
            ys[i][cc] = yparts[i] + blk["yc"][rows(cc)]
            states[i] = states[i] * blk["egend"][cc] + gms[i] + jnp.where(same_head, n2s[i][cc], 0.0)
    for blk, g, y in zip(blocks, states, ys):
        blk["g_ref"][blk["blk"]] = g
        blk["y_ref"][0, :, blk["blk"] * LANES:(blk["blk"] + 1) * LANES] = (
            jnp.concatenate(y, axis=0).astype(blk["y_ref"].dtype))


def _rwkv_chunk(rkv, lora, w, dm):
    b, t, _ = rkv.shape
    assert t % TIME_TILE == 0 and dm % LANES == 0 and LANES % RWKV_N == 0 and TIME_TILE % RWKV_CHUNK == 0
    nblk = math.gcd(RWKV_BLOCKS, dm // LANES)
    nt = t // TIME_TILE
    width = nblk * LANES
    nw = dm // width
    seg_blocks = w["lora_seg"] // LANES
    fwd = lambda n: n
    bwd = lambda n: nt - 1 - n

    def tiles(order, d):
        rkv_specs = [pl.BlockSpec((1, TIME_TILE, width), lambda b_, h, n, s=s: (b_, order(n), s * nw + h))
                     for s in range(3)]
        lora_specs = [pl.BlockSpec((1, TIME_TILE, LANES), lambda b_, h, n, s=s: (b_, order(n), s * seg_blocks + d))
                      for s in range(2)]
        return rkv_specs + lora_specs

    def out_tile(order):
        return pl.BlockSpec((1, TIME_TILE, width), lambda b_, h, n: (b_, order(n), h))

    wblk = pl.BlockSpec((LANES, width), lambda b_, h, n: (0, h))
    row = pl.BlockSpec((1, width), lambda b_, h, n: (0, h))
    shp = jax.ShapeDtypeStruct((b, t, dm), BF16)
    dir_weights = [z for d in range(2) for z in (w["w2"][d], w["a2"][d], w["w0"][d:d + 1], w["a0"][d:d + 1])]
    return pl.pallas_call(
        functools.partial(_rwkv_chunk_kernel, nblk=nblk),
        grid=(b, nw, nt),
        in_specs=tiles(fwd, 0) + tiles(bwd, 1) + [wblk, wblk, row, row] * 2 + [row] * 3,
        out_specs=[out_tile(fwd), out_tile(fwd), out_tile(bwd), out_tile(bwd)],
        out_shape=[shp] * 4,
        scratch_shapes=[pltpu.VMEM((nblk, LANES, LANES), F32), pltpu.VMEM((nblk, LANES, LANES), F32)],
        compiler_params=_cparams(("parallel", "parallel", "arbitrary")),
        name="rwkv_chunk",
    )(rkv, rkv, rkv, lora, lora, rkv, rkv, rkv, lora, lora, *dir_weights, w["k_k"], w["k_a"], w["r_k"])


def _rwkv_out_prologue(y_f, y_b, bn_f, bn_b, g, ln_w, ln_b):
    parts = []
    ones = _same_head_ones()
    for hh in range(y_f.shape[1] // LANES):
        sl = slice(hh * LANES, (hh + 1) * LANES)
        wkv = y_f[:, sl].astype(F32) + y_b[:, sl].astype(F32)
        mu = _head_sum(wkv, ones) * (1.0 / RWKV_N)
        wc = wkv - mu
        var = _head_sum(wc * wc, ones) * (1.0 / RWKV_N)
        o = (wc * lax.rsqrt(var + RWKV_GN_EPS) * ln_w[:, sl] + ln_b[:, sl]
             + (bn_f[:, sl].astype(F32) + bn_b[:, sl].astype(F32)))
        parts.append((o * g[:, sl]).astype(BF16))
    return jnp.concatenate(parts, axis=1) if len(parts) > 1 else parts[0]


def _rwkv_mixer_ln(x, w, alpha, ln_g, ln_b):
    b, t, dm = x.shape
    m = b * t
    x2 = x.reshape(m, dm)
    xx = _token_shift(x).reshape(m, dm)
    seg = w["lora_seg"]
    tn = min(1024, dm)
    rkv, = _fused_matmul([x2, xx], w["w_rkv"], rowk_ins=[w["mix_rkv"]], rowk_seg_tiles=dm // tn,
                         prologue=_mix_prologue, tm=1024, tn=tn, tk=dm, name="rwkv_rkv")

    def lora_epilogue(acc):
        j = pl.program_id(1)
        return (jnp.where(j == 0, jnp.tanh(acc), jnp.where(j == 2, jax.nn.sigmoid(acc), acc)),)

    lora, = _fused_matmul([x2, xx], w["w_lora"], rowk_ins=[w["mix_lora"]], rowk_seg_tiles=1,
                          prologue=_mix_prologue, epilogue=lora_epilogue, out_dtypes=(BF16,),
                          tm=1024, tn=seg, tk=dm, name="rwkv_lora")
    g, = _fused_matmul([lora], w["g2"], a_col0=[2 * seg], tm=1024, tn=1024, tk=seg, name="rwkv_g2")

    y_f, bn_f, y_b, bn_b = _rwkv_chunk(rkv.reshape(b, t, 3 * dm), lora.reshape(b, t, 3 * seg), w, dm)
    out, = _fused_matmul(
        [z.reshape(m, dm) for z in (y_f, y_b, bn_f, bn_b)] + [g], w["w_o"],
        rowk_ins=[w["ln_w"], w["ln_b"]], e_ins=[x2], rown_ins=[ln_g, ln_b],
        prologue=_rwkv_out_prologue, epilogue=_ln_epilogue(alpha), tm=512, tn=dm, tk=1024, name="rwkv_out")
    return out


def _relu2_epilogue(acc):
    return (jnp.square(jnp.maximum(acc, 0.0)),)


def _mlp_ln(x2, w_up, w_down, layer, alpha, ln_g, ln_b):
    dm, dff = x2.shape[1], w_up.shape[1]
    h, = _fused_matmul([x2], w_up, kdim=dm, w_row0=layer * dm, epilogue=_relu2_epilogue, out_dtypes=(BF16,),
                       tm=1024, tn=1024, tk=dm, name="mlp_up")
    out, = _fused_matmul([h], w_down, kdim=dff, w_row0=layer * dff, e_ins=[x2], rown_ins=[ln_g, ln_b],
                         epilogue=_ln_epilogue(alpha), tm=512, tn=dm, tk=2048, name="mlp_down")
    return out


def _ple_epilogue(acc, x, p, w_proj):
    return (x + jax.nn.sigmoid(acc) * jnp.dot(p.astype(BF16), w_proj, preferred_element_type=F32),)


def _ple(x2, p_all, layer, w_proj, w_gate, tm=1024):
    m, dm = x2.shape
    pdim = p_all.shape[1]
    tm = min(tm, m)
    row0 = layer * m // tm
    extra = [(p_all, lambda tm_, tn: (tm_, pdim), lambda i, j, k: (row0 + i, 0)),
             (w_proj, lambda tm_, tn: (pdim, tn), lambda i, j, k: (layer, j))]
    out, = _fused_matmul([x2], w_gate, kdim=dm, w_row0=layer * dm, e_ins=[x2], extra_ins=extra,
                         epilogue=_ple_epilogue, tm=tm, tn=1024, tk=dm, name="ple_gate")
    return out


def _pad_rows(w, rows):
    return jnp.pad(w, ((0, rows - w.shape[0]), (0, 0)))


def _pad_cols(w, cols):
    return jnp.pad(w, ((0, 0), (0, cols - w.shape[1])))


def _gdn_weights(w_in, conv, a_log, dt_bias, norm, w_out):
    hv, dv = a_log.shape[-1], norm.shape[-1]
    vd = hv * dv
    kd = (conv.shape[-1] - vd) // 2
    dk = dv
    assert w_in.shape[1] == 2 * kd + 2 * vd + 4 * hv and w_out.shape[0] == vd
    nz = 2 * kd + 2 * vd
    return dict(kd=kd, vd=vd, hk=kd // dk, hv=hv, dk=dk, dv=dv,
                w_qkvz=w_in[:, :nz].astype(BF16), w_gates=w_in[:, nz:].astype(BF16), conv=conv.astype(F32),
                a_log=a_log, dt_bias=dt_bias, norm_row=jnp.tile(norm.astype(F32), hv).reshape(1, vd),
                w_out=w_out.astype(BF16))


def _rwkv_weights(mix, w_rkv, w0, w1, w2, a0, a1, a2, g1, g2, k_k, k_a, r_k, ln_w, ln_b, w_o):
    dm = w_o.shape[0]
    assert w1.shape[-1] <= LANES and a1.shape[-1] <= LANES and r_k.size == dm and r_k.shape[-1] == RWKV_N
    seg = max(2 * LANES, -(-g1.shape[1] // LANES) * LANES)
    cat = lambda ws: _pad_cols(jnp.concatenate([_pad_cols(ws[d], LANES) for d in range(2)], axis=1), seg)
    row = lambda z: z.reshape(1, dm).astype(F32)
    mix = mix.astype(F32)
    return dict(lora_seg=seg,
                mix_rkv=jnp.stack([mix[0], mix[2], mix[3]]).reshape(3, 1, dm),
                mix_lora=jnp.stack([mix[1], mix[4], mix[5]]).reshape(3, 1, dm),
                w_rkv=jnp.concatenate([w_rkv[0], w_rkv[1], w_rkv[2]], axis=1).astype(BF16),
                w_lora=jnp.concatenate([cat(w1), cat(a1), _pad_cols(g1, seg)], axis=1).astype(BF16),
                w2=[_pad_rows(w2[d], LANES).astype(BF16) for d in range(2)],
                a2=[_pad_rows(a2[d], LANES).astype(BF16) for d in range(2)],
                g2=_pad_rows(g2, seg).astype(BF16), w0=w0.astype(F32), a0=a0.astype(F32),
                k_k=row(k_k), k_a=row(k_a), r_k=row(r_k), ln_w=row(ln_w), ln_b=row(ln_b), w_o=w_o.astype(BF16))


def _trunk(x, p, layers, stacked, alpha):
    b, t, dm = x.shape
    p_all = p.reshape(-1, p.shape[-1])
    for i, lyr in enumerate(layers):
        mixer = _gdn_mixer_ln if lyr["kind"] == "gdn" else _rwkv_mixer_ln
        x2 = mixer(x, lyr["mixer"], alpha, lyr["ln_g"][0:1], lyr["ln_b"][0:1])
        x2 = _mlp_ln(x2, stacked["w_up"], stacked["w_down"], i, alpha, lyr["ln_g"][1:2], lyr["ln_b"][1:2])
        x2 = _ple(x2, p_all, i, stacked["w_proj"], stacked["w_gate"])
        x = x2.reshape(b, t, dm)
    return x


def kernel(x_prompt, x_sample, p_prompt, p_sample, gdn_w_in, gdn_conv, gdn_a_log, gdn_dt_bias, gdn_norm, gdn_w_out, rwkv_mix, rwkv_w_rkv, rwkv_w0, rwkv_w1, rwkv_w2, rwkv_a0, rwkv_a1, rwkv_a2, rwkv_g1, rwkv_g2, rwkv_k_k, rwkv_k_a, rwkv_r_k, rwkv_ln_w, rwkv_ln_b, rwkv_w_o, ln_g, ln_b, mlp_w_up, mlp_w_down, ple_w_proj, ple_w_gate):
    depth = ln_g.shape[0]
    alpha = (2 * depth) ** 0.25
    layers = []
    for i in range(depth):
        j = i // 2
        if i % 2 == 0:
            kind = "gdn"
            mixer = _gdn_weights(gdn_w_in[j], gdn_conv[j], gdn_a_log[j], gdn_dt_bias[j], gdn_norm[j], gdn_w_out[j])
        else:
            kind = "rwkv"
            mixer = _rwkv_weights(rwkv_mix[j], rwkv_w_rkv[j], rwkv_w0[j], rwkv_w1[j], rwkv_w2[j], rwkv_a0[j],
                                  rwkv_a1[j], rwkv_a2[j], rwkv_g1[j], rwkv_g2[j], rwkv_k_k[j], rwkv_k_a[j],
                                  rwkv_r_k[j], rwkv_ln_w[j], rwkv_ln_b[j], rwkv_w_o[j])
        layers.append(dict(kind=kind, mixer=mixer, ln_g=ln_g[i].astype(F32), ln_b=ln_b[i].astype(F32)))
    stack = lambda w: w.reshape(-1, w.shape[-1]).astype(BF16)
    stacked = dict(w_up=stack(mlp_w_up), w_down=stack(mlp_w_down), w_proj=stack(ple_w_proj),
                   w_gate=stack(ple_w_gate))
    y_prompt = _trunk(x_prompt, p_prompt, layers, stacked, alpha)
    y_sample = _trunk(x_sample, p_sample, layers, stacked, alpha)
    return (y_prompt, y_sample)
```

```python
import functools
import math

import jax
import jax.numpy as jnp
from jax import lax
from jax.experimental import pallas as pl
from jax.experimental.pallas import tpu as pltpu

F32 = jnp.float32
BF16 = jnp.bfloat16

LANES = 128
HALO = 8
TIME_TILE = 128
GDN_CHUNK = TIME_TILE
RWKV_CHUNK = 64
GDN_PAIRS = 8
RWKV_BLOCKS = 16
INV_BASE = 8
VMEM_LIMIT = 56 * 1024 * 1024

LN_EPS = 1e-5
RWKV_GN_EPS = 64e-5
RWKV_N = 64
GDN_CONV = 5


def _cparams(sem):
    return pltpu.CompilerParams(dimension_semantics=sem, vmem_limit_bytes=VMEM_LIMIT)


def _mm_kernel(*refs, na, nrk, ne, nrn, nout, nk, prologue, epilogue):
    a_refs = refs[:na]
    rk_refs = refs[na:na + nrk]
    w_ref = refs[na + nrk]
    e_refs = refs[na + nrk + 1:na + nrk + 1 + ne]
    rn_refs = refs[na + nrk + 1 + ne:na + nrk + 1 + ne + nrn]
    out_refs = refs[na + nrk + 1 + ne + nrn:na + nrk + 1 + ne + nrn + nout]

    a = prologue(*[r[...] for r in a_refs], *[r[...] for r in rk_refs])
    part = jnp.dot(a, w_ref[...], preferred_element_type=F32)

    def finish(acc):
        outs = epilogue(acc, *[r[...] for r in e_refs], *[r[...] for r in rn_refs])
        for o_ref, o in zip(out_refs, outs):
            o_ref[...] = o.astype(o_ref.dtype)

    if nk == 1:
        finish(part)
        return

    acc_ref = refs[-1]
    k = pl.program_id(2)

    @pl.when(k == 0)
    def _():
        acc_ref[...] = part

    @pl.when(k > 0)
    def _():
        acc_ref[...] += part

    @pl.when(k == nk - 1)
    def _():
        finish(acc_ref[...])


def _cast_bf16(a):
    return a.astype(BF16)


def _identity_epilogue(acc):
    return (acc,)


def _fused_matmul(a_ins, w, *, a_col0=None, rowk_ins=(), rowk_seg_tiles=None, e_ins=(), rown_ins=(),
                  extra_ins=(), prologue=_cast_bf16, epilogue=_identity_epilogue, out_dtypes=(F32,),
                  tm=512, tn=512, tk=512, kdim=None, w_row0=0, name="mm"):
    m = a_ins[0].shape[0]
    n = w.shape[1]
    kdim = w.shape[0] if kdim is None else kdim
    tm, tn, tk = min(tm, m), min(tn, n), min(tk, kdim)
    assert m % tm == 0 and n % tn == 0 and kdim % tk == 0 and w_row0 % tk == 0, (m, n, kdim, tm, tn, tk, w_row0)
    nk = kdim // tk
    w_blk0 = w_row0 // tk
    a_col0 = [0] * len(a_ins) if a_col0 is None else a_col0
    assert all(c0 % tk == 0 for c0 in a_col0)

    def a_spec(c0):
        return pl.BlockSpec((tm, tk), lambda i, j, k: (i, c0 // tk + k))

    if rowk_seg_tiles is None:
        rowk_spec = pl.BlockSpec((1, tk), lambda i, j, k: (0, k))
    else:
        rowk_spec = pl.BlockSpec((None, 1, tk), lambda i, j, k: (j // rowk_seg_tiles, 0, k))
    in_specs = ([a_spec(c0) for c0 in a_col0]
                + [rowk_spec for _ in rowk_ins]
                + [pl.BlockSpec((tk, tn), lambda i, j, k: (w_blk0 + k, j))]
                + [pl.BlockSpec((tm, tn), lambda i, j, k: (i, j)) for _ in e_ins]
                + [pl.BlockSpec((1, tn), lambda i, j, k: (0, j)) for _ in rown_ins]
                + [pl.BlockSpec(shape(tm, tn), imap) for _, shape, imap in extra_ins])
    out_specs = [pl.BlockSpec((tm, tn), lambda i, j, k: (i, j)) for _ in out_dtypes]
    out_shape = [jax.ShapeDtypeStruct((m, n), dt) for dt in out_dtypes]
    kern = functools.partial(_mm_kernel, na=len(a_ins), nrk=len(rowk_ins), ne=len(e_ins),
                             nrn=len(rown_ins) + len(extra_ins), nout=len(out_dtypes), nk=nk,
                             prologue=prologue, epilogue=epilogue)
    outs = pl.pallas_call(
        kern,
        grid=(m // tm, n // tn, nk),
        in_specs=in_specs,
        out_specs=out_specs,
        out_shape=out_shape,
        scratch_shapes=[pltpu.VMEM((tm, tn), F32)] if nk > 1 else [],
        compiler_params=_cparams(("parallel", "parallel", "arbitrary")),
        name=name,
    )(*a_ins, *rowk_ins, w, *e_ins, *rown_ins, *[x for x, _, _ in extra_ins])
    return outs


def _ln_epilogue(alpha):
    def epilogue(acc, xres, g, b):
        y = alpha * xres + acc
        mu = jnp.mean(y, axis=-1, keepdims=True)
        yc = y - mu
        var = jnp.mean(yc * yc, axis=-1, keepdims=True)
        return (yc * lax.rsqrt(var + LN_EPS) * g + b,)
    return epilogue


def _halo_specs(tt, tc, t_total):
    r = tt // HALO
    last = t_total // HALO - 1
    main = pl.BlockSpec((1, tt, tc), lambda b, i, c: (b, i, c))
    prev = pl.BlockSpec((1, HALO, tc), lambda b, i, c: (b, jnp.maximum(i * r - 1, 0), c))
    nxt = pl.BlockSpec((1, HALO, tc), lambda b, i, c: (b, jnp.minimum((i + 1) * r, last), c))
    return main, prev, nxt


def _fill_halo(scr, x_ref, xp_ref, xn_ref, tt, nt):
    i = pl.program_id(1)
    scr[0:HALO, :] = jnp.where(i > 0, xp_ref[0], 0.0)
    scr[HALO:HALO + tt, :] = x_ref[0]
    scr[HALO + tt:2 * HALO + tt, :] = jnp.where(i < nt - 1, xn_ref[0], 0.0)


def _gdn_conv_kernel(x_ref, xp_ref, xn_ref, w_ref, o_ref, scr, *, tt, tc, nt, nq_tiles, nqk_tiles, q_scale):
    _fill_halo(scr, x_ref, xp_ref, xn_ref, tt, nt)
    pad = (GDN_CONV - 1) // 2
    acc = None
    for j in range(GDN_CONV):
        term = w_ref[j:j + 1, :] * scr[pl.ds(HALO - pad + j, tt), :]
        acc = term if acc is None else acc + term
    y = acc * jax.nn.sigmoid(acc)
    c = pl.program_id(2)
    is_qk = c < nqk_tiles
    scale = jnp.where(c < nq_tiles, q_scale, 1.0).astype(F32)
    for hh in range(tc // LANES):
        blk = y[:, hh * LANES:(hh + 1) * LANES]
        ss = jnp.sum(blk * blk, axis=-1, keepdims=True)
        nrm = blk * (lax.rsqrt(ss + 1e-6) * scale)
        o_ref[0, :, hh * LANES:(hh + 1) * LANES] = jnp.where(is_qk, nrm, blk)


def _gdn_conv(qkv, conv_w, kd, vd, dk, tt=512, tc=2048):
    b, t, _ = qkv.shape
    ch = 2 * kd + vd
    tt = min(tt, t)
    assert t % tt == 0 and ch % tc == 0 and kd % tc == 0 and dk == LANES
    main, prev, nxt = _halo_specs(tt, tc, t)
    kern = functools.partial(_gdn_conv_kernel, tt=tt, tc=tc, nt=t // tt, nq_tiles=kd // tc,
                             nqk_tiles=2 * kd // tc, q_scale=dk ** -0.5)
    return pl.pallas_call(
        kern,
        grid=(b, t // tt, ch // tc),
        in_specs=[main, prev, nxt, pl.BlockSpec((GDN_CONV, tc), lambda b_, i, c: (0, c))],
        out_specs=pl.BlockSpec((1, tt, tc), lambda b_, i, c: (b_, i, c)),
        out_shape=jax.ShapeDtypeStruct((b, t, ch), F32),
        scratch_shapes=[pltpu.VMEM((tt + 2 * HALO, tc), F32)],
        compiler_params=_cparams(("parallel", "parallel", "parallel")),
        name="gdn_conv",
    )(qkv, qkv, qkv, conv_w)


def _chunk_tri_masks(rows, chunk):
    ri = lax.broadcasted_iota(jnp.int32, (rows, rows), 0)
    ci = lax.broadcasted_iota(jnp.int32, (rows, rows), 1)
    same = (ri // chunk) == (ci // chunk)
    return same & (ci <= ri), same & (ci >= ri)


def _softplus(x):
    return jnp.maximum(x, 0.0) + jnp.log1p(jnp.exp(-jnp.abs(x)))


def _zero_one(mask):
    return jnp.where(mask, 1.0, 0.0).astype(BF16)


def _masked_sums(zero_ones, x):
    hi = x.astype(BF16)
    rest = x - hi.astype(F32)
    mid = rest.astype(BF16)
    terms = (hi, mid, (rest - mid.astype(F32)).astype(BF16))
    outs = []
    for m in zero_ones:
        parts = [jnp.dot(m, t, preferred_element_type=F32) for t in terms]
        outs.append(parts[0] + parts[1] + parts[2])
    return outs


def _gdn_gates_kernel(g_ref, alog_ref, dtb_ref, o_ref, ot_ref, *, hv):
    x = g_ref[0]
    g = -jnp.exp(alog_ref[...]) * _softplus(x + dtb_ref[...])
    gc_f, gc_b = _masked_sums([_zero_one(m) for m in _chunk_tri_masks(TIME_TILE, GDN_CHUNK)], g)
    lane = lax.broadcasted_iota(jnp.int32, x.shape, 1)
    out = jnp.where(lane < hv, gc_f, jnp.where(lane < 2 * hv, gc_b, jax.nn.sigmoid(x)))
    o_ref[0] = out
    ot_ref[0] = out.T


def _gdn_gates(gates, a_log, dt_bias, hv):
    b, t, ch = gates.shape
    assert ch == LANES == 4 * hv and t % TIME_TILE == 0
    pad = jnp.zeros((1, 2 * hv), F32)
    alog = jnp.concatenate([a_log.reshape(1, 2 * hv).astype(F32), pad], axis=1)
    dtb = jnp.concatenate([dt_bias.reshape(1, 2 * hv).astype(F32), pad], axis=1)
    row = pl.BlockSpec((1, LANES), lambda b_, i: (0, 0))
    return pl.pallas_call(
        functools.partial(_gdn_gates_kernel, hv=hv),
        grid=(b, t // TIME_TILE),
        in_specs=[pl.BlockSpec((1, TIME_TILE, LANES), lambda b_, i: (b_, i, 0)), row, row],
        out_specs=[pl.BlockSpec((1, TIME_TILE, LANES), lambda b_, i: (b_, i, 0)),
                   pl.BlockSpec((1, LANES, TIME_TILE), lambda b_, i: (b_, 0, i))],
        out_shape=[jax.ShapeDtypeStruct((b, t, LANES), F32), jax.ShapeDtypeStruct((b, LANES, t), F32)],
        compiler_params=_cparams(("parallel", "parallel")),
        name="gdn_gates",
    )(gates, alog, dtb)


def _mm(a, b):
    return jnp.dot(a.astype(BF16), b.astype(BF16), preferred_element_type=F32)


def _mm_nt(a, b):
    return lax.dot_general(a.astype(BF16), b.astype(BF16), (((1,), (1,)), ((), ())), preferred_element_type=F32)


def _mm_tn(a, b):
    return lax.dot_general(a.astype(BF16), b.astype(BF16), (((0,), (0,)), ((), ())), preferred_element_type=F32)


def _half_rows(x, s, half):
    return jnp.concatenate([x[(2 * b + half) * s:(2 * b + half + 1) * s] for b in range(x.shape[0] // (2 * s))],
                           axis=0)


def _with_half_rows(x, new, s, half):
    parts = []
    for b in range(x.shape[0] // (2 * s)):
        keep = x[(2 * b + 1 - half) * s:(2 * b + 2 - half) * s]
        repl = new[b * s:(b + 1) * s]
        parts += [keep, repl] if half else [repl, keep]
    return jnp.concatenate(parts, axis=0)


def _tri_inverses(ms, uppers, size, top):
    ri = lax.broadcasted_iota(jnp.int32, (size, size), 0)
    ci = lax.broadcasted_iota(jnp.int32, (size, size), 1)
    eye = (ri == ci).astype(F32)
    zeros = jnp.zeros((size, size), F32)
    base_bits = int(math.log2(INV_BASE))
    base = (ri >> base_bits) == (ci >> base_bits)
    xps = [jnp.where(base, -m, 0.0) for m in ms]
    ps = [eye + x for x in xps]
    xps = [_mm(xp, xp) for xp in xps]
    for it in range(base_bits - 1):
        if it < base_bits - 2:
            both = [_mm(jnp.concatenate([p, xp], axis=0), xp) for p, xp in zip(ps, xps)]
            ps = [p + b[:size] for p, b in zip(ps, both)]
            xps = [b[size:] for b in both]
        else:
            ps = [p + _mm(p, xp) for p, xp in zip(ps, xps)]
    halves = [0 if up else 1 for up in uppers]
    bits = base_bits
    while (1 << bits) < top:
        s = 1 << bits
        pair = ((ri >> (bits + 1)) == (ci >> (bits + 1))) & ((ri >> bits) != (ci >> bits))
        qs = [_mm(_half_rows(jnp.where(pair, m, 0.0), s, h), p) for m, p, h in zip(ms, ps, halves)]
        rs = [_mm(_half_rows(p, s, h), _with_half_rows(zeros, q, s, h)) for p, q, h in zip(ps, qs, halves)]
        ps = [_with_half_rows(p, _half_rows(p, s, h) - r, s, h) for p, r, h in zip(ps, rs, halves)]
        bits += 1
    return ps


def _pick_lane(tile, lane_idx):
    lane = lax.broadcasted_iota(jnp.int32, tile.shape, 1)
    return jnp.sum(jnp.where(lane == lane_idx, tile, 0.0), axis=1, keepdims=True)


def _gdn_chunk_kernel(qf_ref, kf_ref, vf_ref, gf_ref, gtf_ref, qb_ref, kb_ref, vb_ref, gb_ref, gtb_ref,
                      of_ref, ob_ref, sf_ref, sb_ref, *, rep, hv, npairs):
    n = pl.program_id(2)
    hg = pl.program_id(1)

    @pl.when(n == 0)
    def _():
        sf_ref[...] = jnp.zeros_like(sf_ref)
        sb_ref[...] = jnp.zeros_like(sb_ref)

    c = GDN_CHUNK
    ri = lax.broadcasted_iota(jnp.int32, (c, c), 0)
    ci = lax.broadcasted_iota(jnp.int32, (c, c), 1)
    dirs = ((False, qf_ref, kf_ref, vf_ref, gf_ref, gtf_ref, of_ref, sf_ref),
            (True, qb_ref, kb_ref, vb_ref, gb_ref, gtb_ref, ob_ref, sb_ref))

    pairs = []
    for rev, q_ref, k_ref, *_ in dirs:
        for i in range(npairs):
            pairs.append((q_ref[0, :, i * LANES:(i + 1) * LANES], k_ref[0, :, i * LANES:(i + 1) * LANES]))
    qks = [_mm_nt(q, k) for q, k in pairs]
    kks = [_mm_nt(k, k) for _, k in pairs]

    probs = []
    for d, (rev, q_ref, k_ref, v_ref, g_ref, gt_ref, o_ref, s_ref) in enumerate(dirs):
        incl = (ci >= ri) if rev else (ci <= ri)
        strict = (ci > ri) if rev else (ci < ri)
        g_tile = g_ref[0]
        end = 0 if rev else c - 1
        for i in range(npairs):
            q, k = pairs[d * npairs + i]
            for r in range(rep):
                slot = i * rep + r
                h = (hg * npairs + i) * rep + r
                gcol = _pick_lane(g_tile, d * hv + h)
                bcol = _pick_lane(g_tile, 2 * hv + d * hv + h)
                grow = gt_ref[0, pl.ds(d * hv + h, 1), :]
                decay = jnp.where(incl, jnp.exp(jnp.where(incl, gcol - grow, 0.0)), 0.0)
                g_last = gcol[end:end + 1]
                egc = jnp.exp(gcol)
                v = v_ref[0, :, slot * LANES:(slot + 1) * LANES]
                probs.append(dict(
                    q=q, k=k, egc=egc, g_last=g_last, slot=slot, o_ref=o_ref, s_ref=s_ref, rev=rev,
                    a_kk=jnp.where(strict, kks[d * npairs + i] * bcol * decay, 0.0),
                    a_qk=qks[d * npairs + i] * decay,
                    rhs=jnp.concatenate([v * bcol, k * (bcol * egc)], axis=1),
                    k_dec=k * jnp.exp(g_last - gcol)))

    t_invs = _tri_inverses([p["a_kk"] for p in probs], [p["rev"] for p in probs], c, c)
    uws = [_mm(t, p["rhs"]) for t, p in zip(t_invs, probs)]
    aunms = [_mm(jnp.concatenate([p["a_qk"], p["k_dec"].T], axis=0), uw) for p, uw in zip(probs, uws)]
    aus = [x[:c] for x in aunms]
    nms = [x[c:] for x in aunms]

    states = [p["s_ref"][p["slot"]] for p in probs]
    lhss = [jnp.concatenate([p["q"] * p["egc"] - au[:, LANES:], nm[:, LANES:]], axis=0)
            for p, au, nm in zip(probs, aus, nms)]
    ress = [_mm(lhs, s) for lhs, s in zip(lhss, states)]
    for p, s, au, nm, res in zip(probs, states, aus, nms, ress):
        slot = p["slot"]
        p["o_ref"][0, :, slot * LANES:(slot + 1) * LANES] = (res[:c] + au[:, :LANES]).astype(p["o_ref"].dtype)
        p["s_ref"][slot] = s * jnp.exp(p["g_last"]) - res[c:] + nm[:, :LANES]


def _gdn_chunk(qkv, gcb, gcbt, hk, hv, dk, dv):
    b, t, _ = qkv.shape
    rep = hv // hk
    npairs = math.gcd(GDN_PAIRS, hk)
    assert dk == LANES and dv == LANES and t % TIME_TILE == 0 and GDN_CHUNK == TIME_TILE
    assert (2 * hk) % (npairs * rep) == 0
    nt = t // TIME_TILE
    qw, vw = npairs * LANES, npairs * rep * LANES
    kblk0 = hk // npairs
    vblk0 = 2 * hk // (npairs * rep)

    def specs(tile):
        return [pl.BlockSpec((1, TIME_TILE, qw), lambda b_, h, n: (b_, tile(n), h)),
                pl.BlockSpec((1, TIME_TILE, qw), lambda b_, h, n: (b_, tile(n), kblk0 + h)),
                pl.BlockSpec((1, TIME_TILE, vw), lambda b_, h, n: (b_, tile(n), vblk0 + h)),
                pl.BlockSpec((1, TIME_TILE, LANES), lambda b_, h, n: (b_, tile(n), 0)),
                pl.BlockSpec((1, LANES, TIME_TILE), lambda b_, h, n: (b_, 0, tile(n)))]

    fwd = lambda n: n
    bwd = lambda n: nt - 1 - n
    out_f = pl.BlockSpec((1, TIME_TILE, vw), lambda b_, h, n: (b_, n, h))
    out_b = pl.BlockSpec((1, TIME_TILE, vw), lambda b_, h, n: (b_, nt - 1 - n, h))
    shp = jax.ShapeDtypeStruct((b, t, hv * dv), BF16)
    return pl.pallas_call(
        functools.partial(_gdn_chunk_kernel, rep=rep, hv=hv, npairs=npairs),
        grid=(b, hk // npairs, nt),
        in_specs=specs(fwd) + specs(bwd),
        out_specs=[out_f, out_b],
        out_shape=[shp, shp],
        scratch_shapes=[pltpu.VMEM((npairs * rep, dk, dv), F32), pltpu.VMEM((npairs * rep, dk, dv), F32)],
        compiler_params=_cparams(("parallel", "parallel", "arbitrary")),
        name="gdn_chunk",
    )(qkv, qkv, qkv, gcb, gcbt, qkv, qkv, qkv, gcb, gcbt)


def _gdn_out_prologue(o_f, o_b, z, nw):
    parts = []
    for hh in range(o_f.shape[1] // LANES):
        sl = slice(hh * LANES, (hh + 1) * LANES)
        o = o_f[:, sl].astype(F32) + o_b[:, sl].astype(F32)
        o = o * lax.rsqrt(jnp.mean(o * o, axis=-1, keepdims=True) + 1e-6) * nw[:, sl]
        zz = z[:, sl]
        parts.append((o * (zz * jax.nn.sigmoid(zz))).astype(BF16))
    return jnp.concatenate(parts, axis=1) if len(parts) > 1 else parts[0]


def _gdn_mixer_ln(x, w, alpha, ln_g, ln_b):
    b, t, dm = x.shape
    m = b * t
    x2 = x.reshape(m, dm)
    kd, vd, hk, hv, dk, dv = w["kd"], w["vd"], w["hk"], w["hv"], w["dk"], w["dv"]
    qkvz, = _fused_matmul([x2], w["w_qkvz"], tm=1024, tn=1024, tk=dm, name="gdn_in")
    gates, = _fused_matmul([x2], w["w_gates"], tm=1024, tn=LANES, tk=dm, name="gdn_in_gates")
    qkv = _gdn_conv(qkvz.reshape(b, t, -1), w["conv"], kd, vd, dk)
    gcb, gcbt = _gdn_gates(gates.reshape(b, t, -1), w["a_log"], w["dt_bias"], hv)
    o_f, o_b = _gdn_chunk(qkv, gcb, gcbt, hk, hv, dk, dv)
    out, = _fused_matmul(
        [o_f.reshape(m, vd), o_b.reshape(m, vd), qkvz], w["w_out"], a_col0=[0, 0, 2 * kd + vd],
        rowk_ins=[w["norm_row"]], e_ins=[x2], rown_ins=[ln_g, ln_b],
        prologue=_gdn_out_prologue, epilogue=_ln_epilogue(alpha), tm=512, tn=dm, tk=1024, name="gdn_out")
    return out


def _shift_kernel(x_ref, xp_ref, xn_ref, o_ref, scr, *, tt, nt):
    _fill_halo(scr, x_ref, xp_ref, xn_ref, tt, nt)
    o_ref[0] = 0.5 * (scr[pl.ds(HALO - 1, tt), :] + scr[pl.ds(HALO + 1, tt), :]) - x_ref[0]


def _token_shift(x, tt=512, tc=1024):
    b, t, ch = x.shape
    tt, tc = min(tt, t), min(tc, ch)
    assert t % tt == 0 and ch % tc == 0
    main, prev, nxt = _halo_specs(tt, tc, t)
    return pl.pallas_call(
        functools.partial(_shift_kernel, tt=tt, nt=t // tt),
        grid=(b, t // tt, ch // tc),
        in_specs=[main, prev, nxt],
        out_specs=pl.BlockSpec((1, tt, tc), lambda b_, i, c: (b_, i, c)),
        out_shape=jax.ShapeDtypeStruct((b, t, ch), F32),
        scratch_shapes=[pltpu.VMEM((tt + 2 * HALO, tc), F32)],
        compiler_params=_cparams(("parallel", "parallel", "parallel")),
        name="rwkv_shift",
    )(x, x, x)


def _mix_prologue(x, xx, mix):
    return (x + xx * mix).astype(BF16)


def _same_head_ones():
    rl = lax.broadcasted_iota(jnp.int32, (LANES, LANES), 0)
    cl = lax.broadcasted_iota(jnp.int32, (LANES, LANES), 1)
    return jnp.where((rl // RWKV_N) == (cl // RWKV_N), 1.0, 0.0).astype(BF16)


def _head_sum(x, ones):
    hi = x.astype(BF16)
    lo = (x - hi.astype(F32)).astype(BF16)
    return (jnp.dot(hi, ones, preferred_element_type=F32) + jnp.dot(lo, ones, preferred_element_type=F32))


def _rwkv_chunk_kernel(rf_ref, kf_ref, vf_ref, lwf_ref, laf_ref, rb_ref, kb_ref, vb_ref, lwb_ref, lab_ref,
                       w2f_ref, a2f_ref, w0f_ref, a0f_ref, w2b_ref, a2b_ref, w0b_ref, a0b_ref,
                       kk_ref, ka_ref, rk_ref, yf_ref, bf_ref, yb_ref, bb_ref, gf_ref, gb_ref, *, nblk):
    n = pl.program_id(2)

    @pl.when(n == 0)
    def _():
        gf_ref[...] = jnp.zeros_like(gf_ref)
        gb_ref[...] = jnp.zeros_like(gb_ref)

    c, nn, tt = RWKV_CHUNK, RWKV_N, TIME_TILE
    nch, nh = tt // c, LANES // nn
    row = lax.broadcasted_iota(jnp.int32, (tt, LANES), 0)
    ri = lax.broadcasted_iota(jnp.int32, (tt, tt), 0)
    ci = lax.broadcasted_iota(jnp.int32, (tt, tt), 1)
    same = (ri // c) == (ci // c)
    ones = _same_head_ones()
    dirs = ((False, rf_ref, kf_ref, vf_ref, lwf_ref, laf_ref, w2f_ref, a2f_ref, w0f_ref, a0f_ref,
             yf_ref, bf_ref, gf_ref),
            (True, rb_ref, kb_ref, vb_ref, lwb_ref, lab_ref, w2b_ref, a2b_ref, w0b_ref, a0b_ref,
             yb_ref, bb_ref, gb_ref))

    probs, blocks = [], []
    for rev, r_ref, k_ref, v_ref, lw_ref, la_ref, w2_ref, a2_ref, w0_ref, a0_ref, y_ref, bonus_ref, g_ref in dirs:
        incl = same & ((ci >= ri) if rev else (ci <= ri))
        strict = same & ((ci > ri) if rev else (ci < ri))
        incl_ones = _zero_one(incl)
        ends = [cc * c if rev else (cc + 1) * c - 1 for cc in range(nch)]
        lw_all = -jnp.exp(-_softplus(-(w0_ref[...] + jnp.dot(lw_ref[0], w2_ref[...], preferred_element_type=F32)))
                          - 0.5)
        a_all = jax.nn.sigmoid(a0_ref[...] + jnp.dot(la_ref[0], a2_ref[...], preferred_element_type=F32))
        for blk in range(nblk):
            ls = slice(blk * LANES, (blk + 1) * LANES)
            r, k, v, lw, a = r_ref[0, :, ls], k_ref[0, :, ls], v_ref[0, :, ls], lw_all[:, ls], a_all[:, ls]
            kkk = k * kk_ref[:, ls]
            kk = kkk * lax.rsqrt(_head_sum(kkk * kkk, ones) + 1e-6)
            kd = k * (1.0 + (a - 1.0) * ka_ref[:, ls])
            av = -kk
            bv = kk * a
            bonus_ref[0, :, ls] = (_head_sum(r * kd * rk_ref[:, ls], ones) * v).astype(bonus_ref.dtype)
            gc, = _masked_sums([incl_ones], lw)
            gends = [gc[e:e + 1] for e in ends]
            gend_tile = gends[0]
            for cc in range(1, nch):
                gend_tile = jnp.where(row >= cc * c, gends[cc], gend_tile)
            ee = jnp.exp(gend_tile - gc)
            ei = jnp.exp(-gc)
            at, kt, bt, rt = av * jnp.exp(gc - lw), kd * ei, bv * ei, r * jnp.exp(gc)
            kh, bh = kd * ee, bv * ee
            blocks.append(dict(rev=rev, y_ref=y_ref, g_ref=g_ref, blk=blk, rt=rt, v=v, kh=kh, bh=bh,
                               egend=[jnp.exp(g) for g in gends], first=len(probs)))
            for j in range(nh):
                hs = slice(j * nn, (j + 1) * nn)
                probs.append(dict(rev=rev, incl=incl, strict=strict, at=at[:, hs], kt=kt[:, hs], bt=bt[:, hs],
                                  rt=rt[:, hs], v=v[:, hs]))

    bigs = [_mm_nt(jnp.concatenate([p["at"], p["rt"]], axis=0), jnp.concatenate([p["bt"], p["kt"]], axis=0))
            for p in probs]
    for p, big in zip(probs, bigs):
        p["a_ab"] = jnp.where(p["strict"], big[:tt, :tt], 0.0)
        p["a_ak"] = jnp.where(p["strict"], big[:tt, tt:], 0.0)
        p["m_rb"] = jnp.where(p["incl"], big[tt:, :tt], 0.0)
        p["m_rk"] = jnp.where(p["incl"], big[tt:, tt:], 0.0)
    avyvs = [_mm(jnp.concatenate([p["a_ak"], p["m_rk"]], axis=0), p["v"]) for p in probs]
    t_invs = _tri_inverses([-p["a_ab"] for p in probs], [p["rev"] for p in probs], tt, c)
    tauvs = [_mm(t, jnp.concatenate([p["at"], av[:tt]], axis=1)) for t, p, av in zip(t_invs, probs, avyvs)]
    mtmus = [_mm(p["m_rb"], tauv) for p, tauv in zip(probs, tauvs)]

    for blk in blocks:
        hd = range(blk["first"], blk["first"] + nh)
        blk["ta"] = jnp.concatenate([tauvs[i][:, :nn] for i in hd], axis=1)
        blk["uv"] = jnp.concatenate([tauvs[i][:, nn:] for i in hd], axis=1)
        blk["rq"] = blk["rt"] + jnp.concatenate([mtmus[i][:, :nn] for i in hd], axis=1)
        blk["yc"] = jnp.concatenate([avyvs[i][tt:] + mtmus[i][:, nn:] for i in hd], axis=1)

    def rows(cc):
        return slice(cc * c, (cc + 1) * c)

    rl = lax.broadcasted_iota(jnp.int32, (LANES, LANES), 0)
    cl = lax.broadcasted_iota(jnp.int32, (LANES, LANES), 1)
    same_head = (rl // nn) == (cl // nn)
    mps = [[_mm_tn(blk["ta"][rows(cc)], blk["bh"][rows(cc)]) for cc in range(nch)] for blk in blocks]
    n2s = [[_mm_tn(jnp.concatenate([blk["v"][rows(cc)], blk["uv"][rows(cc)]], axis=0),
                   jnp.concatenate([blk["kh"][rows(cc)], blk["bh"][rows(cc)]], axis=0)) for cc in range(nch)]
           for blk in blocks]
    states = [blk["g_ref"][blk["blk"]] for blk in blocks]
    ys = [[None] * nch for _ in blocks]
    for step in range(nch):
        ccs = [nch - 1 - step if blk["rev"] else step for blk in blocks]
        yparts = [_mm_nt(blk["rq"][rows(cc)], g) for blk, g, cc in zip(blocks, states, ccs)]
        gms = [_mm(g, jnp.where(same_head, mp[cc], 0.0)) for g, mp, cc in zip(states, mps, ccs)]
        for i, (blk, cc) in enumerate(zip(blocks, ccs)):
            ys[i][cc] = yparts[i] + blk["yc"][rows(cc)]
            states[i] = states[i] * blk["egend"][cc] + gms[i] + jnp.where(same_head, n2s[i][cc], 0.0)
    for blk, g, y in zip(blocks, states, ys):
        blk["g_ref"][blk["blk"]] = g
        blk["y_ref"][0, :, blk["blk"] * LANES:(blk["blk"] + 1) * LANES] = (
            jnp.concatenate(y, axis=0).astype(blk["y_ref"].dtype))


def _rwkv_chunk(rkv, lora, w, dm):
    b, t, _ = rkv.shape
    assert t % TIME_TILE == 0 and dm % LANES == 0 and LANES % RWKV_N == 0 and TIME_TILE % RWKV_CHUNK == 0
    nblk = math.gcd(RWKV_BLOCKS, dm // LANES)
    nt = t // TIME_TILE
    width = nblk * LANES
    nw = dm // width
    seg_blocks = w["lora_seg"] // LANES
    fwd = lambda n: n
    bwd = lambda n: nt - 1 - n

    def tiles(order, d):
        rkv_specs = [pl.BlockSpec((1, TIME_TILE, width), lambda b_, h, n, s=s: (b_, order(n), s * nw + h))
                     for s in range(3)]
        lora_specs = [pl.BlockSpec((1, TIME_TILE, LANES), lambda b_, h, n, s=s: (b_, order(n), s * seg_blocks + d))
                      for s in range(2)]
        return rkv_specs + lora_specs

    def out_tile(order):
        return pl.BlockSpec((1, TIME_TILE, width), lambda b_, h, n: (b_, order(n), h))

    wblk = pl.BlockSpec((LANES, width), lambda b_, h, n: (0, h))
    row = pl.BlockSpec((1, width), lambda b_, h, n: (0, h))
    shp = jax.ShapeDtypeStruct((b, t, dm), BF16)
    dir_weights = [z for d in range(2) for z in (w["w2"][d], w["a2"][d], w["w0"][d:d + 1], w["a0"][d:d + 1])]
    return pl.pallas_call(
        functools.partial(_rwkv_chunk_kernel, nblk=nblk),
        grid=(b, nw, nt),
        in_specs=tiles(fwd, 0) + tiles(bwd, 1) + [wblk, wblk, row, row] * 2 + [row] * 3,
        out_specs=[out_tile(fwd), out_tile(fwd), out_tile(bwd), out_tile(bwd)],
        out_shape=[shp] * 4,
        scratch_shapes=[pltpu.VMEM((nblk, LANES, LANES), F32), pltpu.VMEM((nblk, LANES, LANES), F32)],
        compiler_params=_cparams(("parallel", "parallel", "arbitrary")),
        name="rwkv_chunk",
    )(rkv, rkv, rkv, lora, lora, rkv, rkv, rkv, lora, lora, *dir_weights, w["k_k"], w["k_a"], w["r_k"])


def _rwkv_out_prologue(y_f, y_b, bn_f, bn_b, g, ln_w, ln_b):
    parts = []
    ones = _same_head_ones()
    for hh in range(y_f.shape[1] // LANES):
        sl = slice(hh * LANES, (hh + 1) * LANES)
        wkv = y_f[:, sl].astype(F32) + y_b[:, sl].astype(F32)
        mu = _head_sum(wkv, ones) * (1.0 / RWKV_N)
        wc = wkv - mu
        var = _head_sum(wc * wc, ones) * (1.0 / RWKV_N)
        o = (wc * lax.rsqrt(var + RWKV_GN_EPS) * ln_w[:, sl] + ln_b[:, sl]
             + (bn_f[:, sl].astype(F32) + bn_b[:, sl].astype(F32)))
        parts.append((o * g[:, sl]).astype(BF16))
    return jnp.concatenate(parts, axis=1) if len(parts) > 1 else parts[0]


def _rwkv_mixer_ln(x, w, alpha, ln_g, ln_b):
    b, t, dm = x.shape
    m = b * t
    x2 = x.reshape(m, dm)
    xx = _token_shift(x).reshape(m, dm)
    seg = w["lora_seg"]
    tn = min(1024, dm)
    rkv, = _fused_matmul([x2, xx], w["w_rkv"], rowk_ins=[w["mix_rkv"]], rowk_seg_tiles=dm // tn,
                         prologue=_mix_prologue, tm=1024, tn=tn, tk=dm, name="rwkv_rkv")

    def lora_epilogue(acc):
        j = pl.program_id(1)
        return (jnp.where(j == 0, jnp.tanh(acc), jnp.where(j == 2, jax.nn.sigmoid(acc), acc)),)

    lora, = _fused_matmul([x2, xx], w["w_lora"], rowk_ins=[w["mix_lora"]], rowk_seg_tiles=1,
                          prologue=_mix_prologue, epilogue=lora_epilogue, out_dtypes=(BF16,),
                          tm=1024, tn=seg, tk=dm, name="rwkv_lora")
    g, = _fused_matmul([lora], w["g2"], a_col0=[2 * seg], tm=1024, tn=1024, tk=seg, name="rwkv_g2")

    y_f, bn_f, y_b, bn_b = _rwkv_chunk(rkv.reshape(b, t, 3 * dm), lora.reshape(b, t, 3 * seg), w, dm)
    out, = _fused_matmul(
        [z.reshape(m, dm) for z in (y_f, y_b, bn_f, bn_b)] + [g], w["w_o"],
        rowk_ins=[w["ln_w"], w["ln_b"]], e_ins=[x2], rown_ins=[ln_g, ln_b],
        prologue=_rwkv_out_prologue, epilogue=_ln_epilogue(alpha), tm=512, tn=dm, tk=1024, name="rwkv_out")
    return out


def _relu2_epilogue(acc):
    return (jnp.square(jnp.maximum(acc, 0.0)),)


def _mlp_ln(x2, w_up, w_down, layer, alpha, ln_g, ln_b):
    dm, dff = x2.shape[1], w_up.shape[1]
    h, = _fused_matmul([x2], w_up, kdim=dm, w_row0=layer * dm, epilogue=_relu2_epilogue, out_dtypes=(BF16,),
                       tm=1024, tn=1024, tk=dm, name="mlp_up")
    out, = _fused_matmul([h], w_down, kdim=dff, w_row0=layer * dff, e_ins=[x2], rown_ins=[ln_g, ln_b],
                         epilogue=_ln_epilogue(alpha), tm=512, tn=dm, tk=2048, name="mlp_down")
    return out


def _ple_epilogue(acc, x, p, w_proj):
    return (x + jax.nn.sigmoid(acc) * jnp.dot(p.astype(BF16), w_proj, preferred_element_type=F32),)


def _ple(x2, p_all, layer, w_proj, w_gate, tm=1024):
    m, dm = x2.shape
    pdim = p_all.shape[1]
    tm = min(tm, m)
    row0 = layer * m // tm
    extra = [(p_all, lambda tm_, tn: (tm_, pdim), lambda i, j, k: (row0 + i, 0)),
             (w_proj, lambda tm_, tn: (pdim, tn), lambda i, j, k: (layer, j))]
    out, = _fused_matmul([x2], w_gate, kdim=dm, w_row0=layer * dm, e_ins=[x2], extra_ins=extra,
                         epilogue=_ple_epilogue, tm=tm, tn=1024, tk=dm, name="ple_gate")
    return out


def _pad_rows(w, rows):
    return jnp.pad(w, ((0, rows - w.shape[0]), (0, 0)))


def _pad_cols(w, cols):
    return jnp.pad(w, ((0, 0), (0, cols - w.shape[1])))


def _gdn_weights(w_in, conv, a_log, dt_bias, norm, w_out):
    hv, dv = a_log.shape[-1], norm.shape[-1]
    vd = hv * dv
    kd = (conv.shape[-1] - vd) // 2
    dk = dv
    assert w_in.shape[1] == 2 * kd + 2 * vd + 4 * hv and w_out.shape[0] == vd
    nz = 2 * kd + 2 * vd
    return dict(kd=kd, vd=vd, hk=kd // dk, hv=hv, dk=dk, dv=dv,
                w_qkvz=w_in[:, :nz].astype(BF16), w_gates=w_in[:, nz:].astype(BF16), conv=conv.astype(F32),
                a_log=a_log, dt_bias=dt_bias, norm_row=jnp.tile(norm.astype(F32), hv).reshape(1, vd),
                w_out=w_out.astype(BF16))


def _rwkv_weights(mix, w_rkv, w0, w1, w2, a0, a1, a2, g1, g2, k_k, k_a, r_k, ln_w, ln_b, w_o):
    dm = w_o.shape[0]
    assert w1.shape[-1] <= LANES and a1.shape[-1] <= LANES and r_k.size == dm and r_k.shape[-1] == RWKV_N
    seg = max(2 * LANES, -(-g1.shape[1] // LANES) * LANES)
    cat = lambda ws: _pad_cols(jnp.concatenate([_pad_cols(ws[d], LANES) for d in range(2)], axis=1), seg)
    row = lambda z: z.reshape(1, dm).astype(F32)
    mix = mix.astype(F32)
    return dict(lora_seg=seg,
                mix_rkv=jnp.stack([mix[0], mix[2], mix[3]]).reshape(3, 1, dm),
                mix_lora=jnp.stack([mix[1], mix[4], mix[5]]).reshape(3, 1, dm),
                w_rkv=jnp.concatenate([w_rkv[0], w_rkv[1], w_rkv[2]], axis=1).astype(BF16),
                w_lora=jnp.concatenate([cat(w1), cat(a1), _pad_cols(g1, seg)], axis=1).astype(BF16),
                w2=[_pad_rows(w2[d], LANES).astype(BF16) for d in range(2)],
                a2=[_pad_rows(a2[d], LANES).astype(BF16) for d in range(2)],
                g2=_pad_rows(g2, seg).astype(BF16), w0=w0.astype(F32), a0=a0.astype(F32),
                k_k=row(k_k), k_a=row(k_a), r_k=row(r_k), ln_w=row(ln_w), ln_b=row(ln_b), w_o=w_o.astype(BF16))


def _trunk(x, p, layers, stacked, alpha):
    b, t, dm = x.shape
    p_all = p.reshape(-1, p.shape[-1])
    for i, lyr in enumerate(layers):
        mixer = _gdn_mixer_ln if lyr["kind"] == "gdn" else _rwkv_mixer_ln
        x2 = mixer(x, lyr["mixer"], alpha, lyr["ln_g"][0:1], lyr["ln_b"][0:1])
        x2 = _mlp_ln(x2, stacked["w_up"], stacked["w_down"], i, alpha, lyr["ln_g"][1:2], lyr["ln_b"][1:2])
        x2 = _ple(x2, p_all, i, stacked["w_proj"], stacked["w_gate"])
        x = x2.reshape(b, t, dm)
    return x


def kernel(x_prompt, x_sample, p_prompt, p_sample, gdn_w_in, gdn_conv, gdn_a_log, gdn_dt_bias, gdn_norm, gdn_w_out, rwkv_mix, rwkv_w_rkv, rwkv_w0, rwkv_w1, rwkv_w2, rwkv_a0, rwkv_a1, rwkv_a2, rwkv_g1, rwkv_g2, rwkv_k_k, rwkv_k_a, rwkv_r_k, rwkv_ln_w, rwkv_ln_b, rwkv_w_o, ln_g, ln_b, mlp_w_up, mlp_w_down, ple_w_proj, ple_w_gate):
    depth = ln_g.shape[0]
    alpha = (2 * depth) ** 0.25
    layers = []
    for i in range(depth):
        j = i // 2
        if i % 2 == 0:
            kind = "gdn"
            mixer = _gdn_weights(gdn_w_in[j], gdn_conv[j], gdn_a_log[j], gdn_dt_bias[j], gdn_norm[j], gdn_w_out[j])
        else:
            kind = "rwkv"
            mixer = _rwkv_weights(rwkv_mix[j], rwkv_w_rkv[j], rwkv_w0[j], rwkv_w1[j], rwkv_w2[j], rwkv_a0[j],
                                  rwkv_a1[j], rwkv_a2[j], rwkv_g1[j], rwkv_g2[j], rwkv_k_k[j], rwkv_k_a[j],
                                  rwkv_r_k[j], rwkv_ln_w[j], rwkv_ln_b[j], rwkv_w_o[j])
        layers.append(dict(kind=kind, mixer=mixer, ln_g=ln_g[i].astype(F32), ln_b=ln_b[i].astype(F32)))
    stack = lambda w: w.reshape(-1, w.shape[-1]).astype(BF16)
    stacked = dict(w_up=stack(mlp_w_up), w_down=stack(mlp_w_down), w_proj=stack(ple_w_proj),
                   w_gate=stack(ple_w_gate))
    y_prompt = _trunk(x_prompt, p_prompt, layers, stacked, alpha)
    y_sample = _trunk(x_sample, p_sample, layers, stacked, alpha)
    return (y_prompt, y_sample)
```
